```python
import math
import jax, jax.numpy as jnp
from jax import lax
import numpy as np

D_MODEL = 1024
BATCH = 4
SEQ = 4096
DEPTH = 2

N_A_LAYERS = DEPTH // 2
N_B_LAYERS = DEPTH - N_A_LAYERS
N_DENSE = (DEPTH + 1) // 2
N_MOE = DEPTH // 2

CHUNK = 128
GMLP_FFN = 6 * D_MODEL
GMLP_HALF = GMLP_FFN // 2
GMLP_GROUPS = 8
GMLP_GROUP_DIM = GMLP_HALF // GMLP_GROUPS

N_HEADS = 16
N_KV_HEADS = 4
HEAD_DIM = 64
KV_REP = N_HEADS // N_KV_HEADS
WINDOW = 128
ROPE_THETA = 10000.0

D_FF_DENSE = 2816
N_EXPERTS = 8
TOP_K = 2
D_FF_EXPERT = 3584

PLE_DIM = 256

EPS = 1e-6
MASK_VALUE = -1e30

kernel_name = "yoco_gmlp_swa_sink_moe_trunk"


def rms_norm(x, g):
    xf = x.astype(jnp.float32)
    y = xf * lax.rsqrt(jnp.mean(xf * xf, axis=-1, keepdims=True) + EPS)
    return (y * g.astype(jnp.float32)).astype(x.dtype)


def layer_norm(x, g, b):
    xf = x.astype(jnp.float32)
    mu = jnp.mean(xf, axis=-1, keepdims=True)
    xc = xf - mu
    var = jnp.mean(xc * xc, axis=-1, keepdims=True)
    y = xc * lax.rsqrt(var + EPS)
    return (y * g.astype(jnp.float32) + b.astype(jnp.float32)).astype(x.dtype)


def rope(t):
    S, Dh = t.shape[1], t.shape[-1]
    freqs = ROPE_THETA ** (-jnp.arange(0, Dh, 2, dtype=jnp.float32) / Dh)
    ang = jnp.arange(S, dtype=jnp.float32)[:, None] * freqs[None, :]
    cos = jnp.cos(ang)[None, :, None, :]
    sin = jnp.sin(ang)[None, :, None, :]
    tf = t.astype(jnp.float32)
    t1, t2 = tf[..., : Dh // 2], tf[..., Dh // 2:]
    out = jnp.concatenate([t1 * cos - t2 * sin, t2 * cos + t1 * sin], axis=-1)
    return out.astype(t.dtype)


def gmlp_mixer(h, w_in, b_in, ln_g, ln_b, w_s, b_s, w_out, b_out):
    B, S, _ = h.shape
    n_chunks = S // CHUNK
    uv = jax.nn.gelu(h @ w_in + b_in)
    u, v = uv[..., :GMLP_HALF], uv[..., GMLP_HALF:]
    v = layer_norm(v, ln_g, ln_b)
    v = v.reshape(B, n_chunks, CHUNK, GMLP_GROUPS, GMLP_GROUP_DIM)
    causal = jnp.tril(jnp.ones((CHUNK, CHUNK), dtype=bool))
    ws = jnp.where(causal[None], w_s, jnp.zeros((), w_s.dtype))
    mixed = jnp.einsum('gts,bcsgd->bctgd', ws, v)
    mixed = mixed + jnp.transpose(b_s)[None, None, :, :, None]
    gated = u * mixed.reshape(B, S, GMLP_HALF)
    return gated @ w_out + b_out


def shared_kv(h, kv_norm_g, w_kv, b_kv):
    B, S, _ = h.shape
    n_blocks = S // WINDOW
    kv = rms_norm(h, kv_norm_g) @ w_kv + b_kv
    k = kv[..., : N_KV_HEADS * HEAD_DIM].reshape(B, S, N_KV_HEADS, HEAD_DIM)
    v = kv[..., N_KV_HEADS * HEAD_DIM:].reshape(B, S, N_KV_HEADS, HEAD_DIM)
    k = rope(k)

    def band(t):
        tp = jnp.pad(t, ((0, 0), (WINDOW, 0), (0, 0), (0, 0)))
        tp = tp.reshape(B, n_blocks + 1, WINDOW, N_KV_HEADS, HEAD_DIM)
        return jnp.concatenate([tp[:, :-1], tp[:, 1:]], axis=2)

    return band(k), band(v)


def swa_sink_attention(h, w_q, b_q, sinks, w_o, b_o, k_band, v_band):
    B, S, _ = h.shape
    n_blocks = S // WINDOW
    q = (h @ w_q + b_q).reshape(B, S, N_HEADS, HEAD_DIM)
    q = rope(q).reshape(B, n_blocks, WINDOW, N_KV_HEADS, KV_REP, HEAD_DIM)
    scores = jnp.einsum('bcqkrd,bcjkd->bckrqj', q, k_band).astype(jnp.float32)
    scores = scores * (1.0 / math.sqrt(HEAD_DIM))
    qi = jnp.arange(WINDOW)[:, None]
    kj = jnp.arange(2 * WINDOW)[None, :]
    in_window = (kj > qi) & (kj <= qi + WINDOW)
    key_pos = jnp.arange(n_blocks)[:, None] * WINDOW + jnp.arange(2 * WINDOW)[None, :] - WINDOW
    mask = in_window[None] & (key_pos >= 0)[:, None, :]
    scores = jnp.where(mask[None, :, None, None], scores, MASK_VALUE)
    sink = sinks.astype(jnp.float32).reshape(N_KV_HEADS, KV_REP)[None, None, :, :, None, None]
    sink = jnp.broadcast_to(sink, scores.shape[:-1] + (1,))
    probs = jax.nn.softmax(jnp.concatenate([scores, sink], axis=-1), axis=-1)[..., :-1]
    out = jnp.einsum('bckrqj,bcjkd->bcqkrd', probs.astype(v_band.dtype), v_band)
    out = out.reshape(B, S, N_HEADS * HEAD_DIM)
    return out @ w_o + b_o


def swiglu(h, w_gate, w_up, w_down):
    return (jax.nn.silu(h @ w_gate) * (h @ w_up)) @ w_down


def moe_swiglu(h, w_router, w_gate, w_up, w_down):
    B, S, D = h.shape
    t = h.reshape(B * S, D)
    logits = (t @ w_router).astype(jnp.float32)
    top_vals, top_idx = lax.top_k(logits, TOP_K)
    top_w = jax.nn.softmax(top_vals, axis=-1)
    combine = jnp.sum(jax.nn.one_hot(top_idx, N_EXPERTS, dtype=jnp.float32) * top_w[..., None], axis=1)
    combine = combine.astype(t.dtype)
    y = jnp.zeros_like(t)
    for e in range(N_EXPERTS):
        y = y + combine[:, e:e + 1] * swiglu(t, w_gate[e], w_up[e], w_down[e])
    return y.reshape(B, S, D)


def per_layer_embedding(h, p_i, w_proj, norm_g, w_gate):
    gate = jax.nn.sigmoid(rms_norm(h, norm_g) @ w_gate)
    return (p_i @ w_proj) * gate


def setup_inputs(seed: int = 0) -> dict:
    key = jax.random.key(seed)
    ks = iter(jax.random.split(key, 40))
    f32 = jnp.float32

    def nrm(shape, scale):
        return jax.random.normal(next(ks), shape, f32) * scale

    def gain(shape):
        return 1.0 + nrm(shape, 0.05)

    D = D_MODEL
    QW = N_HEADS * HEAD_DIM
    KVW = 2 * N_KV_HEADS * HEAD_DIM
    return {
        "x": nrm((BATCH, SEQ, D), 1.0),
        "p": nrm((DEPTH, BATCH, SEQ, PLE_DIM), 1.0),
        "mix_norm_g": gain((DEPTH, D)),
        "ffn_norm_g": gain((DEPTH, D)),
        "gmlp_w_in": nrm((N_A_LAYERS, D, GMLP_FFN), D ** -0.5),
        "gmlp_b_in": nrm((N_A_LAYERS, GMLP_FFN), 0.02),
        "gmlp_ln_g": gain((N_A_LAYERS, GMLP_HALF)),
        "gmlp_ln_b": nrm((N_A_LAYERS, GMLP_HALF), 0.02),
        "gmlp_w_s": nrm((N_A_LAYERS, GMLP_GROUPS, CHUNK, CHUNK), 0.5 * CHUNK ** -0.5),
        "gmlp_b_s": gain((N_A_LAYERS, GMLP_GROUPS, CHUNK)),
        "gmlp_w_out": nrm((N_A_LAYERS, GMLP_HALF, D), GMLP_HALF ** -0.5),
        "gmlp_b_out": nrm((N_A_LAYERS, D), 0.02),
        "kv_norm_g": gain((D,)),
        "w_kv": nrm((D, KVW), D ** -0.5),
        "b_kv": nrm((KVW,), 0.02),
        "attn_w_q": nrm((N_B_LAYERS, D, QW), D ** -0.5),
        "attn_b_q": nrm((N_B_LAYERS, QW), 0.02),
        "attn_sinks": nrm((N_B_LAYERS, N_HEADS), 0.5),
        "attn_w_o": nrm((N_B_LAYERS, QW, D), QW ** -0.5),
        "attn_b_o": nrm((N_B_LAYERS, D), 0.02),
        "ffn_w_gate": nrm((N_DENSE, D, D_FF_DENSE), D ** -0.5),
        "ffn_w_up": nrm((N_DENSE, D, D_FF_DENSE), D ** -0.5),
        "ffn_w_down": nrm((N_DENSE, D_FF_DENSE, D), D_FF_DENSE ** -0.5),
        "moe_w_router": nrm((N_MOE, D, N_EXPERTS), D ** -0.5),
        "moe_w_gate": nrm((N_MOE, N_EXPERTS, D, D_FF_EXPERT), D ** -0.5),
        "moe_w_up": nrm((N_MOE, N_EXPERTS, D, D_FF_EXPERT), D ** -0.5),
        "moe_w_down": nrm((N_MOE, N_EXPERTS, D_FF_EXPERT, D), D_FF_EXPERT ** -0.5),
        "ple_w_proj": nrm((DEPTH, PLE_DIM, D), PLE_DIM ** -0.5),
        "ple_norm_g": gain((DEPTH, D)),
        "ple_w_gate": nrm((DEPTH, D, D), D ** -0.5),
        "final_norm_g": gain((D,)),
    }


def reference(x, p, mix_norm_g, ffn_norm_g,
              gmlp_w_in, gmlp_b_in, gmlp_ln_g, gmlp_ln_b, gmlp_w_s, gmlp_b_s, gmlp_w_out, gmlp_b_out,
              kv_norm_g, w_kv, b_kv,
              attn_w_q, attn_b_q, attn_sinks, attn_w_o, attn_b_o,
              ffn_w_gate, ffn_w_up, ffn_w_down,
              moe_w_router, moe_w_gate, moe_w_up, moe_w_down,
              ple_w_proj, ple_norm_g, ple_w_gate,
              final_norm_g):
    k_band, v_band = None, None
    for i in range(DEPTH):
        h = rms_norm(x, mix_norm_g[i])
        if i < N_A_LAYERS:
            a = i
            x = x + gmlp_mixer(h, gmlp_w_in[a], gmlp_b_in[a], gmlp_ln_g[a], gmlp_ln_b[a],
                               gmlp_w_s[a], gmlp_b_s[a], gmlp_w_out[a], gmlp_b_out[a])
        else:
            b = i - N_A_LAYERS
            if b == 0:
                k_band, v_band = shared_kv(x, kv_norm_g, w_kv, b_kv)
            x = x + swa_sink_attention(h, attn_w_q[b], attn_b_q[b], attn_sinks[b],
                                       attn_w_o[b], attn_b_o[b], k_band, v_band)
        hn = rms_norm(x, ffn_norm_g[i])
        if i % 2 == 0:
            j = i // 2
            x = x + swiglu(hn, ffn_w_gate[j], ffn_w_up[j], ffn_w_down[j])
        else:
            j = i // 2
            x = x + moe_swiglu(hn, moe_w_router[j], moe_w_gate[j], moe_w_up[j], moe_w_down[j])
        x = x + per_layer_embedding(x, p[i], ple_w_proj[i], ple_norm_g[i], ple_w_gate[i])
    return rms_norm(x, final_norm_g)
```

```python
import functools
import math

import jax
import jax.numpy as jnp
from jax import lax
from jax.experimental import pallas as pl
from jax.experimental.pallas import tpu as pltpu

F32 = jnp.float32
BF16 = jnp.bfloat16

D_MODEL = 1024
BATCH = 4
SEQ = 4096
N_TOK = BATCH * SEQ

CHUNK = 128
GMLP_FFN = 6 * D_MODEL
GMLP_HALF = GMLP_FFN // 2
GMLP_GROUPS = 8
GMLP_GROUP_DIM = GMLP_HALF // GMLP_GROUPS

N_HEADS = 16
N_KV_HEADS = 4
HEAD_DIM = 64
KV_REP = N_HEADS // N_KV_HEADS
KV_WIDTH = N_KV_HEADS * HEAD_DIM
WINDOW = 128
ROPE_THETA = 10000.0

D_FF_DENSE = 2816
N_EXPERTS = 8
D_FF_EXPERT = 3584
PLE_DIM = 256

EPS = 1e-6
MASK_VALUE = -1e30

LANES = 128

GMLP_TILE = 256
FFN_TILE = 256
ATTN_TILE = 512
ROUTER_TILE = 512
DISPATCH_TILE = 2048
EXPERT_TILE = 512
EXPERT_FF_BLOCK = 512
COMBINE_TILE = 256

SORTED_ROWS = 2 * N_TOK + N_EXPERTS * EXPERT_TILE
N_ROW_TILES = SORTED_ROWS // EXPERT_TILE

VMEM_LIMIT = 56 * 1024 * 1024


def _resident(shape):
    zeros = (0,) * len(shape)
    return pl.BlockSpec(shape, lambda *_: zeros, pipeline_mode=pl.Buffered(1))


def _rms(x, g):
    return x * lax.rsqrt(jnp.mean(x * x, axis=-1, keepdims=True) + EPS) * g


def _gelu_tanh(x):
    c = math.sqrt(2.0 / math.pi)
    return 0.5 * x * (1.0 + jnp.tanh(c * (x + 0.044715 * (x * x * x))))


def _silu(x):
    return x * (1.0 / (1.0 + jnp.exp(-x)))


def _sigmoid(x):
    return 1.0 / (1.0 + jnp.exp(-x))


def _dot(a, b):
    return jnp.dot(a, b, preferred_element_type=F32)


def _gmlp_kernel(x_ref, g_ref, win_ref, bin_ref, lng_ref, lnb_ref, ws_ref, bs_ref,
                 wout_ref, bout_ref, o_ref):
    x = x_ref[...]
    h = _rms(x, g_ref[...]).astype(BF16)
    v = _gelu_tanh(_dot(h, win_ref[:, GMLP_HALF:]) + bin_ref[:, GMLP_HALF:])
    mu = jnp.mean(v, axis=-1, keepdims=True)
    vc = v - mu
    var = jnp.mean(vc * vc, axis=-1, keepdims=True)
    vn = (vc * lax.rsqrt(var + EPS) * lng_ref[...] + lnb_ref[...]).astype(BF16)

    row = lax.broadcasted_iota(jnp.int32, (CHUNK, CHUNK), 0)
    col = lax.broadcasted_iota(jnp.int32, (CHUNK, CHUNK), 1)
    causal = col <= row

    pair_w = 2 * GMLP_GROUP_DIM
    acc = x + bout_ref[...]
    for pair in range(GMLP_GROUPS // 2):
        c0 = pair * pair_w
        u = _gelu_tanh(_dot(h, win_ref[:, c0:c0 + pair_w]) + bin_ref[:, c0:c0 + pair_w])
        parts = []
        for gi in range(2 * pair, 2 * pair + 2):
            ws = jnp.where(causal, ws_ref[gi], 0.0).astype(BF16)
            bs = bs_ref[gi]
            vg = vn[:, gi * GMLP_GROUP_DIM:(gi + 1) * GMLP_GROUP_DIM]
            rows = [_dot(ws, vg[c * CHUNK:(c + 1) * CHUNK]) + bs for c in range(GMLP_TILE // CHUNK)]
            parts.append(jnp.concatenate(rows, axis=0))
        mixed = jnp.concatenate(parts, axis=1)
        gated = (u * mixed).astype(BF16)
        acc = acc + _dot(gated, wout_ref[c0:c0 + pair_w, :])
    o_ref[...] = acc


def _gmlp_mixer(x, g, w_in, b_in, ln_g, ln_b, w_s, b_s, w_out, b_out):
    t = GMLP_TILE
    tok = pl.BlockSpec((t, D_MODEL), lambda i: (i, 0))
    return pl.pallas_call(
        _gmlp_kernel,
        grid=(N_TOK // t,),
        in_specs=[tok, _resident((1, D_MODEL)), _resident((D_MODEL, GMLP_FFN)), _resident((1, GMLP_FFN)),
                  _resident((1, GMLP_HALF)), _resident((1, GMLP_HALF)),
                  _resident((GMLP_GROUPS, CHUNK, CHUNK)), _resident((GMLP_GROUPS, CHUNK, 1)),
                  _resident((GMLP_HALF, D_MODEL)), _resident((1, D_MODEL))],
        out_specs=tok,
        out_shape=jax.ShapeDtypeStruct((N_TOK, D_MODEL), F32),
        compiler_params=pltpu.CompilerParams(dimension_semantics=("arbitrary",), vmem_limit_bytes=VMEM_LIMIT),
        name="gmlp_mixer",
    )(x, g, w_in, b_in, ln_g, ln_b, w_s, b_s, w_out, b_out)


def _rope_slices(t, cos, sin_signed, first_half):
    outs = []
    for j in range(t.shape[1] // LANES):
        s = t[:, j * LANES:(j + 1) * LANES]
        partner = jnp.where(first_half, pltpu.roll(s, LANES - HEAD_DIM // 2, 1), pltpu.roll(s, HEAD_DIM // 2, 1))
        outs.append(s * cos + partner * sin_signed)
    return jnp.concatenate(outs, axis=1)


def _ffn_ple_qkv_kernel(x_ref, p_ref, cos_ref, sin_ref, fg_ref, wg_ref, wu_ref, wd_ref,
                        wproj_ref, pg_ref, wgate_ref, mg_ref, wq_ref, bq_ref, kg_ref, wkv_ref, bkv_ref,
                        x_out, q_out, k_out, v_out):
    x = x_ref[...]
    hn = _rms(x, fg_ref[...]).astype(BF16)
    a = (_silu(_dot(hn, wg_ref[...])) * _dot(hn, wu_ref[...])).astype(BF16)
    x = x + _dot(a, wd_ref[...])
    gate = _sigmoid(_dot(_rms(x, pg_ref[...]).astype(BF16), wgate_ref[...]))
    x = x + _dot(p_ref[...].astype(BF16), wproj_ref[...]) * gate
    x_out[...] = x

    cos = cos_ref[...]
    sin_signed = sin_ref[...]
    lane = lax.broadcasted_iota(jnp.int32, cos.shape, 1)
    first_half = (lane % HEAD_DIM) < (HEAD_DIM // 2)

    q = _dot(_rms(x, mg_ref[...]).astype(BF16), wq_ref[...]) + bq_ref[...]
    q = _rope_slices(q, cos, sin_signed, first_half) * (1.0 / math.sqrt(HEAD_DIM))
    q_out[...] = q.astype(BF16)

    kv = _dot(_rms(x, kg_ref[...]).astype(BF16), wkv_ref[...]) + bkv_ref[...]
    k = _rope_slices(kv[:, :KV_WIDTH], cos, sin_signed, first_half)
    k_out[...] = k.astype(BF16)
    v_out[...] = kv[:, KV_WIDTH:].astype(BF16)


def _ffn_ple_qkv(x, p0, cos_t, sin_t, fg, wg, wu, wd, wproj, pg, wgate, mg, wq, bq, kg, wkv, bkv):
    t = FFN_TILE
    per_seq = SEQ // t
    tok = lambda w: pl.BlockSpec((t, w), lambda i: (i, 0))
    rope = pl.BlockSpec((t, LANES), lambda i: (i % per_seq, 0))
    return pl.pallas_call(
        _ffn_ple_qkv_kernel,
        grid=(N_TOK // t,),
        in_specs=[tok(D_MODEL), tok(PLE_DIM), rope, rope,
                  _resident((1, D_MODEL)), _resident((D_MODEL, D_FF_DENSE)), _resident((D_MODEL, D_FF_DENSE)),
                  _resident((D_FF_DENSE, D_MODEL)),
                  _resident((PLE_DIM, D_MODEL)), _resident((1, D_MODEL)), _resident((D_MODEL, D_MODEL)),
                  _resident((1, D_MODEL)), _resident((D_MODEL, D_MODEL)), _resident((1, D_MODEL)),
                  _resident((1, D_MODEL)), _resident((D_MODEL, 2 * KV_WIDTH)), _resident((1, 2 * KV_WIDTH))],
        out_specs=[tok(D_MODEL), tok(D_MODEL), tok(KV_WIDTH), tok(KV_WIDTH)],
        out_shape=[jax.ShapeDtypeStruct((N_TOK, D_MODEL), F32), jax.ShapeDtypeStruct((N_TOK, D_MODEL), BF16),
                   jax.ShapeDtypeStruct((N_TOK, KV_WIDTH), BF16), jax.ShapeDtypeStruct((N_TOK, KV_WIDTH), BF16)],
        compiler_params=pltpu.CompilerParams(dimension_semantics=("arbitrary",), vmem_limit_bytes=VMEM_LIMIT),
        name="ffn_ple_qkv",
    )(x, p0, cos_t, sin_t, fg, wg, wu, wd, wproj, pg, wgate, mg, wq, bq, kg, wkv, bkv)


def _attn_kernel(sink_ref, x_ref, q_ref, kp_ref, kc_ref, vp_ref, vc_ref, wo_ref, bo_ref, o_ref):
    i = pl.program_id(0)
    seq_start = (i % (SEQ // ATTN_TILE)) == 0
    kk = jnp.concatenate([kp_ref[...], kc_ref[...]], axis=0)
    vv = jnp.concatenate([vp_ref[...], vc_ref[...]], axis=0)

    qi = lax.broadcasted_iota(jnp.int32, (KV_REP * WINDOW, 2 * WINDOW), 0) % WINDOW
    kj = lax.broadcasted_iota(jnp.int32, (KV_REP * WINDOW, 2 * WINDOW), 1)
    band = (kj > qi) & (kj <= qi + WINDOW)
    rep = lax.broadcasted_iota(jnp.int32, (KV_REP * WINDOW, 1), 0) // WINDOW

    blocks = []
    for sb in range(ATTN_TILE // WINDOW):
        q_sb = q_ref[sb * WINDOW:(sb + 1) * WINDOW, :]
        kb = kk[sb * WINDOW:sb * WINDOW + 2 * WINDOW]
        vb = vv[sb * WINDOW:sb * WINDOW + 2 * WINDOW]
        mask = band
        if sb == 0:
            mask = band & ((kj >= WINDOW) | jnp.logical_not(seq_start))
        heads = []
        for kh in range(N_KV_HEADS):
            k_h = kb[:, kh * HEAD_DIM:(kh + 1) * HEAD_DIM]
            v_h = vb[:, kh * HEAD_DIM:(kh + 1) * HEAD_DIM]
            q4 = jnp.concatenate(
                [q_sb[:, (kh * KV_REP + r) * HEAD_DIM:(kh * KV_REP + r + 1) * HEAD_DIM] for r in range(KV_REP)],
                axis=0)
            s = lax.dot_general(q4, k_h, (((1,), (1,)), ((), ())), preferred_element_type=F32)
            s = jnp.where(mask, s, MASK_VALUE)
            sink = jnp.zeros((KV_REP * WINDOW, 1), F32)
            for r in range(KV_REP):
                sink = jnp.where(rep == r, sink_ref[kh * KV_REP + r], sink)
            m = jnp.maximum(jnp.max(s, axis=-1, keepdims=True), sink)
            pr = jnp.exp(s - m)
            den = jnp.sum(pr, axis=-1, keepdims=True) + jnp.exp(sink - m)
            o = _dot(pr.astype(BF16), v_h) * (1.0 / den)
            heads.extend(o[r * WINDOW:(r + 1) * WINDOW] for r in range(KV_REP))
        blocks.append(jnp.concatenate(heads, axis=1))
    attn = jnp.concatenate(blocks, axis=0).astype(BF16)
    o_ref[...] = x_ref[...] + _dot(attn, wo_ref[...]) + bo_ref[...]


def _swa_attention(sinks, x, q, k, v, wo, bo):
    t = ATTN_TILE
    blocks_per_tile = t // WINDOW
    tok = lambda w: pl.BlockSpec((t, w), lambda i: (i, 0))
    prev = pl.BlockSpec((WINDOW, KV_WIDTH), lambda i: (jnp.maximum(i * blocks_per_tile - 1, 0), 0))
    return pl.pallas_call(
        _attn_kernel,
        grid=(N_TOK // t,),
        in_specs=[pl.BlockSpec(memory_space=pltpu.SMEM), tok(D_MODEL), tok(D_MODEL), prev, tok(KV_WIDTH),
                  prev, tok(KV_WIDTH), _resident((D_MODEL, D_MODEL)), _resident((1, D_MODEL))],
        out_specs=tok(D_MODEL),
        out_shape=jax.ShapeDtypeStruct((N_TOK, D_MODEL), F32),
        compiler_params=pltpu.CompilerParams(dimension_semantics=("arbitrary",), vmem_limit_bytes=VMEM_LIMIT),
        name="swa_attention",
    )(sinks, x, q, k, k, v, v, wo, bo)


def _router_kernel(x_ref, g_ref, wr_ref, hn_ref, idx_ref, wts_ref, rank_ref, cnt_ref):
    t = ROUTER_TILE

    @pl.when(pl.program_id(0) == 0)
    def _():
        cnt_ref[...] = jnp.zeros_like(cnt_ref)

    hn = _rms(x_ref[...], g_ref[...])
    hn_ref[...] = hn
    logits = jnp.dot(hn, wr_ref[...], preferred_element_type=F32, precision=lax.Precision.HIGHEST)
    lane = lax.broadcasted_iota(jnp.int32, (t, LANES), 1)
    logits = jnp.where(lane < N_EXPERTS, logits, -jnp.inf)

    m1 = jnp.max(logits, axis=-1, keepdims=True)
    i1 = jnp.min(jnp.where(logits == m1, lane, LANES), axis=-1, keepdims=True)
    rest = jnp.where(lane == i1, -jnp.inf, logits)
    m2 = jnp.max(rest, axis=-1, keepdims=True)
    i2 = jnp.min(jnp.where(rest == m2, lane, LANES), axis=-1, keepdims=True)
    e2 = jnp.exp(m2 - m1)
    w1 = 1.0 / (1.0 + e2)
    w2 = e2 / (1.0 + e2)

    chosen = (lane == i1) | (lane == i2)
    onehot = jnp.where(chosen, 1.0, 0.0).astype(BF16)
    r = lax.broadcasted_iota(jnp.int32, (t, t), 0)
    c = lax.broadcasted_iota(jnp.int32, (t, t), 1)
    before = jnp.where(c < r, 1.0, 0.0).astype(BF16)
    seen = _dot(before, onehot) + cnt_ref[...]
    rank1 = jnp.sum(jnp.where(lane == i1, seen, 0.0), axis=-1, keepdims=True)
    rank2 = jnp.sum(jnp.where(lane == i2, seen, 0.0), axis=-1, keepdims=True)
    cnt_ref[...] = cnt_ref[...] + jnp.sum(onehot.astype(F32), axis=0, keepdims=True)

    idx_ref[...] = jnp.concatenate([i1, i2], axis=1)
    wts_ref[...] = jnp.concatenate([w1, w2], axis=1)
    rank_ref[...] = jnp.concatenate([rank1, rank2], axis=1).astype(jnp.int32)


def _moe_router(x, g, w_router_padded):
    t = ROUTER_TILE
    tok = lambda w: pl.BlockSpec((t, w), lambda i: (i, 0))
    return pl.pallas_call(
        _router_kernel,
        grid=(N_TOK // t,),
        in_specs=[tok(D_MODEL), _resident((1, D_MODEL)), _resident((D_MODEL, LANES))],
        out_specs=[tok(D_MODEL), tok(2), tok(2), tok(2), pl.BlockSpec((1, LANES), lambda i: (0, 0))],
        out_shape=[jax.ShapeDtypeStruct((N_TOK, D_MODEL), F32), jax.ShapeDtypeStruct((N_TOK, 2), jnp.int32),
                   jax.ShapeDtypeStruct((N_TOK, 2), F32), jax.ShapeDtypeStruct((N_TOK, 2), jnp.int32),
                   jax.ShapeDtypeStruct((1, LANES), F32)],
        compiler_params=pltpu.CompilerParams(dimension_semantics=("arbitrary",), vmem_limit_bytes=VMEM_LIMIT),
        name="moe_router",
    )(x, g, w_router_padded)


def _dispatch_kernel(pos_ref, hn_ref, init_ref, xs_ref, sem):
    del init_ref
    base = pl.program_id(0) * DISPATCH_TILE

    def row_copy(t, k):
        return pltpu.make_async_copy(hn_ref.at[pl.ds(base + t, 1)], xs_ref.at[pl.ds(pos_ref[2 * t + k], 1)], sem)

    def issue(t, carry):
        row_copy(t, 0).start()
        row_copy(t, 1).start()
        return carry

    def drain(t, carry):
        row_copy(t, 0).wait()
        row_copy(t, 1).wait()
        return carry

    lax.fori_loop(0, DISPATCH_TILE, issue, 0)
    lax.fori_loop(0, DISPATCH_TILE, drain, 0)


def _moe_dispatch(pos_flat, hn, xs_init):
    t = DISPATCH_TILE
    return pl.pallas_call(
        _dispatch_kernel,
        grid=(N_TOK // t,),
        in_specs=[pl.BlockSpec((2 * t,), lambda i: (i,), memory_space=pltpu.SMEM),
                  pl.BlockSpec(memory_space=pl.ANY), pl.BlockSpec(memory_space=pl.ANY)],
        out_specs=pl.BlockSpec(memory_space=pl.ANY),
        out_shape=jax.ShapeDtypeStruct((SORTED_ROWS, D_MODEL), F32),
        scratch_shapes=[pltpu.SemaphoreType.DMA(())],
        input_output_aliases={2: 0},
        compiler_params=pltpu.CompilerParams(dimension_semantics=("arbitrary",), has_side_effects=True),
        name="moe_dispatch",
    )(pos_flat, hn, xs_init)


def _experts_kernel(te_ref, nv_ref, xs_ref, wg_ref, wu_ref, wd_ref, ys_ref, xb_ref):
    del te_ref
    i = pl.program_id(0)
    j = pl.program_id(1)

    @pl.when(j == 0)
    def _():
        xb_ref[...] = xs_ref[...].astype(BF16)
        ys_ref[...] = jnp.zeros_like(ys_ref)

    @pl.when(i < nv_ref[0])
    def _():
        xb = xb_ref[...]
        a = (_silu(_dot(xb, wg_ref[...])) * _dot(xb, wu_ref[...])).astype(BF16)
        ys_ref[...] += _dot(a, wd_ref[...])


def _moe_experts(tile_expert, n_valid, xs, wg, wu, wd):
    tm, fb = EXPERT_TILE, EXPERT_FF_BLOCK
    n_fb = D_FF_EXPERT // fb
    ff = lambda i, j, te, nv: jnp.where(i < nv[0], j, n_fb - 1)
    grid_spec = pltpu.PrefetchScalarGridSpec(
        num_scalar_prefetch=2,
        grid=(N_ROW_TILES, n_fb),
        in_specs=[pl.BlockSpec((tm, D_MODEL), lambda i, j, te, nv: (i, 0)),
                  pl.BlockSpec((None, D_MODEL, fb), lambda i, j, te, nv: (te[i], 0, ff(i, j, te, nv))),
                  pl.BlockSpec((None, D_MODEL, fb), lambda i, j, te, nv: (te[i], 0, ff(i, j, te, nv))),
                  pl.BlockSpec((None, fb, D_MODEL), lambda i, j, te, nv: (te[i], ff(i, j, te, nv), 0))],
        out_specs=pl.BlockSpec((tm, D_MODEL), lambda i, j, te, nv: (i, 0)),
        scratch_shapes=[pltpu.VMEM((tm, D_MODEL), BF16)],
    )
    return pl.pallas_call(
        _experts_kernel,
        grid_spec=grid_spec,
        out_shape=jax.ShapeDtypeStruct((SORTED_ROWS, D_MODEL), F32),
        compiler_params=pltpu.CompilerParams(dimension_semantics=("arbitrary", "arbitrary"),
                                             vmem_limit_bytes=VMEM_LIMIT),
        name="moe_experts",
    )(tile_expert, n_valid, xs, wg, wu, wd)


def _combine_kernel(pos_ref, x_ref, wts_ref, p_ref, ys_ref, wproj_ref, pg_ref, wgate_ref, fg_ref, o_ref,
                    rows_ref, sem):
    t = COMBINE_TILE

    def row_copy(r, k):
        return pltpu.make_async_copy(ys_ref.at[pl.ds(pos_ref[2 * r + k], 1)], rows_ref.at[k, pl.ds(r, 1)], sem)

    def issue(r, carry):
        row_copy(r, 0).start()
        row_copy(r, 1).start()
        return carry

    def drain(r, carry):
        row_copy(r, 0).wait()
        row_copy(r, 1).wait()
        return carry

    lax.fori_loop(0, t, issue, 0)
    lax.fori_loop(0, t, drain, 0)

    w = wts_ref[...]
    x = x_ref[...] + w[:, 0:1] * rows_ref[0] + w[:, 1:2] * rows_ref[1]
    gate = _sigmoid(_dot(_rms(x, pg_ref[...]).astype(BF16), wgate_ref[...]))
    x = x + _dot(p_ref[...].astype(BF16), wproj_ref[...]) * gate
    o_ref[...] = _rms(x, fg_ref[...])


def _moe_combine_out(pos_flat, x, wts, p1, ys, wproj, pg, wgate, fg):
    t = COMBINE_TILE
    tok = lambda w: pl.BlockSpec((t, w), lambda i: (i, 0))
    return pl.pallas_call(
        _combine_kernel,
        grid=(N_TOK // t,),
        in_specs=[pl.BlockSpec((2 * t,), lambda i: (i,), memory_space=pltpu.SMEM),
                  tok(D_MODEL), tok(2), tok(PLE_DIM), pl.BlockSpec(memory_space=pl.ANY),
                  _resident((PLE_DIM, D_MODEL)), _resident((1, D_MODEL)), _resident((D_MODEL, D_MODEL)),
                  _resident((1, D_MODEL))],
        out_specs=tok(D_MODEL),
        out_shape=jax.ShapeDtypeStruct((N_TOK, D_MODEL), F32),
        scratch_shapes=[pltpu.VMEM((2, t, D_MODEL), F32), pltpu.SemaphoreType.DMA(())],
        compiler_params=pltpu.CompilerParams(dimension_semantics=("arbitrary",), vmem_limit_bytes=VMEM_LIMIT),
        name="moe_combine_out",
    )(pos_flat, x, wts, p1, ys, wproj, pg, wgate, fg)


def _rope_tables():
    half = HEAD_DIM // 2
    freqs = ROPE_THETA ** (-jnp.arange(0, HEAD_DIM, 2, dtype=F32) / HEAD_DIM)
    ang = jnp.arange(SEQ, dtype=F32)[:, None] * freqs[None, :]
    cos, sin = jnp.cos(ang), jnp.sin(ang)
    reps = LANES // HEAD_DIM
    cos_t = jnp.tile(jnp.concatenate([cos, cos], axis=1), (1, reps))
    sin_t = jnp.tile(jnp.concatenate([-sin, sin], axis=1), (1, reps))
    del half
    return cos_t, sin_t


def _routing_plan(idx, rank, counts):
    tm = EXPERT_TILE
    tiles = (counts + tm - 1) // tm
    tile_end = jnp.cumsum(tiles)
    group_start = (tile_end - tiles) * tm
    pos = (group_start[idx] + rank).reshape(-1).astype(jnp.int32)
    n_valid = tile_end[-1]
    tile_ids = jnp.minimum(jnp.arange(N_ROW_TILES, dtype=jnp.int32), n_valid - 1)
    tile_expert = jnp.sum(tile_ids[:, None] >= tile_end[None, :], axis=1).astype(jnp.int32)
    return pos, tile_expert, n_valid.reshape(1).astype(jnp.int32)


def kernel(x, p, mix_norm_g, ffn_norm_g, gmlp_w_in, gmlp_b_in, gmlp_ln_g, gmlp_ln_b, gmlp_w_s, gmlp_b_s, gmlp_w_out, gmlp_b_out, kv_norm_g, w_kv, b_kv, attn_w_q, attn_b_q, attn_sinks, attn_w_o, attn_b_o, ffn_w_gate, ffn_w_up, ffn_w_down, moe_w_router, moe_w_gate, moe_w_up, moe_w_down, ple_w_proj, ple_norm_g, ple_w_gate, final_norm_g):
    row = lambda a: a.reshape(1, -1)
    bf = lambda a: a.astype(BF16)
    xf = x.reshape(N_TOK, D_MODEL)
    pf = p.reshape(2, N_TOK, PLE_DIM)

    x1 = _gmlp_mixer(xf, row(mix_norm_g[0]), bf(gmlp_w_in[0]), row(gmlp_b_in[0]), row(gmlp_ln_g[0]),
                     row(gmlp_ln_b[0]), gmlp_w_s[0], gmlp_b_s[0][:, :, None], bf(gmlp_w_out[0]),
                     row(gmlp_b_out[0]))

    cos_t, sin_t = _rope_tables()
    x3, q, k, v = _ffn_ple_qkv(x1, pf[0], cos_t, sin_t, row(ffn_norm_g[0]), bf(ffn_w_gate[0]), bf(ffn_w_up[0]),
                               bf(ffn_w_down[0]), bf(ple_w_proj[0]), row(ple_norm_g[0]), bf(ple_w_gate[0]),
                               row(mix_norm_g[1]), bf(attn_w_q[0]), row(attn_b_q[0]),
                               row(kv_norm_g), bf(w_kv), row(b_kv))

    x4 = _swa_attention(attn_sinks[0], x3, q, k, v, bf(attn_w_o[0]), row(attn_b_o[0]))

    w_router = jnp.pad(moe_w_router[0], ((0, 0), (0, LANES - N_EXPERTS)))
    hn, idx, wts, rank, counts = _moe_router(x4, row(ffn_norm_g[1]), w_router)
    pos, tile_expert, n_valid = _routing_plan(idx, rank, counts[0, :N_EXPERTS].astype(jnp.int32))

    xs = _moe_dispatch(pos, hn, jnp.zeros((SORTED_ROWS, D_MODEL), F32))
    ys = _moe_experts(tile_expert, n_valid, xs, bf(moe_w_gate[0]), bf(moe_w_up[0]), bf(moe_w_down[0]))
    out = _moe_combine_out(pos, x4, wts, pf[1], ys, bf(ple_w_proj[1]), row(ple_norm_g[1]), bf(ple_w_gate[1]),
                           row(final_norm_g))
    return out.reshape(BATCH, SEQ, D_MODEL)
```

```python
import functools
import math

import jax
import jax.numpy as jnp
from jax import lax
from jax.experimental import pallas as pl
from jax.experimental.pallas import tpu as pltpu

F32 = jnp.float32
BF16 = jnp.bfloat16

D_MODEL = 1024
BATCH = 4
SEQ = 4096
N_TOK = BATCH * SEQ

CHUNK = 128
GMLP_FFN = 6 * D_MODEL
GMLP_HALF = GMLP_FFN // 2
GMLP_GROUPS = 8
GMLP_GROUP_DIM = GMLP_HALF // GMLP_GROUPS

N_HEADS = 16
N_KV_HEADS = 4
HEAD_DIM = 64
KV_REP = N_HEADS // N_KV_HEADS
KV_WIDTH = N_KV_HEADS * HEAD_DIM
WINDOW = 128
ROPE_THETA = 10000.0

D_FF_DENSE = 2816
N_EXPERTS = 8
D_FF_EXPERT = 3584
PLE_DIM = 256

EPS = 1e-6
MASK_VALUE = -1e30

LANES = 128

GMLP_TILE = 256
FFN_TILE = 256
ATTN_TILE = 512
ROUTER_TILE = 512
DISPATCH_TILE = 1024
ROW_COPY_UNROLL = 8
EXPERT_TILE = 512
EXPERT_FF_BLOCK = 512
COMBINE_TILE = 256

SORTED_ROWS = 2 * N_TOK + N_EXPERTS * EXPERT_TILE
N_ROW_TILES = SORTED_ROWS // EXPERT_TILE

VMEM_LIMIT = 56 * 1024 * 1024


def _resident(shape):
    zeros = (0,) * len(shape)
    return pl.BlockSpec(shape, lambda *_: zeros, pipeline_mode=pl.Buffered(1))


def _rms(x, g):
    return x * lax.rsqrt(jnp.mean(x * x, axis=-1, keepdims=True) + EPS) * g


def _gelu_tanh(x):
    c = math.sqrt(2.0 / math.pi)
    return 0.5 * x * (1.0 + jnp.tanh(c * (x + 0.044715 * (x * x * x))))


def _silu(x):
    return x * (1.0 / (1.0 + jnp.exp(-x)))


def _sigmoid(x):
    return 1.0 / (1.0 + jnp.exp(-x))


def _dot(a, b):
    return jnp.dot(a, b, preferred_element_type=F32)


def _gmlp_kernel(x_ref, g_ref, win_ref, bin_ref, lng_ref, lnb_ref, ws_ref, bs_ref,
                 wout_ref, bout_ref, o_ref):
    x = x_ref[...]
    h = _rms(x, g_ref[...]).astype(BF16)
    v = _gelu_tanh(_dot(h, win_ref[:, GMLP_HALF:]) + bin_ref[:, GMLP_HALF:])
    mu = jnp.mean(v, axis=-1, keepdims=True)
    vc = v - mu
    var = jnp.mean(vc * vc, axis=-1, keepdims=True)
    vn = (vc * lax.rsqrt(var + EPS) * lng_ref[...] + lnb_ref[...]).astype(BF16)

    row = lax.broadcasted_iota(jnp.int32, (CHUNK, CHUNK), 0)
    col = lax.broadcasted_iota(jnp.int32, (CHUNK, CHUNK), 1)
    causal = col <= row

    pair_w = 2 * GMLP_GROUP_DIM
    acc = x + bout_ref[...]
    for pair in range(GMLP_GROUPS // 2):
        c0 = pair * pair_w
        u = _gelu_tanh(_dot(h, win_ref[:, c0:c0 + pair_w]) + bin_ref[:, c0:c0 + pair_w])
        parts = []
        for gi in range(2 * pair, 2 * pair + 2):
            ws = jnp.where(causal, ws_ref[gi], 0.0).astype(BF16)
            bs = bs_ref[gi]
            vg = vn[:, gi * GMLP_GROUP_DIM:(gi + 1) * GMLP_GROUP_DIM]
            rows = [_dot(ws, vg[c * CHUNK:(c + 1) * CHUNK]) + bs for c in range(GMLP_TILE // CHUNK)]
            parts.append(jnp.concatenate(rows, axis=0))
        mixed = jnp.concatenate(parts, axis=1)
        gated = (u * mixed).astype(BF16)
        acc = acc + _dot(gated, wout_ref[c0:c0 + pair_w, :])
    o_ref[...] = acc


def _gmlp_mixer(x, g, w_in, b_in, ln_g, ln_b, w_s, b_s, w_out, b_out):
    t = GMLP_TILE
    tok = pl.BlockSpec((t, D_MODEL), lambda i: (i, 0))
    return pl.pallas_call(
        _gmlp_kernel,
        grid=(N_TOK // t,),
        in_specs=[tok, _resident((1, D_MODEL)), _resident((D_MODEL, GMLP_FFN)), _resident((1, GMLP_FFN)),
                  _resident((1, GMLP_HALF)), _resident((1, GMLP_HALF)),
                  _resident((GMLP_GROUPS, CHUNK, CHUNK)), _resident((GMLP_GROUPS, CHUNK, 1)),
                  _resident((GMLP_HALF, D_MODEL)), _resident((1, D_MODEL))],
        out_specs=tok,
        out_shape=jax.ShapeDtypeStruct((N_TOK, D_MODEL), F32),
        compiler_params=pltpu.CompilerParams(dimension_semantics=("arbitrary",), vmem_limit_bytes=VMEM_LIMIT),
        name="gmlp_mixer",
    )(x, g, w_in, b_in, ln_g, ln_b, w_s, b_s, w_out, b_out)


def _rope_slices(t, cos, sin_signed, first_half):
    outs = []
    for j in range(t.shape[1] // LANES):
        s = t[:, j * LANES:(j + 1) * LANES]
        partner = jnp.where(first_half, pltpu.roll(s, LANES - HEAD_DIM // 2, 1), pltpu.roll(s, HEAD_DIM // 2, 1))
        outs.append(s * cos + partner * sin_signed)
    return jnp.concatenate(outs, axis=1)


def _ffn_ple_qkv_kernel(x_ref, p_ref, cos_ref, sin_ref, fg_ref, wg_ref, wu_ref, wd_ref,
                        wproj_ref, pg_ref, wgate_ref, mg_ref, wq_ref, bq_ref, kg_ref, wkv_ref, bkv_ref,
                        x_out, q_out, k_out, v_out):
    x = x_ref[...]
    hn = _rms(x, fg_ref[...]).astype(BF16)
    a = (_silu(_dot(hn, wg_ref[...])) * _dot(hn, wu_ref[...])).astype(BF16)
    x = x + _dot(a, wd_ref[...])
    gate = _sigmoid(_dot(_rms(x, pg_ref[...]).astype(BF16), wgate_ref[...]))
    x = x + _dot(p_ref[...].astype(BF16), wproj_ref[...]) * gate
    x_out[...] = x

    cos = cos_ref[...]
    sin_signed = sin_ref[...]
    lane = lax.broadcasted_iota(jnp.int32, cos.shape, 1)
    first_half = (lane % HEAD_DIM) < (HEAD_DIM // 2)

    q = _dot(_rms(x, mg_ref[...]).astype(BF16), wq_ref[...]) + bq_ref[...]
    q = _rope_slices(q, cos, sin_signed, first_half) * (1.0 / math.sqrt(HEAD_DIM))
    q_out[...] = q.astype(BF16)

    kv = _dot(_rms(x, kg_ref[...]).astype(BF16), wkv_ref[...]) + bkv_ref[...]
    k = _rope_slices(kv[:, :KV_WIDTH], cos, sin_signed, first_half)
    k_out[...] = k.astype(BF16)
    v_out[...] = kv[:, KV_WIDTH:].astype(BF16)


def _ffn_ple_qkv(x, p0, cos_t, sin_t, fg, wg, wu, wd, wproj, pg, wgate, mg, wq, bq, kg, wkv, bkv):
    t = FFN_TILE
    per_seq = SEQ // t
    tok = lambda w: pl.BlockSpec((t, w), lambda i: (i, 0))
    rope = pl.BlockSpec((t, LANES), lambda i: (i % per_seq, 0))
    return pl.pallas_call(
        _ffn_ple_qkv_kernel,
        grid=(N_TOK // t,),
        in_specs=[tok(D_MODEL), tok(PLE_DIM), rope, rope,
                  _resident((1, D_MODEL)), _resident((D_MODEL, D_FF_DENSE)), _resident((D_MODEL, D_FF_DENSE)),
                  _resident((D_FF_DENSE, D_MODEL)),
                  _resident((PLE_DIM, D_MODEL)), _resident((1, D_MODEL)), _resident((D_MODEL, D_MODEL)),
                  _resident((1, D_MODEL)), _resident((D_MODEL, D_MODEL)), _resident((1, D_MODEL)),
                  _resident((1, D_MODEL)), _resident((D_MODEL, 2 * KV_WIDTH)), _resident((1, 2 * KV_WIDTH))],
        out_specs=[tok(D_MODEL), tok(D_MODEL), tok(KV_WIDTH), tok(KV_WIDTH)],
        out_shape=[jax.ShapeDtypeStruct((N_TOK, D_MODEL), F32), jax.ShapeDtypeStruct((N_TOK, D_MODEL), BF16),
                   jax.ShapeDtypeStruct((N_TOK, KV_WIDTH), BF16), jax.ShapeDtypeStruct((N_TOK, KV_WIDTH), BF16)],
        compiler_params=pltpu.CompilerParams(dimension_semantics=("arbitrary",), vmem_limit_bytes=VMEM_LIMIT),
        name="ffn_ple_qkv",
    )(x, p0, cos_t, sin_t, fg, wg, wu, wd, wproj, pg, wgate, mg, wq, bq, kg, wkv, bkv)


def _attn_kernel(sink_ref, x_ref, q_ref, kp_ref, kc_ref, vp_ref, vc_ref, wo_ref, bo_ref, o_ref):
    i = pl.program_id(0)
    seq_start = (i % (SEQ // ATTN_TILE)) == 0
    kk = jnp.concatenate([kp_ref[...], kc_ref[...]], axis=0)
    vv = jnp.concatenate([vp_ref[...], vc_ref[...]], axis=0)

    qi = lax.broadcasted_iota(jnp.int32, (KV_REP * WINDOW, 2 * WINDOW), 0) % WINDOW
    kj = lax.broadcasted_iota(jnp.int32, (KV_REP * WINDOW, 2 * WINDOW), 1)
    band = (kj > qi) & (kj <= qi + WINDOW)
    rep = lax.broadcasted_iota(jnp.int32, (KV_REP * WINDOW, 1), 0) // WINDOW

    blocks = []
    for sb in range(ATTN_TILE // WINDOW):
        q_sb = q_ref[sb * WINDOW:(sb + 1) * WINDOW, :]
        kb = kk[sb * WINDOW:sb * WINDOW + 2 * WINDOW]
        vb = vv[sb * WINDOW:sb * WINDOW + 2 * WINDOW]
        mask = band
        if sb == 0:
            mask = band & ((kj >= WINDOW) | jnp.logical_not(seq_start))
        heads = []
        for kh in range(N_KV_HEADS):
            k_h = kb[:, kh * HEAD_DIM:(kh + 1) * HEAD_DIM]
            v_h = vb[:, kh * HEAD_DIM:(kh + 1) * HEAD_DIM]
            q4 = jnp.concatenate(
                [q_sb[:, (kh * KV_REP + r) * HEAD_DIM:(kh * KV_REP + r + 1) * HEAD_DIM] for r in range(KV_REP)],
                axis=0)
            s = lax.dot_general(q4, k_h, (((1,), (1,)), ((), ())), preferred_element_type=F32)
            s = jnp.where(mask, s, MASK_VALUE)
            sink = jnp.zeros((KV_REP * WINDOW, 1), F32)
            for r in range(KV_REP):
                sink = jnp.where(rep == r, sink_ref[kh * KV_REP + r], sink)
            m = jnp.maximum(jnp.max(s, axis=-1, keepdims=True), sink)
            pr = jnp.exp(s - m)
            den = jnp.sum(pr, axis=-1, keepdims=True) + jnp.exp(sink - m)
            o = _dot(pr.astype(BF16), v_h) * (1.0 / den)
            heads.extend(o[r * WINDOW:(r + 1) * WINDOW] for r in range(KV_REP))
        blocks.append(jnp.concatenate(heads, axis=1))
    attn = jnp.concatenate(blocks, axis=0).astype(BF16)
    o_ref[...] = x_ref[...] + _dot(attn, wo_ref[...]) + bo_ref[...]


def _swa_attention(sinks, x, q, k, v, wo, bo):
    t = ATTN_TILE
    blocks_per_tile = t // WINDOW
    tok = lambda w: pl.BlockSpec((t, w), lambda i: (i, 0))
    prev = pl.BlockSpec((WINDOW, KV_WIDTH), lambda i: (jnp.maximum(i * blocks_per_tile - 1, 0), 0))
    return pl.pallas_call(
        _attn_kernel,
        grid=(N_TOK // t,),
        in_specs=[pl.BlockSpec(memory_space=pltpu.SMEM), tok(D_MODEL), tok(D_MODEL), prev, tok(KV_WIDTH),
                  prev, tok(KV_WIDTH), _resident((D_MODEL, D_MODEL)), _resident((1, D_MODEL))],
        out_specs=tok(D_MODEL),
        out_shape=jax.ShapeDtypeStruct((N_TOK, D_MODEL), F32),
        compiler_params=pltpu.CompilerParams(dimension_semantics=("arbitrary",), vmem_limit_bytes=VMEM_LIMIT),
        name="swa_attention",
    )(sinks, x, q, k, k, v, v, wo, bo)


def _router_kernel(x_ref, g_ref, wr_ref, hn_ref, idx_ref, wts_ref, rank_ref, cnt_ref):
    t = ROUTER_TILE

    @pl.when(pl.program_id(0) == 0)
    def _():
        cnt_ref[...] = jnp.zeros_like(cnt_ref)

    hn = _rms(x_ref[...], g_ref[...])
    hn_ref[...] = hn
    logits = jnp.dot(hn, wr_ref[...], preferred_element_type=F32, precision=lax.Precision.HIGHEST)
    lane = lax.broadcasted_iota(jnp.int32, (t, LANES), 1)
    logits = jnp.where(lane < N_EXPERTS, logits, -jnp.inf)

    m1 = jnp.max(logits, axis=-1, keepdims=True)
    i1 = jnp.min(jnp.where(logits == m1, lane, LANES), axis=-1, keepdims=True)
    rest = jnp.where(lane == i1, -jnp.inf, logits)
    m2 = jnp.max(rest, axis=-1, keepdims=True)
    i2 = jnp.min(jnp.where(rest == m2, lane, LANES), axis=-1, keepdims=True)
    e2 = jnp.exp(m2 - m1)
    w1 = 1.0 / (1.0 + e2)
    w2 = e2 / (1.0 + e2)

    chosen = (lane == i1) | (lane == i2)
    onehot = jnp.where(chosen, 1.0, 0.0).astype(BF16)
    r = lax.broadcasted_iota(jnp.int32, (t, t), 0)
    c = lax.broadcasted_iota(jnp.int32, (t, t), 1)
    before = jnp.where(c < r, 1.0, 0.0).astype(BF16)
    seen = _dot(before, onehot) + cnt_ref[...]
    rank1 = jnp.sum(jnp.where(lane == i1, seen, 0.0), axis=-1, keepdims=True)
    rank2 = jnp.sum(jnp.where(lane == i2, seen, 0.0), axis=-1, keepdims=True)
    cnt_ref[...] = cnt_ref[...] + jnp.sum(onehot.astype(F32), axis=0, keepdims=True)

    idx_ref[...] = jnp.concatenate([i1, i2], axis=1)
    wts_ref[...] = jnp.concatenate([w1, w2], axis=1)
    rank_ref[...] = jnp.concatenate([rank1, rank2], axis=1).astype(jnp.int32)


def _moe_router(x, g, w_router_padded):
    t = ROUTER_TILE
    tok = lambda w: pl.BlockSpec((t, w), lambda i: (i, 0))
    return pl.pallas_call(
        _router_kernel,
        grid=(N_TOK // t,),
        in_specs=[tok(D_MODEL), _resident((1, D_MODEL)), _resident((D_MODEL, LANES))],
        out_specs=[tok(D_MODEL), tok(2), tok(2), tok(2), pl.BlockSpec((1, LANES), lambda i: (0, 0))],
        out_shape=[jax.ShapeDtypeStruct((N_TOK, D_MODEL), F32), jax.ShapeDtypeStruct((N_TOK, 2), jnp.int32),
                   jax.ShapeDtypeStruct((N_TOK, 2), F32), jax.ShapeDtypeStruct((N_TOK, 2), jnp.int32),
                   jax.ShapeDtypeStruct((1, LANES), F32)],
        compiler_params=pltpu.CompilerParams(dimension_semantics=("arbitrary",), vmem_limit_bytes=VMEM_LIMIT),
        name="moe_router",
    )(x, g, w_router_padded)


def _dispatch_kernel(pos_ref, hn_ref, init_ref, xs_ref, sem):
    del init_ref

    def row_copy(t, k):
        return pltpu.make_async_copy(hn_ref.at[pl.ds(t, 1)], xs_ref.at[pl.ds(pos_ref[2 * t + k], 1)], sem)

    def issue(t, carry):
        row_copy(t, 0).start()
        row_copy(t, 1).start()
        return carry

    def drain(t, carry):
        row_copy(t, 0).wait()
        row_copy(t, 1).wait()
        return carry

    lax.fori_loop(0, DISPATCH_TILE, issue, 0, unroll=ROW_COPY_UNROLL)
    lax.fori_loop(0, DISPATCH_TILE, drain, 0, unroll=ROW_COPY_UNROLL)


def _moe_dispatch(pos_flat, hn, xs_init):
    t = DISPATCH_TILE
    return pl.pallas_call(
        _dispatch_kernel,
        grid=(N_TOK // t,),
        in_specs=[pl.BlockSpec((2 * t,), lambda i: (i,), memory_space=pltpu.SMEM),
                  pl.BlockSpec((t, D_MODEL), lambda i: (i, 0)), pl.BlockSpec(memory_space=pl.ANY)],
        out_specs=pl.BlockSpec(memory_space=pl.ANY),
        out_shape=jax.ShapeDtypeStruct((SORTED_ROWS, D_MODEL), F32),
        scratch_shapes=[pltpu.SemaphoreType.DMA(())],
        input_output_aliases={2: 0},
        compiler_params=pltpu.CompilerParams(dimension_semantics=("arbitrary",), has_side_effects=True,
                                             vmem_limit_bytes=VMEM_LIMIT),
        name="moe_dispatch",
    )(pos_flat, hn, xs_init)


def _experts_kernel(te_ref, nv_ref, xs_ref, wg_ref, wu_ref, wd_ref, ys_ref, xb_ref):
    del te_ref
    i = pl.program_id(0)
    j = pl.program_id(1)

    @pl.when(j == 0)
    def _():
        xb_ref[...] = xs_ref[...].astype(BF16)
        ys_ref[...] = jnp.zeros_like(ys_ref)

    @pl.when(i < nv_ref[0])
    def _():
        xb = xb_ref[...]
        a = (_silu(_dot(xb, wg_ref[...])) * _dot(xb, wu_ref[...])).astype(BF16)
        ys_ref[...] += _dot(a, wd_ref[...])


def _moe_experts(tile_expert, n_valid, xs, wg, wu, wd):
    tm, fb = EXPERT_TILE, EXPERT_FF_BLOCK
    n_fb = D_FF_EXPERT // fb
    ff = lambda i, j, te, nv: jnp.where(i < nv[0], j, n_fb - 1)
    grid_spec = pltpu.PrefetchScalarGridSpec(
        num_scalar_prefetch=2,
        grid=(N_ROW_TILES, n_fb),
        in_specs=[pl.BlockSpec((tm, D_MODEL), lambda i, j, te, nv: (i, 0)),
                  pl.BlockSpec((None, D_MODEL, fb), lambda i, j, te, nv: (te[i], 0, ff(i, j, te, nv))),
                  pl.BlockSpec((None, D_MODEL, fb), lambda i, j, te, nv: (te[i], 0, ff(i, j, te, nv))),
                  pl.BlockSpec((None, fb, D_MODEL), lambda i, j, te, nv: (te[i], ff(i, j, te, nv), 0))],
        out_specs=pl.BlockSpec((tm, D_MODEL), lambda i, j, te, nv: (i, 0)),
        scratch_shapes=[pltpu.VMEM((tm, D_MODEL), BF16)],
    )
    return pl.pallas_call(
        _experts_kernel,
        grid_spec=grid_spec,
        out_shape=jax.ShapeDtypeStruct((SORTED_ROWS, D_MODEL), F32),
        compiler_params=pltpu.CompilerParams(dimension_semantics=("arbitrary", "arbitrary"),
                                             vmem_limit_bytes=VMEM_LIMIT),
        name="moe_experts",
    )(tile_expert, n_valid, xs, wg, wu, wd)


def _combine_kernel(pos_ref, x_ref, wts_ref, p_ref, ys_ref, wproj_ref, pg_ref, wgate_ref, fg_ref, o_ref,
                    rows_ref, sem):
    t = COMBINE_TILE

    def row_copy(r, k):
        return pltpu.make_async_copy(ys_ref.at[pl.ds(pos_ref[2 * r + k], 1)], rows_ref.at[k, pl.ds(r, 1)], sem)

    def issue(r, carry):
        row_copy(r, 0).start()
        row_copy(r, 1).start()
        return carry

    def drain(r, carry):
        row_copy(r, 0).wait()
        row_copy(r, 1).wait()
        return carry

    lax.fori_loop(0, t, issue, 0, unroll=ROW_COPY_UNROLL)
    lax.fori_loop(0, t, drain, 0, unroll=ROW_COPY_UNROLL)

    w = wts_ref[...]
    x = x_ref[...] + w[:, 0:1] * rows_ref[0] + w[:, 1:2] * rows_ref[1]
    gate = _sigmoid(_dot(_rms(x, pg_ref[...]).astype(BF16), wgate_ref[...]))
    x = x + _dot(p_ref[...].astype(BF16), wproj_ref[...]) * gate
    o_ref[...] = _rms(x, fg_ref[...])


def _moe_combine_out(pos_flat, x, wts, p1, ys, wproj, pg, wgate, fg):
    t = COMBINE_TILE
    tok = lambda w: pl.BlockSpec((t, w), lambda i: (i, 0))
    return pl.pallas_call(
        _combine_kernel,
        grid=(N_TOK // t,),
        in_specs=[pl.BlockSpec((2 * t,), lambda i: (i,), memory_space=pltpu.SMEM),
                  tok(D_MODEL), tok(2), tok(PLE_DIM), pl.BlockSpec(memory_space=pl.ANY),
                  _resident((PLE_DIM, D_MODEL)), _resident((1, D_MODEL)), _resident((D_MODEL, D_MODEL)),
                  _resident((1, D_MODEL))],
        out_specs=tok(D_MODEL),
        out_shape=jax.ShapeDtypeStruct((N_TOK, D_MODEL), F32),
        scratch_shapes=[pltpu.VMEM((2, t, D_MODEL), F32), pltpu.SemaphoreType.DMA(())],
        compiler_params=pltpu.CompilerParams(dimension_semantics=("arbitrary",), vmem_limit_bytes=VMEM_LIMIT),
        name="moe_combine_out",
    )(pos_flat, x, wts, p1, ys, wproj, pg, wgate, fg)


def _rope_tables():
    half = HEAD_DIM // 2
    freqs = ROPE_THETA ** (-jnp.arange(0, HEAD_DIM, 2, dtype=F32) / HEAD_DIM)
    ang = jnp.arange(SEQ, dtype=F32)[:, None] * freqs[None, :]
    cos, sin = jnp.cos(ang), jnp.sin(ang)
    reps = LANES // HEAD_DIM
    cos_t = jnp.tile(jnp.concatenate([cos, cos], axis=1), (1, reps))
    sin_t = jnp.tile(jnp.concatenate([-sin, sin], axis=1), (1, reps))
    del half
    return cos_t, sin_t


def _routing_plan(idx, rank, counts):
    tm = EXPERT_TILE
    tiles = (counts + tm - 1) // tm
    tile_end = jnp.cumsum(tiles)
    group_start = (tile_end - tiles) * tm
    pos = (group_start[idx] + rank).reshape(-1).astype(jnp.int32)
    n_valid = tile_end[-1]
    tile_ids = jnp.minimum(jnp.arange(N_ROW_TILES, dtype=jnp.int32), n_valid - 1)
    tile_expert = jnp.sum(tile_ids[:, None] >= tile_end[None, :], axis=1).astype(jnp.int32)
    return pos, tile_expert, n_valid.reshape(1).astype(jnp.int32)


def kernel(x, p, mix_norm_g, ffn_norm_g, gmlp_w_in, gmlp_b_in, gmlp_ln_g, gmlp_ln_b, gmlp_w_s, gmlp_b_s, gmlp_w_out, gmlp_b_out, kv_norm_g, w_kv, b_kv, attn_w_q, attn_b_q, attn_sinks, attn_w_o, attn_b_o, ffn_w_gate, ffn_w_up, ffn_w_down, moe_w_router, moe_w_gate, moe_w_up, moe_w_down, ple_w_proj, ple_norm_g, ple_w_gate, final_norm_g):
    row = lambda a: a.reshape(1, -1)
    bf = lambda a: a.astype(BF16)
    xf = x.reshape(N_TOK, D_MODEL)
    pf = p.reshape(2, N_TOK, PLE_DIM)

    x1 = _gmlp_mixer(xf, row(mix_norm_g[0]), bf(gmlp_w_in[0]), row(gmlp_b_in[0]), row(gmlp_ln_g[0]),
                     row(gmlp_ln_b[0]), gmlp_w_s[0], gmlp_b_s[0][:, :, None], bf(gmlp_w_out[0]),
                     row(gmlp_b_out[0]))

    cos_t, sin_t = _rope_tables()
    x3, q, k, v = _ffn_ple_qkv(x1, pf[0], cos_t, sin_t, row(ffn_norm_g[0]), bf(ffn_w_gate[0]), bf(ffn_w_up[0]),
                               bf(ffn_w_down[0]), bf(ple_w_proj[0]), row(ple_norm_g[0]), bf(ple_w_gate[0]),
                               row(mix_norm_g[1]), bf(attn_w_q[0]), row(attn_b_q[0]),
                               row(kv_norm_g), bf(w_kv), row(b_kv))

    x4 = _swa_attention(attn_sinks[0], x3, q, k, v, bf(attn_w_o[0]), row(attn_b_o[0]))

    w_router = jnp.pad(moe_w_router[0], ((0, 0), (0, LANES - N_EXPERTS)))
    hn, idx, wts, rank, counts = _moe_router(x4, row(ffn_norm_g[1]), w_router)
    pos, tile_expert, n_valid = _routing_plan(idx, rank, counts[0, :N_EXPERTS].astype(jnp.int32))

    xs = _moe_dispatch(pos, hn, jnp.zeros((SORTED_ROWS, D_MODEL), F32))
    ys = _moe_experts(tile_expert, n_valid, xs, bf(moe_w_gate[0]), bf(moe_w_up[0]), bf(moe_w_down[0]))
    out = _moe_combine_out(pos, x4, wts, pf[1], ys, bf(ple_w_proj[1]), row(ple_norm_g[1]), bf(ple_w_gate[1]),
                           row(final_norm_g))
    return out.reshape(BATCH, SEQ, D_MODEL)
```

```python
import functools
import math

import jax
import jax.numpy as jnp
from jax import lax
from jax.experimental import pallas as pl
from jax.experimental.pallas import tpu as pltpu

F32 = jnp.float32
BF16 = jnp.bfloat16

D_MODEL = 1024
BATCH = 4
SEQ = 4096
N_TOK = BATCH * SEQ

CHUNK = 128
GMLP_FFN = 6 * D_MODEL
GMLP_HALF = GMLP_FFN // 2
GMLP_GROUPS = 8
GMLP_GROUP_DIM = GMLP_HALF // GMLP_GROUPS

N_HEADS = 16
N_KV_HEADS = 4
HEAD_DIM = 64
KV_REP = N_HEADS // N_KV_HEADS
KV_WIDTH = N_KV_HEADS * HEAD_DIM
WINDOW = 128
ROPE_THETA = 10000.0

D_FF_DENSE = 2816
N_EXPERTS = 8
D_FF_EXPERT = 3584
PLE_DIM = 256

EPS = 1e-6
MASK_VALUE = -1e30

LANES = 128

GMLP_TILE = 256
FFN_TILE = 256
ATTN_TILE = 512
ROUTER_TILE = 512
DISPATCH_TILE = 1024
ROW_COPY_UNROLL = 8
EXPERT_TILE = 512
EXPERT_FF_BLOCK = 1792
COMBINE_TILE = 256

SORTED_ROWS = 2 * N_TOK + N_EXPERTS * EXPERT_TILE
N_ROW_TILES = SORTED_ROWS // EXPERT_TILE

VMEM_LIMIT = 56 * 1024 * 1024


def _resident(shape):
    zeros = (0,) * len(shape)
    return pl.BlockSpec(shape, lambda *_: zeros, pipeline_mode=pl.Buffered(1))


def _rms(x, g):
    return x * lax.rsqrt(jnp.mean(x * x, axis=-1, keepdims=True) + EPS) * g


def _gelu_tanh(x):
    c = math.sqrt(2.0 / math.pi)
    return 0.5 * x * (1.0 + jnp.tanh(c * (x + 0.044715 * (x * x * x))))


def _silu(x):
    return x * (1.0 / (1.0 + jnp.exp(-x)))


def _sigmoid(x):
    return 1.0 / (1.0 + jnp.exp(-x))


def _dot(a, b):
    return jnp.dot(a, b, preferred_element_type=F32)


def _gmlp_kernel(x_ref, g_ref, win_ref, bin_ref, lng_ref, lnb_ref, ws_ref, bs_ref,
                 wout_ref, bout_ref, o_ref):
    x = x_ref[...]
    h = _rms(x, g_ref[...]).astype(BF16)
    v = _gelu_tanh(_dot(h, win_ref[:, GMLP_HALF:]) + bin_ref[:, GMLP_HALF:])
    mu = jnp.mean(v, axis=-1, keepdims=True)
    vc = v - mu
    var = jnp.mean(vc * vc, axis=-1, keepdims=True)
    vn = (vc * lax.rsqrt(var + EPS) * lng_ref[...] + lnb_ref[...]).astype(BF16)

    row = lax.broadcasted_iota(jnp.int32, (CHUNK, CHUNK), 0)
    col = lax.broadcasted_iota(jnp.int32, (CHUNK, CHUNK), 1)
    causal = col <= row

    pair_w = 2 * GMLP_GROUP_DIM
    acc = x + bout_ref[...]
    for pair in range(GMLP_GROUPS // 2):
        c0 = pair * pair_w
        u = _gelu_tanh(_dot(h, win_ref[:, c0:c0 + pair_w]) + bin_ref[:, c0:c0 + pair_w])
        parts = []
        for gi in range(2 * pair, 2 * pair + 2):
            ws = jnp.where(causal, ws_ref[gi], 0.0).astype(BF16)
            bs = bs_ref[gi]
            vg = vn[:, gi * GMLP_GROUP_DIM:(gi + 1) * GMLP_GROUP_DIM]
            rows = [_dot(ws, vg[c * CHUNK:(c + 1) * CHUNK]) + bs for c in range(GMLP_TILE // CHUNK)]
            parts.append(jnp.concatenate(rows, axis=0))
        mixed = jnp.concatenate(parts, axis=1)
        gated = (u * mixed).astype(BF16)
        acc = acc + _dot(gated, wout_ref[c0:c0 + pair_w, :])
    o_ref[...] = acc


def _gmlp_mixer(x, g, w_in, b_in, ln_g, ln_b, w_s, b_s, w_out, b_out):
    t = GMLP_TILE
    tok = pl.BlockSpec((t, D_MODEL), lambda i: (i, 0))
    return pl.pallas_call(
        _gmlp_kernel,
        grid=(N_TOK // t,),
        in_specs=[tok, _resident((1, D_MODEL)), _resident((D_MODEL, GMLP_FFN)), _resident((1, GMLP_FFN)),
                  _resident((1, GMLP_HALF)), _resident((1, GMLP_HALF)),
                  _resident((GMLP_GROUPS, CHUNK, CHUNK)), _resident((GMLP_GROUPS, CHUNK, 1)),
                  _resident((GMLP_HALF, D_MODEL)), _resident((1, D_MODEL))],
        out_specs=tok,
        out_shape=jax.ShapeDtypeStruct((N_TOK, D_MODEL), F32),
        compiler_params=pltpu.CompilerParams(dimension_semantics=("arbitrary",), vmem_limit_bytes=VMEM_LIMIT),
        name="gmlp_mixer",
    )(x, g, w_in, b_in, ln_g, ln_b, w_s, b_s, w_out, b_out)


def _rope_slices(t, cos, sin_signed, first_half):
    outs = []
    for j in range(t.shape[1] // LANES):
        s = t[:, j * LANES:(j + 1) * LANES]
        partner = jnp.where(first_half, pltpu.roll(s, LANES - HEAD_DIM // 2, 1), pltpu.roll(s, HEAD_DIM // 2, 1))
        outs.append(s * cos + partner * sin_signed)
    return jnp.concatenate(outs, axis=1)


def _dot_nt(a, b):
    return lax.dot_general(a, b, (((1,), (1,)), ((), ())), preferred_element_type=F32)


def _ffn_ple_qkv_kernel(x_ref, p_ref, cos_ref, sin_ref, cost_ref, sint_ref, fg_ref, wg_ref, wu_ref, wd_ref,
                        wproj_ref, pg_ref, wgate_ref, mg_ref, wqt_ref, bq_ref, kg_ref, wk_ref, bk_ref,
                        wvt_ref, bv_ref, x_out, qt_out, k_out, vt_out):
    x = x_ref[...]
    hn = _rms(x, fg_ref[...]).astype(BF16)
    a = (_silu(_dot(hn, wg_ref[...])) * _dot(hn, wu_ref[...])).astype(BF16)
    x = x + _dot(a, wd_ref[...])
    gate = _sigmoid(_dot(_rms(x, pg_ref[...]).astype(BF16), wgate_ref[...]))
    x = x + _dot(p_ref[...].astype(BF16), wproj_ref[...]) * gate
    x_out[...] = x

    qt = _dot_nt(wqt_ref[...], _rms(x, mg_ref[...]).astype(BF16)) + bq_ref[...]
    cos_t = cost_ref[...]
    sin_t = sint_ref[...]
    half = HEAD_DIM // 2
    scale = 1.0 / math.sqrt(HEAD_DIM)
    for h in range(N_HEADS):
        t1 = qt[h * HEAD_DIM:h * HEAD_DIM + half]
        t2 = qt[h * HEAD_DIM + half:(h + 1) * HEAD_DIM]
        qt_out[h * HEAD_DIM:h * HEAD_DIM + half, :] = ((t1 * cos_t - t2 * sin_t) * scale).astype(BF16)
        qt_out[h * HEAD_DIM + half:(h + 1) * HEAD_DIM, :] = ((t2 * cos_t + t1 * sin_t) * scale).astype(BF16)

    hkv = _rms(x, kg_ref[...]).astype(BF16)
    lane = lax.broadcasted_iota(jnp.int32, cos_ref.shape, 1)
    first_half = (lane % HEAD_DIM) < half
    k = _rope_slices(_dot(hkv, wk_ref[...]) + bk_ref[...], cos_ref[...], sin_ref[...], first_half)
    k_out[...] = k.astype(BF16)
    vt_out[...] = (_dot_nt(wvt_ref[...], hkv) + bv_ref[...]).astype(BF16)


def _ffn_ple_qkv(x, p0, rope_tabs, fg, wg, wu, wd, wproj, pg, wgate, mg, wqt, bq_col, kg, wk, bk, wvt, bv_col):
    t = FFN_TILE
    per_seq = SEQ // t
    half = HEAD_DIM // 2
    tok = lambda w: pl.BlockSpec((t, w), lambda i: (i, 0))
    feat = lambda w: pl.BlockSpec((w, t), lambda i: (0, i))
    rope = pl.BlockSpec((t, LANES), lambda i: (i % per_seq, 0))
    rope_t = pl.BlockSpec((half, t), lambda i: (0, i % per_seq))
    cos_t, sin_t, cos_tt, sin_tt = rope_tabs
    return pl.pallas_call(
        _ffn_ple_qkv_kernel,
        grid=(N_TOK // t,),
        in_specs=[tok(D_MODEL), tok(PLE_DIM), rope, rope, rope_t, rope_t,
                  _resident((1, D_MODEL)), _resident((D_MODEL, D_FF_DENSE)), _resident((D_MODEL, D_FF_DENSE)),
                  _resident((D_FF_DENSE, D_MODEL)),
                  _resident((PLE_DIM, D_MODEL)), _resident((1, D_MODEL)), _resident((D_MODEL, D_MODEL)),
                  _resident((1, D_MODEL)), _resident((D_MODEL, D_MODEL)), _resident((D_MODEL, 1)),
                  _resident((1, D_MODEL)), _resident((D_MODEL, KV_WIDTH)), _resident((1, KV_WIDTH)),
                  _resident((KV_WIDTH, D_MODEL)), _resident((KV_WIDTH, 1))],
        out_specs=[tok(D_MODEL), feat(D_MODEL), tok(KV_WIDTH), feat(KV_WIDTH)],
        out_shape=[jax.ShapeDtypeStruct((N_TOK, D_MODEL), F32), jax.ShapeDtypeStruct((D_MODEL, N_TOK), BF16),
                   jax.ShapeDtypeStruct((N_TOK, KV_WIDTH), BF16), jax.ShapeDtypeStruct((KV_WIDTH, N_TOK), BF16)],
        compiler_params=pltpu.CompilerParams(dimension_semantics=("arbitrary",), vmem_limit_bytes=VMEM_LIMIT),
        name="ffn_ple_qkv",
    )(x, p0, cos_t, sin_t, cos_tt, sin_tt, fg, wg, wu, wd, wproj, pg, wgate, mg, wqt, bq_col, kg, wk, bk,
      wvt, bv_col)


def _attn_kernel(sink_ref, x_ref, qt_ref, kp_ref, kc_ref, vtp_ref, vtc_ref, wo_ref, bo_ref, o_ref, attnt_ref):
    i = pl.program_id(0)
    seq_start = (i % (SEQ // ATTN_TILE)) == 0
    kk = jnp.concatenate([kp_ref[...], kc_ref[...]], axis=0)
    vvt = jnp.concatenate([vtp_ref[...], vtc_ref[...]], axis=1)

    width = KV_REP * WINDOW
    kj = lax.broadcasted_iota(jnp.int32, (2 * WINDOW, width), 0)
    qi = lax.broadcasted_iota(jnp.int32, (2 * WINDOW, width), 1) % WINDOW
    band = (kj > qi) & (kj <= qi + WINDOW)
    first_band = band & ((kj >= WINDOW) | jnp.logical_not(seq_start))
    rep = lax.broadcasted_iota(jnp.int32, (1, width), 1) // WINDOW

    for kh in range(N_KV_HEADS):
        feats = slice(kh * HEAD_DIM, (kh + 1) * HEAD_DIM)
        sink = jnp.zeros((1, width), F32)
        for r in range(KV_REP):
            sink = jnp.where(rep == r, sink_ref[kh * KV_REP + r], sink)
        for sb in range(ATTN_TILE // WINDOW):
            toks = slice(sb * WINDOW, (sb + 1) * WINDOW)
            keys = slice(sb * WINDOW, sb * WINDOW + 2 * WINDOW)
            qt = jnp.concatenate(
                [qt_ref[(kh * KV_REP + r) * HEAD_DIM:(kh * KV_REP + r + 1) * HEAD_DIM, toks] for r in range(KV_REP)],
                axis=1)
            s = _dot(kk[keys, feats], qt)
            s = jnp.where(first_band if sb == 0 else band, s, MASK_VALUE)
            m = jnp.maximum(jnp.max(s, axis=0, keepdims=True), sink)
            pr = jnp.exp(s - m)
            den = jnp.sum(pr, axis=0, keepdims=True) + jnp.exp(sink - m)
            ot = _dot(vvt[feats, keys], pr.astype(BF16)) * (1.0 / den)
            for r in range(KV_REP):
                h = kh * KV_REP + r
                attnt_ref[h * HEAD_DIM:(h + 1) * HEAD_DIM, toks] = ot[:, r * WINDOW:(r + 1) * WINDOW].astype(BF16)
    attn_out = lax.dot_general(attnt_ref[...], wo_ref[...], (((0,), (0,)), ((), ())), preferred_element_type=F32)
    o_ref[...] = x_ref[...] + attn_out + bo_ref[...]


def _swa_attention(sinks, x, qt, k, vt, wo, bo):
    t = ATTN_TILE
    blocks_per_tile = t // WINDOW
    tok = lambda w: pl.BlockSpec((t, w), lambda i: (i, 0))
    feat = lambda w: pl.BlockSpec((w, t), lambda i: (0, i))
    prev_block = lambda i: jnp.maximum(i * blocks_per_tile - 1, 0)
    k_prev = pl.BlockSpec((WINDOW, KV_WIDTH), lambda i: (prev_block(i), 0))
    vt_prev = pl.BlockSpec((KV_WIDTH, WINDOW), lambda i: (0, prev_block(i)))
    return pl.pallas_call(
        _attn_kernel,
        grid=(N_TOK // t,),
        in_specs=[pl.BlockSpec(memory_space=pltpu.SMEM), tok(D_MODEL), feat(D_MODEL), k_prev, tok(KV_WIDTH),
                  vt_prev, feat(KV_WIDTH), _resident((D_MODEL, D_MODEL)), _resident((1, D_MODEL))],
        out_specs=tok(D_MODEL),
        out_shape=jax.ShapeDtypeStruct((N_TOK, D_MODEL), F32),
        scratch_shapes=[pltpu.VMEM((D_MODEL, t), BF16)],
        compiler_params=pltpu.CompilerParams(dimension_semantics=("arbitrary",), vmem_limit_bytes=VMEM_LIMIT),
        name="swa_attention",
    )(sinks, x, qt, k, k, vt, vt, wo, bo)


def _router_kernel(x_ref, g_ref, wr_ref, hn_ref, idx_ref, wts_ref, rank_ref, cnt_ref):
    t = ROUTER_TILE

    @pl.when(pl.program_id(0) == 0)
    def _():
        cnt_ref[...] = jnp.zeros_like(cnt_ref)

    hn = _rms(x_ref[...], g_ref[...])
    hn_ref[...] = hn
    logits = jnp.dot(hn, wr_ref[...], preferred_element_type=F32, precision=lax.Precision.HIGHEST)
    lane = lax.broadcasted_iota(jnp.int32, (t, LANES), 1)
    logits = jnp.where(lane < N_EXPERTS, logits, -jnp.inf)

    m1 = jnp.max(logits, axis=-1, keepdims=True)
    i1 = jnp.min(jnp.where(logits == m1, lane, LANES), axis=-1, keepdims=True)
    rest = jnp.where(lane == i1, -jnp.inf, logits)
    m2 = jnp.max(rest, axis=-1, keepdims=True)
    i2 = jnp.min(jnp.where(rest == m2, lane, LANES), axis=-1, keepdims=True)
    e2 = jnp.exp(m2 - m1)
    w1 = 1.0 / (1.0 + e2)
    w2 = e2 / (1.0 + e2)

    chosen = (lane == i1) | (lane == i2)
    onehot = jnp.where(chosen, 1.0, 0.0).astype(BF16)
    r = lax.broadcasted_iota(jnp.int32, (t, t), 0)
    c = lax.broadcasted_iota(jnp.int32, (t, t), 1)
    before = jnp.where(c < r, 1.0, 0.0).astype(BF16)
    seen = _dot(before, onehot) + cnt_ref[...]
    rank1 = jnp.sum(jnp.where(lane == i1, seen, 0.0), axis=-1, keepdims=True)
    rank2 = jnp.sum(jnp.where(lane == i2, seen, 0.0), axis=-1, keepdims=True)
    cnt_ref[...] = cnt_ref[...] + jnp.sum(onehot.astype(F32), axis=0, keepdims=True)

    idx_ref[...] = jnp.concatenate([i1, i2], axis=1)
    wts_ref[...] = jnp.concatenate([w1, w2], axis=1)
    rank_ref[...] = jnp.concatenate([rank1, rank2], axis=1).astype(jnp.int32)


def _moe_router(x, g, w_router_padded):
    t = ROUTER_TILE
    tok = lambda w: pl.BlockSpec((t, w), lambda i: (i, 0))
    return pl.pallas_call(
        _router_kernel,
        grid=(N_TOK // t,),
        in_specs=[tok(D_MODEL), _resident((1, D_MODEL)), _resident((D_MODEL, LANES))],
        out_specs=[tok(D_MODEL), tok(2), tok(2), tok(2), pl.BlockSpec((1, LANES), lambda i: (0, 0))],
        out_shape=[jax.ShapeDtypeStruct((N_TOK, D_MODEL), F32), jax.ShapeDtypeStruct((N_TOK, 2), jnp.int32),
                   jax.ShapeDtypeStruct((N_TOK, 2), F32), jax.ShapeDtypeStruct((N_TOK, 2), jnp.int32),
                   jax.ShapeDtypeStruct((1, LANES), F32)],
        compiler_params=pltpu.CompilerParams(dimension_semantics=("arbitrary",), vmem_limit_bytes=VMEM_LIMIT),
        name="moe_router",
    )(x, g, w_router_padded)


def _dispatch_kernel(pad_start_ref, pad_len_ref, nv_ref, pos_ref, hn_ref, xs_ref, zero_ref, sem):
    def row_copy(t, k):
        return pltpu.make_async_copy(hn_ref.at[pl.ds(t, 1)], xs_ref.at[pl.ds(pos_ref[2 * t + k], 1)], sem)

    def issue(t, carry):
        row_copy(t, 0).start()
        row_copy(t, 1).start()
        return carry

    def drain(t, carry):
        row_copy(t, 0).wait()
        row_copy(t, 1).wait()
        return carry

    lax.fori_loop(0, DISPATCH_TILE, issue, 0, unroll=ROW_COPY_UNROLL)
    lax.fori_loop(0, DISPATCH_TILE, drain, 0, unroll=ROW_COPY_UNROLL)

    @pl.when(pl.program_id(0) == pl.num_programs(0) - 1)
    def _():
        zero_ref[...] = jnp.zeros_like(zero_ref)
        for e in range(N_EXPERTS):
            def zero_row(r, e=e):
                return pltpu.make_async_copy(zero_ref.at[pl.ds(0, 1)],
                                             xs_ref.at[pl.ds(pad_start_ref[e] + r, 1)], sem)

            def row_issue(r, carry, zero_row=zero_row):
                zero_row(r).start()
                return carry

            def row_drain(r, carry, zero_row=zero_row):
                zero_row(r).wait()
                return carry

            lax.fori_loop(0, pad_len_ref[e], row_issue, 0)
            lax.fori_loop(0, pad_len_ref[e], row_drain, 0)

        def zero_tile(i):
            return pltpu.make_async_copy(zero_ref, xs_ref.at[pl.ds(i * EXPERT_TILE, EXPERT_TILE)], sem)

        def tile_issue(i, carry):
            zero_tile(i).start()
            return carry

        def tile_drain(i, carry):
            zero_tile(i).wait()
            return carry

        lax.fori_loop(nv_ref[0], N_ROW_TILES, tile_issue, 0)
        lax.fori_loop(nv_ref[0], N_ROW_TILES, tile_drain, 0)


def _moe_dispatch(pad_start, pad_len, n_valid, pos_flat, hn):
    t = DISPATCH_TILE
    grid_spec = pltpu.PrefetchScalarGridSpec(
        num_scalar_prefetch=3,
        grid=(N_TOK // t,),
        in_specs=[pl.BlockSpec((2 * t,), lambda i, ps, pn, nv: (i,), memory_space=pltpu.SMEM),
                  pl.BlockSpec((t, D_MODEL), lambda i, ps, pn, nv: (i, 0))],
        out_specs=pl.BlockSpec(memory_space=pl.ANY),
        scratch_shapes=[pltpu.VMEM((EXPERT_TILE, D_MODEL), F32), pltpu.SemaphoreType.DMA(())],
    )
    return pl.pallas_call(
        _dispatch_kernel,
        grid_spec=grid_spec,
        out_shape=jax.ShapeDtypeStruct((SORTED_ROWS, D_MODEL), F32),
        compiler_params=pltpu.CompilerParams(dimension_semantics=("arbitrary",), has_side_effects=True,
                                             vmem_limit_bytes=VMEM_LIMIT),
        name="moe_dispatch",
    )(pad_start, pad_len, n_valid, pos_flat, hn)


def _experts_kernel(te_ref, nv_ref, xs_ref, wg_ref, wu_ref, wd_ref, ys_ref, xb_ref):
    del te_ref
    i = pl.program_id(0)
    j = pl.program_id(1)

    @pl.when(i < nv_ref[0])
    def _():
        @pl.when(j == 0)
        def _():
            xb_ref[...] = xs_ref[...].astype(BF16)

        xb = xb_ref[...]
        a = (_silu(_dot(xb, wg_ref[...])) * _dot(xb, wu_ref[...])).astype(BF16)
        y = _dot(a, wd_ref[...])

        @pl.when(j == 0)
        def _():
            ys_ref[...] = y

        @pl.when(j > 0)
        def _():
            ys_ref[...] += y

    @pl.when((i >= nv_ref[0]) & (j == 0))
    def _():
        ys_ref[...] = jnp.zeros_like(ys_ref)


def _moe_experts(tile_expert, n_valid, xs, wg, wu, wd):
    tm, fb = EXPERT_TILE, EXPERT_FF_BLOCK
    n_fb = D_FF_EXPERT // fb
    rows = lambda i, j, te, nv: jnp.minimum(i, nv[0] - 1)
    ff = lambda i, j, te, nv: jnp.where(i < nv[0], j, n_fb - 1)
    grid_spec = pltpu.PrefetchScalarGridSpec(
        num_scalar_prefetch=2,
        grid=(N_ROW_TILES, n_fb),
        in_specs=[pl.BlockSpec((tm, D_MODEL), lambda i, j, te, nv: (rows(i, j, te, nv), 0)),
                  pl.BlockSpec((None, D_MODEL, fb), lambda i, j, te, nv: (te[i], 0, ff(i, j, te, nv))),
                  pl.BlockSpec((None, D_MODEL, fb), lambda i, j, te, nv: (te[i], 0, ff(i, j, te, nv))),
                  pl.BlockSpec((None, fb, D_MODEL), lambda i, j, te, nv: (te[i], ff(i, j, te, nv), 0))],
        out_specs=pl.BlockSpec((tm, D_MODEL), lambda i, j, te, nv: (i, 0)),
        scratch_shapes=[pltpu.VMEM((tm, D_MODEL), BF16)],
    )
    return pl.pallas_call(
        _experts_kernel,
        grid_spec=grid_spec,
        out_shape=jax.ShapeDtypeStruct((SORTED_ROWS, D_MODEL), F32),
        compiler_params=pltpu.CompilerParams(dimension_semantics=("arbitrary", "arbitrary"),
                                             vmem_limit_bytes=VMEM_LIMIT),
        name="moe_experts",
    )(tile_expert, n_valid, xs, wg, wu, wd)


def _combine_kernel(pos_ref, pos_next_ref, x_ref, wts_ref, p_ref, ys_ref, wproj_ref, pg_ref, wgate_ref, fg_ref,
                    o_ref, rows_ref, sems):
    t = COMBINE_TILE
    i = pl.program_id(0)
    slot = i % 2

    def row_copy(pref, slot, r, k):
        return pltpu.make_async_copy(ys_ref.at[pl.ds(pref[2 * r + k], 1)], rows_ref.at[slot, k, pl.ds(r, 1)],
                                     sems.at[slot])

    def issue(pref, slot):
        def body(r, carry):
            row_copy(pref, slot, r, 0).start()
            row_copy(pref, slot, r, 1).start()
            return carry
        lax.fori_loop(0, t, body, 0, unroll=ROW_COPY_UNROLL)

    def drain(pref, slot):
        def body(r, carry):
            row_copy(pref, slot, r, 0).wait()
            row_copy(pref, slot, r, 1).wait()
            return carry
        lax.fori_loop(0, t, body, 0, unroll=ROW_COPY_UNROLL)

    @pl.when(i == 0)
    def _():
        issue(pos_ref, 0)

    @pl.when(i + 1 < pl.num_programs(0))
    def _():
        issue(pos_next_ref, 1 - slot)

    drain(pos_ref, slot)

    w = wts_ref[...]
    x = x_ref[...] + w[:, 0:1] * rows_ref[slot, 0] + w[:, 1:2] * rows_ref[slot, 1]
    gate = _sigmoid(_dot(_rms(x, pg_ref[...]).astype(BF16), wgate_ref[...]))
    x = x + _dot(p_ref[...].astype(BF16), wproj_ref[...]) * gate
    o_ref[...] = _rms(x, fg_ref[...])


def _moe_combine_out(pos_flat, x, wts, p1, ys, wproj, pg, wgate, fg):
    t = COMBINE_TILE
    n = N_TOK // t
    tok = lambda w: pl.BlockSpec((t, w), lambda i: (i, 0))
    return pl.pallas_call(
        _combine_kernel,
        grid=(n,),
        in_specs=[pl.BlockSpec((2 * t,), lambda i: (i,), memory_space=pltpu.SMEM),
                  pl.BlockSpec((2 * t,), lambda i: (jnp.minimum(i + 1, n - 1),), memory_space=pltpu.SMEM),
                  tok(D_MODEL), tok(2), tok(PLE_DIM), pl.BlockSpec(memory_space=pl.ANY),
                  _resident((PLE_DIM, D_MODEL)), _resident((1, D_MODEL)), _resident((D_MODEL, D_MODEL)),
                  _resident((1, D_MODEL))],
        out_specs=tok(D_MODEL),
        out_shape=jax.ShapeDtypeStruct((N_TOK, D_MODEL), F32),
        scratch_shapes=[pltpu.VMEM((2, 2, t, D_MODEL), F32), pltpu.SemaphoreType.DMA((2,))],
        compiler_params=pltpu.CompilerParams(dimension_semantics=("arbitrary",), vmem_limit_bytes=VMEM_LIMIT),
        name="moe_combine_out",
    )(pos_flat, pos_flat, x, wts, p1, ys, wproj, pg, wgate, fg)


def _rope_tables():
    half = HEAD_DIM // 2
    freqs = ROPE_THETA ** (-jnp.arange(0, HEAD_DIM, 2, dtype=F32) / HEAD_DIM)
    ang = jnp.arange(SEQ, dtype=F32)[:, None] * freqs[None, :]
    cos, sin = jnp.cos(ang), jnp.sin(ang)
    reps = LANES // HEAD_DIM
    cos_t = jnp.tile(jnp.concatenate([cos, cos], axis=1), (1, reps))
    sin_t = jnp.tile(jnp.concatenate([-sin, sin], axis=1), (1, reps))
    del half
    return cos_t, sin_t, cos.T, sin.T


def _routing_plan(idx, rank, counts):
    tm = EXPERT_TILE
    tiles = (counts + tm - 1) // tm
    tile_end = jnp.cumsum(tiles)
    group_start = (tile_end - tiles) * tm
    pos = (group_start[idx] + rank).reshape(-1).astype(jnp.int32)
    n_valid = tile_end[-1]
    tile_ids = jnp.minimum(jnp.arange(N_ROW_TILES, dtype=jnp.int32), n_valid - 1)
    tile_expert = jnp.sum(tile_ids[:, None] >= tile_end[None, :], axis=1).astype(jnp.int32)
    pad_start = (group_start + counts).astype(jnp.int32)
    pad_len = (tiles * tm - counts).astype(jnp.int32)
    return pos, tile_expert, n_valid.reshape(1).astype(jnp.int32), pad_start, pad_len


def kernel(x, p, mix_norm_g, ffn_norm_g, gmlp_w_in, gmlp_b_in, gmlp_ln_g, gmlp_ln_b, gmlp_w_s, gmlp_b_s, gmlp_w_out, gmlp_b_out, kv_norm_g, w_kv, b_kv, attn_w_q, attn_b_q, attn_sinks, attn_w_o, attn_b_o, ffn_w_gate, ffn_w_up, ffn_w_down, moe_w_router, moe_w_gate, moe_w_up, moe_w_down, ple_w_proj, ple_norm_g, ple_w_gate, final_norm_g):
    row = lambda a: a.reshape(1, -1)
    bf = lambda a: a.astype(BF16)
    xf = x.reshape(N_TOK, D_MODEL)
    pf = p.reshape(2, N_TOK, PLE_DIM)

    x1 = _gmlp_mixer(xf, row(mix_norm_g[0]), bf(gmlp_w_in[0]), row(gmlp_b_in[0]), row(gmlp_ln_g[0]),
                     row(gmlp_ln_b[0]), gmlp_w_s[0], gmlp_b_s[0][:, :, None], bf(gmlp_w_out[0]),
                     row(gmlp_b_out[0]))

    col = lambda a: a.reshape(-1, 1)
    x3, qt, k, vt = _ffn_ple_qkv(x1, pf[0], _rope_tables(), row(ffn_norm_g[0]), bf(ffn_w_gate[0]),
                                 bf(ffn_w_up[0]), bf(ffn_w_down[0]), bf(ple_w_proj[0]), row(ple_norm_g[0]),
                                 bf(ple_w_gate[0]), row(mix_norm_g[1]), bf(attn_w_q[0].T), col(attn_b_q[0]),
                                 row(kv_norm_g), bf(w_kv[:, :KV_WIDTH]), row(b_kv[:KV_WIDTH]),
                                 bf(w_kv[:, KV_WIDTH:].T), col(b_kv[KV_WIDTH:]))

    x4 = _swa_attention(attn_sinks[0], x3, qt, k, vt, bf(attn_w_o[0]), row(attn_b_o[0]))

    w_router = jnp.pad(moe_w_router[0], ((0, 0), (0, LANES - N_EXPERTS)))
    hn, idx, wts, rank, counts = _moe_router(x4, row(ffn_norm_g[1]), w_router)
    pos, tile_expert, n_valid, pad_start, pad_len = _routing_plan(idx, rank,
                                                                  counts[0, :N_EXPERTS].astype(jnp.int32))

    xs = _moe_dispatch(pad_start, pad_len, n_valid, pos, hn)
    ys = _moe_experts(tile_expert, n_valid, xs, bf(moe_w_gate[0]), bf(moe_w_up[0]), bf(moe_w_down[0]))
    out = _moe_combine_out(pos, x4, wts, pf[1], ys, bf(ple_w_proj[1]), row(ple_norm_g[1]), bf(ple_w_gate[1]),
                           row(final_norm_g))
    return out.reshape(BATCH, SEQ, D_MODEL)
```

```python
import functools
import math

import jax
import jax.numpy as jnp
from jax import lax
from jax.experimental import pallas as pl
from jax.experimental.pallas import tpu as pltpu

F32 = jnp.float32
BF16 = jnp.bfloat16

D_MODEL = 1024
BATCH = 4
SEQ = 4096
N_TOK = BATCH * SEQ

CHUNK = 128
GMLP_FFN = 6 * D_MODEL
GMLP_HALF = GMLP_FFN // 2
GMLP_GROUPS = 8
GMLP_GROUP_DIM = GMLP_HALF // GMLP_GROUPS

N_HEADS = 16
N_KV_HEADS = 4
HEAD_DIM = 64
KV_REP = N_HEADS // N_KV_HEADS
KV_WIDTH = N_KV_HEADS * HEAD_DIM
WINDOW = 128
ROPE_THETA = 10000.0

D_FF_DENSE = 2816
N_EXPERTS = 8
D_FF_EXPERT = 3584
PLE_DIM = 256

EPS = 1e-6
MASK_VALUE = -1e30

LANES = 128
SUBLANES = 8
assert D_MODEL == SUBLANES * LANES

GMLP_TILE = 256
FFN_TILE = 256
ATTN_TILE = 512
ROUTER_TILE = 512
DISPATCH_TILE = 1024
ROW_COPY_UNROLL = 8
EXPERT_TILE = 1024
EXPERT_SUB = 512
EXPERT_FF_BLOCK = 512
COMBINE_TILE = 256

SORTED_ROWS = 2 * N_TOK + N_EXPERTS * EXPERT_TILE
N_ROW_TILES = SORTED_ROWS // EXPERT_TILE

VMEM_LIMIT = 56 * 1024 * 1024


def _resident(shape):
    zeros = (0,) * len(shape)
    return pl.BlockSpec(shape, lambda *_: zeros, pipeline_mode=pl.Buffered(1))


def _rms(x, g):
    return x * lax.rsqrt(jnp.mean(x * x, axis=-1, keepdims=True) + EPS) * g


def _gelu_tanh(x):
    c = math.sqrt(2.0 / math.pi)
    return 0.5 * x * (1.0 + jnp.tanh(c * (x + 0.044715 * (x * x * x))))


def _silu(x):
    return x * (1.0 / (1.0 + jnp.exp(-x)))


def _sigmoid(x):
    return 1.0 / (1.0 + jnp.exp(-x))


def _dot(a, b):
    return jnp.dot(a, b, preferred_element_type=F32)


def _store_token_tiles(ref, x):
    t = x.shape[0]
    for s in range(SUBLANES):
        ref[pl.ds(s, t, stride=SUBLANES), :] = x[:, s * LANES:(s + 1) * LANES]


def _tokens(ref, first, n):
    return ref.at[pl.ds(first * SUBLANES, n * SUBLANES)]


def _token(ref, r):
    return _tokens(ref, r, 1)


def _load_token_tiles(ref, t):
    return jnp.concatenate([ref[pl.ds(s, t, stride=SUBLANES), :] for s in range(SUBLANES)], axis=1)


def _gmlp_kernel(x_ref, g_ref, win_ref, bin_ref, lng_ref, lnb_ref, ws_ref, bs_ref,
                 wout_ref, bout_ref, o_ref):
    x = x_ref[...]
    h = _rms(x, g_ref[...]).astype(BF16)
    v = _gelu_tanh(_dot(h, win_ref[:, GMLP_HALF:]) + bin_ref[:, GMLP_HALF:])
    mu = jnp.mean(v, axis=-1, keepdims=True)
    vc = v - mu
    var = jnp.mean(vc * vc, axis=-1, keepdims=True)
    vn = (vc * lax.rsqrt(var + EPS) * lng_ref[...] + lnb_ref[...]).astype(BF16)

    row = lax.broadcasted_iota(jnp.int32, (CHUNK, CHUNK), 0)
    col = lax.broadcasted_iota(jnp.int32, (CHUNK, CHUNK), 1)
    causal = col <= row

    pair_w = 2 * GMLP_GROUP_DIM
    acc = x + bout_ref[...]
    for pair in range(GMLP_GROUPS // 2):
        c0 = pair * pair_w
        u = _gelu_tanh(_dot(h, win_ref[:, c0:c0 + pair_w]) + bin_ref[:, c0:c0 + pair_w])
        parts = []
        for gi in range(2 * pair, 2 * pair + 2):
            ws = jnp.where(causal, ws_ref[gi], 0.0).astype(BF16)
            bs = bs_ref[gi]
            vg = vn[:, gi * GMLP_GROUP_DIM:(gi + 1) * GMLP_GROUP_DIM]
            rows = [_dot(ws, vg[c * CHUNK:(c + 1) * CHUNK]) + bs for c in range(GMLP_TILE // CHUNK)]
            parts.append(jnp.concatenate(rows, axis=0))
        mixed = jnp.concatenate(parts, axis=1)
        gated = (u * mixed).astype(BF16)
        acc = acc + _dot(gated, wout_ref[c0:c0 + pair_w, :])
    o_ref[...] = acc


def _gmlp_mixer(x, g, w_in, b_in, ln_g, ln_b, w_s, b_s, w_out, b_out):
    t = GMLP_TILE
    tok = pl.BlockSpec((t, D_MODEL), lambda i: (i, 0))
    return pl.pallas_call(
        _gmlp_kernel,
        grid=(N_TOK // t,),
        in_specs=[tok, _resident((1, D_MODEL)), _resident((D_MODEL, GMLP_FFN)), _resident((1, GMLP_FFN)),
                  _resident((1, GMLP_HALF)), _resident((1, GMLP_HALF)),
                  _resident((GMLP_GROUPS, CHUNK, CHUNK)), _resident((GMLP_GROUPS, CHUNK, 1)),
                  _resident((GMLP_HALF, D_MODEL)), _resident((1, D_MODEL))],
        out_specs=tok,
        out_shape=jax.ShapeDtypeStruct((N_TOK, D_MODEL), F32),
        compiler_params=pltpu.CompilerParams(dimension_semantics=("arbitrary",), vmem_limit_bytes=VMEM_LIMIT),
        name="gmlp_mixer",
    )(x, g, w_in, b_in, ln_g, ln_b, w_s, b_s, w_out, b_out)


def _rope_slices(t, cos, sin_signed, first_half):
    outs = []
    for j in range(t.shape[1] // LANES):
        s = t[:, j * LANES:(j + 1) * LANES]
        partner = jnp.where(first_half, pltpu.roll(s, LANES - HEAD_DIM // 2, 1), pltpu.roll(s, HEAD_DIM // 2, 1))
        outs.append(s * cos + partner * sin_signed)
    return jnp.concatenate(outs, axis=1)


def _dot_nt(a, b):
    return lax.dot_general(a, b, (((1,), (1,)), ((), ())), preferred_element_type=F32)


def _ffn_ple_qkv_kernel(x_ref, p_ref, cos_ref, sin_ref, cost_ref, sint_ref, fg_ref, wg_ref, wu_ref, wd_ref,
                        wproj_ref, pg_ref, wgate_ref, mg_ref, wqt_ref, bq_ref, kg_ref, wk_ref, bk_ref,
                        wvt_ref, bv_ref, x_out, qt_out, k_out, vt_out):
    x = x_ref[...]
    hn = _rms(x, fg_ref[...]).astype(BF16)
    a = (_silu(_dot(hn, wg_ref[...])) * _dot(hn, wu_ref[...])).astype(BF16)
    x = x + _dot(a, wd_ref[...])
    gate = _sigmoid(_dot(_rms(x, pg_ref[...]).astype(BF16), wgate_ref[...]))
    x = x + _dot(p_ref[...].astype(BF16), wproj_ref[...]) * gate
    x_out[...] = x

    qt = _dot_nt(wqt_ref[...], _rms(x, mg_ref[...]).astype(BF16)) + bq_ref[...]
    cos_t = cost_ref[...]
    sin_t = sint_ref[...]
    half = HEAD_DIM // 2
    scale = 1.0 / math.sqrt(HEAD_DIM)
    for h in range(N_HEADS):
        t1 = qt[h * HEAD_DIM:h * HEAD_DIM + half]
        t2 = qt[h * HEAD_DIM + half:(h + 1) * HEAD_DIM]
        qt_out[h * HEAD_DIM:h * HEAD_DIM + half, :] = ((t1 * cos_t - t2 * sin_t) * scale).astype(BF16)
        qt_out[h * HEAD_DIM + half:(h + 1) * HEAD_DIM, :] = ((t2 * cos_t + t1 * sin_t) * scale).astype(BF16)

    hkv = _rms(x, kg_ref[...]).astype(BF16)
    lane = lax.broadcasted_iota(jnp.int32, cos_ref.shape, 1)
    first_half = (lane % HEAD_DIM) < half
    k = _rope_slices(_dot(hkv, wk_ref[...]) + bk_ref[...], cos_ref[...], sin_ref[...], first_half)
    k_out[...] = k.astype(BF16)
    vt_out[...] = (_dot_nt(wvt_ref[...], hkv) + bv_ref[...]).astype(BF16)


def _ffn_ple_qkv(x, p0, rope_tabs, fg, wg, wu, wd, wproj, pg, wgate, mg, wqt, bq_col, kg, wk, bk, wvt, bv_col):
    t = FFN_TILE
    per_seq = SEQ // t
    half = HEAD_DIM // 2
    tok = lambda w: pl.BlockSpec((t, w), lambda i: (i, 0))
    feat = lambda w: pl.BlockSpec((w, t), lambda i: (0, i))
    rope = pl.BlockSpec((t, LANES), lambda i: (i % per_seq, 0))
    rope_t = pl.BlockSpec((half, t), lambda i: (0, i % per_seq))
    cos_t, sin_t, cos_tt, sin_tt = rope_tabs
    return pl.pallas_call(
        _ffn_ple_qkv_kernel,
        grid=(N_TOK // t,),
        in_specs=[tok(D_MODEL), tok(PLE_DIM), rope, rope, rope_t, rope_t,
                  _resident((1, D_MODEL)), _resident((D_MODEL, D_FF_DENSE)), _resident((D_MODEL, D_FF_DENSE)),
                  _resident((D_FF_DENSE, D_MODEL)),
                  _resident((PLE_DIM, D_MODEL)), _resident((1, D_MODEL)), _resident((D_MODEL, D_MODEL)),
                  _resident((1, D_MODEL)), _resident((D_MODEL, D_MODEL)), _resident((D_MODEL, 1)),
                  _resident((1, D_MODEL)), _resident((D_MODEL, KV_WIDTH)), _resident((1, KV_WIDTH)),
                  _resident((KV_WIDTH, D_MODEL)), _resident((KV_WIDTH, 1))],
        out_specs=[tok(D_MODEL), feat(D_MODEL), tok(KV_WIDTH), feat(KV_WIDTH)],
        out_shape=[jax.ShapeDtypeStruct((N_TOK, D_MODEL), F32), jax.ShapeDtypeStruct((D_MODEL, N_TOK), BF16),
                   jax.ShapeDtypeStruct((N_TOK, KV_WIDTH), BF16), jax.ShapeDtypeStruct((KV_WIDTH, N_TOK), BF16)],
        compiler_params=pltpu.CompilerParams(dimension_semantics=("arbitrary",), vmem_limit_bytes=VMEM_LIMIT),
        name="ffn_ple_qkv",
    )(x, p0, cos_t, sin_t, cos_tt, sin_tt, fg, wg, wu, wd, wproj, pg, wgate, mg, wqt, bq_col, kg, wk, bk,
      wvt, bv_col)


def _attn_kernel(sink_ref, x_ref, qt_ref, kp_ref, kc_ref, vtp_ref, vtc_ref, wo_ref, bo_ref, o_ref, attnt_ref):
    i = pl.program_id(0)
    seq_start = (i % (SEQ // ATTN_TILE)) == 0
    kk = jnp.concatenate([kp_ref[...], kc_ref[...]], axis=0)
    vvt = jnp.concatenate([vtp_ref[...], vtc_ref[...]], axis=1)

    width = KV_REP * WINDOW
    kj = lax.broadcasted_iota(jnp.int32, (2 * WINDOW, width), 0)
    qi = lax.broadcasted_iota(jnp.int32, (2 * WINDOW, width), 1) % WINDOW
    band = (kj > qi) & (kj <= qi + WINDOW)
    first_band = band & ((kj >= WINDOW) | jnp.logical_not(seq_start))
    rep = lax.broadcasted_iota(jnp.int32, (1, width), 1) // WINDOW

    for kh in range(N_KV_HEADS):
        feats = slice(kh * HEAD_DIM, (kh + 1) * HEAD_DIM)
        sink = jnp.zeros((1, width), F32)
        for r in range(KV_REP):
            sink = jnp.where(rep == r, sink_ref[kh * KV_REP + r], sink)
        for sb in range(ATTN_TILE // WINDOW):
            toks = slice(sb * WINDOW, (sb + 1) * WINDOW)
            keys = slice(sb * WINDOW, sb * WINDOW + 2 * WINDOW)
            qt = jnp.concatenate(
                [qt_ref[(kh * KV_REP + r) * HEAD_DIM:(kh * KV_REP + r + 1) * HEAD_DIM, toks] for r in range(KV_REP)],
                axis=1)
            s = _dot(kk[keys, feats], qt)
            s = jnp.where(first_band if sb == 0 else band, s, MASK_VALUE)
            m = jnp.maximum(jnp.max(s, axis=0, keepdims=True), sink)
            pr = jnp.exp(s - m)
            den = jnp.sum(pr, axis=0, keepdims=True) + jnp.exp(sink - m)
            ot = _dot(vvt[feats, keys], pr.astype(BF16)) * (1.0 / den)
            for r in range(KV_REP):
                h = kh * KV_REP + r
                attnt_ref[h * HEAD_DIM:(h + 1) * HEAD_DIM, toks] = ot[:, r * WINDOW:(r + 1) * WINDOW].astype(BF16)
    attn_out = lax.dot_general(attnt_ref[...], wo_ref[...], (((0,), (0,)), ((), ())), preferred_element_type=F32)
    o_ref[...] = x_ref[...] + attn_out + bo_ref[...]


def _swa_attention(sinks, x, qt, k, vt, wo, bo):
    t = ATTN_TILE
    blocks_per_tile = t // WINDOW
    tok = lambda w: pl.BlockSpec((t, w), lambda i: (i, 0))
    feat = lambda w: pl.BlockSpec((w, t), lambda i: (0, i))
    prev_block = lambda i: jnp.maximum(i * blocks_per_tile - 1, 0)
    k_prev = pl.BlockSpec((WINDOW, KV_WIDTH), lambda i: (prev_block(i), 0))
    vt_prev = pl.BlockSpec((KV_WIDTH, WINDOW), lambda i: (0, prev_block(i)))
    return pl.pallas_call(
        _attn_kernel,
        grid=(N_TOK // t,),
        in_specs=[pl.BlockSpec(memory_space=pltpu.SMEM), tok(D_MODEL), feat(D_MODEL), k_prev, tok(KV_WIDTH),
                  vt_prev, feat(KV_WIDTH), _resident((D_MODEL, D_MODEL)), _resident((1, D_MODEL))],
        out_specs=tok(D_MODEL),
        out_shape=jax.ShapeDtypeStruct((N_TOK, D_MODEL), F32),
        scratch_shapes=[pltpu.VMEM((D_MODEL, t), BF16)],
        compiler_params=pltpu.CompilerParams(dimension_semantics=("arbitrary",), vmem_limit_bytes=VMEM_LIMIT),
        name="swa_attention",
    )(sinks, x, qt, k, k, vt, vt, wo, bo)


def _router_kernel(x_ref, g_ref, wr_ref, hn_ref, idx_ref, wts_ref, rank_ref, cnt_ref):
    t = ROUTER_TILE

    @pl.when(pl.program_id(0) == 0)
    def _():
        cnt_ref[...] = jnp.zeros_like(cnt_ref)

    hn = _rms(x_ref[...], g_ref[...])
    _store_token_tiles(hn_ref, hn)
    hn_hi = hn.astype(BF16)
    hn_lo = (hn - hn_hi.astype(F32)).astype(BF16)
    wr = wr_ref[...]
    wr_hi = wr.astype(BF16)
    wr_lo = (wr - wr_hi.astype(F32)).astype(BF16)
    logits = _dot(hn_hi, wr_hi) + (_dot(hn_hi, wr_lo) + _dot(hn_lo, wr_hi))
    lane = lax.broadcasted_iota(jnp.int32, (t, LANES), 1)
    logits = jnp.where(lane < N_EXPERTS, logits, -jnp.inf)

    m1 = jnp.max(logits, axis=-1, keepdims=True)
    i1 = jnp.min(jnp.where(logits == m1, lane, LANES), axis=-1, keepdims=True)
    rest = jnp.where(lane == i1, -jnp.inf, logits)
    m2 = jnp.max(rest, axis=-1, keepdims=True)
    i2 = jnp.min(jnp.where(rest == m2, lane, LANES), axis=-1, keepdims=True)
    e2 = jnp.exp(m2 - m1)
    w1 = 1.0 / (1.0 + e2)
    w2 = e2 / (1.0 + e2)

    chosen = (lane == i1) | (lane == i2)
    onehot = jnp.where(chosen, 1.0, 0.0).astype(BF16)
    r = lax.broadcasted_iota(jnp.int32, (t, t), 0)
    c = lax.broadcasted_iota(jnp.int32, (t, t), 1)
    before = jnp.where(c < r, 1.0, 0.0).astype(BF16)
    seen = _dot(before, onehot) + cnt_ref[...]
    rank1 = jnp.sum(jnp.where(lane == i1, seen, 0.0), axis=-1, keepdims=True)
    rank2 = jnp.sum(jnp.where(lane == i2, seen, 0.0), axis=-1, keepdims=True)
    cnt_ref[...] = cnt_ref[...] + jnp.sum(onehot.astype(F32), axis=0, keepdims=True)

    idx_ref[...] = jnp.concatenate([i1, i2], axis=1)
    wts_ref[...] = jnp.concatenate([w1, w2], axis=1)
    rank_ref[...] = jnp.concatenate([rank1, rank2], axis=1).astype(jnp.int32)


def _moe_router(x, g, w_router_padded):
    t = ROUTER_TILE
    tok = lambda w: pl.BlockSpec((t, w), lambda i: (i, 0))
    return pl.pallas_call(
        _router_kernel,
        grid=(N_TOK // t,),
        in_specs=[tok(D_MODEL), _resident((1, D_MODEL)), _resident((D_MODEL, LANES))],
        out_specs=[pl.BlockSpec((t * SUBLANES, LANES), lambda i: (i, 0)), tok(2), tok(2), tok(2),
                   pl.BlockSpec((1, LANES), lambda i: (0, 0))],
        out_shape=[jax.ShapeDtypeStruct((N_TOK * SUBLANES, LANES), F32), jax.ShapeDtypeStruct((N_TOK, 2), jnp.int32),
                   jax.ShapeDtypeStruct((N_TOK, 2), F32), jax.ShapeDtypeStruct((N_TOK, 2), jnp.int32),
                   jax.ShapeDtypeStruct((1, LANES), F32)],
        compiler_params=pltpu.CompilerParams(dimension_semantics=("arbitrary",), vmem_limit_bytes=VMEM_LIMIT),
        name="moe_router",
    )(x, g, w_router_padded)


def _dispatch_kernel(pad_start_ref, pad_len_ref, nv_ref, pos_ref, hn_ref, xs_ref, zero_ref, sem):
    def row_copy(t, k):
        return pltpu.make_async_copy(_token(hn_ref, t), _token(xs_ref, pos_ref[2 * t + k]), sem)

    def issue(t, carry):
        row_copy(t, 0).start()
        row_copy(t, 1).start()
        return carry

    def drain(t, carry):
        row_copy(t, 0).wait()
        row_copy(t, 1).wait()
        return carry

    lax.fori_loop(0, DISPATCH_TILE, issue, 0, unroll=ROW_COPY_UNROLL)
    lax.fori_loop(0, DISPATCH_TILE, drain, 0, unroll=ROW_COPY_UNROLL)

    @pl.when(pl.program_id(0) == pl.num_programs(0) - 1)
    def _():
        zero_ref[...] = jnp.zeros_like(zero_ref)
        for e in range(N_EXPERTS):
            def zero_row(r, e=e):
                return pltpu.make_async_copy(_token(zero_ref, 0), _token(xs_ref, pad_start_ref[e] + r), sem)

            def row_issue(r, carry, zero_row=zero_row):
                zero_row(r).start()
                return carry

            def row_drain(r, carry, zero_row=zero_row):
                zero_row(r).wait()
                return carry

            lax.fori_loop(0, pad_len_ref[e], row_issue, 0)
            lax.fori_loop(0, pad_len_ref[e], row_drain, 0)

        def zero_tile(i):
            return pltpu.make_async_copy(zero_ref, _tokens(xs_ref, i * EXPERT_TILE, EXPERT_TILE), sem)

        def tile_issue(i, carry):
            zero_tile(i).start()
            return carry

        def tile_drain(i, carry):
            zero_tile(i).wait()
            return carry

        lax.fori_loop(nv_ref[0], N_ROW_TILES, tile_issue, 0)
        lax.fori_loop(nv_ref[0], N_ROW_TILES, tile_drain, 0)


def _moe_dispatch(pad_start, pad_len, n_valid, pos_flat, hn):
    t = DISPATCH_TILE
    grid_spec = pltpu.PrefetchScalarGridSpec(
        num_scalar_prefetch=3,
        grid=(N_TOK // t,),
        in_specs=[pl.BlockSpec((2 * t,), lambda i, ps, pn, nv: (i,), memory_space=pltpu.SMEM),
                  pl.BlockSpec((t * SUBLANES, LANES), lambda i, ps, pn, nv: (i, 0))],
        out_specs=pl.BlockSpec(memory_space=pl.ANY),
        scratch_shapes=[pltpu.VMEM((EXPERT_TILE * SUBLANES, LANES), F32), pltpu.SemaphoreType.DMA(())],
    )
    return pl.pallas_call(
        _dispatch_kernel,
        grid_spec=grid_spec,
        out_shape=jax.ShapeDtypeStruct((SORTED_ROWS * SUBLANES, LANES), F32),
        compiler_params=pltpu.CompilerParams(dimension_semantics=("arbitrary",), has_side_effects=True,
                                             vmem_limit_bytes=VMEM_LIMIT),
        name="moe_dispatch",
    )(pad_start, pad_len, n_valid, pos_flat, hn)


def _experts_kernel(te_ref, nr_ref, nv_ref, xs_ref, wg_ref, wu_ref, wd_ref, ys_ref,
                    xb_ref, wgb_ref, wub_ref, wdb_ref, acc_ref):
    del te_ref
    i = pl.program_id(0)
    j = pl.program_id(1)
    last = pl.num_programs(1) - 1
    sub = EXPERT_SUB
    n_rows = nr_ref[i]

    @pl.when(i < nv_ref[0])
    def _():
        for sb in range(EXPERT_TILE // sub):
            rows = slice(sb * sub, (sb + 1) * sub)
            tiles = lambda s, sb=sb: pl.ds(sb * sub * SUBLANES + s, sub, stride=SUBLANES)

            @pl.when(sb * sub < n_rows)
            def _(sb=sb, rows=rows, tiles=tiles):
                @pl.when(j == 0)
                def _():
                    for s in range(SUBLANES):
                        xb_ref[rows, s * LANES:(s + 1) * LANES] = xs_ref[tiles(s), :].astype(BF16)

                if sb == 0:
                    wg, wu, wd = wg_ref[...].astype(BF16), wu_ref[...].astype(BF16), wd_ref[...].astype(BF16)
                    wgb_ref[...], wub_ref[...], wdb_ref[...] = wg, wu, wd
                else:
                    wg, wu, wd = wgb_ref[...], wub_ref[...], wdb_ref[...]
                xb = xb_ref[rows, :]
                a = (_silu(_dot(xb, wg)) * _dot(xb, wu)).astype(BF16)
                y = _dot(a, wd)

                @pl.when(j == 0)
                def _():
                    acc_ref[rows, :] = y

                @pl.when((j > 0) & (j < last))
                def _():
                    acc_ref[rows, :] += y

                @pl.when(j == last)
                def _():
                    total = acc_ref[rows, :] + y
                    for s in range(SUBLANES):
                        ys_ref[tiles(s), :] = total[:, s * LANES:(s + 1) * LANES]

            @pl.when((sb * sub >= n_rows) & (j == last))
            def _(sb=sb):
                ys_ref[sb * sub * SUBLANES:(sb + 1) * sub * SUBLANES, :] = jnp.zeros((sub * SUBLANES, LANES), F32)

    @pl.when((i >= nv_ref[0]) & (j == last))
    def _():
        ys_ref[...] = jnp.zeros_like(ys_ref)


def _moe_experts(tile_expert, tile_rows, n_valid, xs, wg, wu, wd):
    tm, fb = EXPERT_TILE, EXPERT_FF_BLOCK
    n_fb = D_FF_EXPERT // fb
    assert n_fb >= 2
    rows = lambda i, j, te, nr, nv: jnp.maximum(jnp.minimum(i, nv[0] - 1), 0)
    ff = lambda i, j, te, nr, nv: jnp.where(i < nv[0], j, n_fb - 1)
    grid_spec = pltpu.PrefetchScalarGridSpec(
        num_scalar_prefetch=3,
        grid=(N_ROW_TILES, n_fb),
        in_specs=[pl.BlockSpec((tm * SUBLANES, LANES), lambda i, j, te, nr, nv: (rows(i, j, te, nr, nv), 0)),
                  pl.BlockSpec((None, D_MODEL, fb), lambda i, j, te, nr, nv: (te[i], 0, ff(i, j, te, nr, nv))),
                  pl.BlockSpec((None, D_MODEL, fb), lambda i, j, te, nr, nv: (te[i], 0, ff(i, j, te, nr, nv))),
                  pl.BlockSpec((None, fb, D_MODEL), lambda i, j, te, nr, nv: (te[i], ff(i, j, te, nr, nv), 0))],
        out_specs=pl.BlockSpec((tm * SUBLANES, LANES), lambda i, j, te, nr, nv: (i, 0)),
        scratch_shapes=[pltpu.VMEM((tm, D_MODEL), BF16), pltpu.VMEM((D_MODEL, fb), BF16),
                        pltpu.VMEM((D_MODEL, fb), BF16), pltpu.VMEM((fb, D_MODEL), BF16),
                        pltpu.VMEM((tm, D_MODEL), F32)],
    )
    return pl.pallas_call(
        _experts_kernel,
        grid_spec=grid_spec,
        out_shape=jax.ShapeDtypeStruct((SORTED_ROWS * SUBLANES, LANES), F32),
        compiler_params=pltpu.CompilerParams(dimension_semantics=("arbitrary", "arbitrary"),
                                             vmem_limit_bytes=VMEM_LIMIT),
        name="moe_experts",
    )(tile_expert, tile_rows, n_valid, xs, wg, wu, wd)


def _combine_kernel(pos_ref, pos_next_ref, x_ref, wts_ref, p_ref, ys_ref, wproj_ref, pg_ref, wgate_ref, fg_ref,
                    o_ref, rows_ref, sems):
    t = COMBINE_TILE
    i = pl.program_id(0)
    slot = i % 2

    def row_copy(pref, slot, r, k):
        return pltpu.make_async_copy(_token(ys_ref, pref[2 * r + k]), _token(rows_ref.at[slot, k], r),
                                     sems.at[slot])

    def issue(pref, slot):
        def body(r, carry):
            row_copy(pref, slot, r, 0).start()
            row_copy(pref, slot, r, 1).start()
            return carry
        lax.fori_loop(0, t, body, 0, unroll=ROW_COPY_UNROLL)

    def drain(pref, slot):
        def body(r, carry):
            row_copy(pref, slot, r, 0).wait()
            row_copy(pref, slot, r, 1).wait()
            return carry
        lax.fori_loop(0, t, body, 0, unroll=ROW_COPY_UNROLL)

    @pl.when(i == 0)
    def _():
        issue(pos_ref, 0)

    @pl.when(i + 1 < pl.num_programs(0))
    def _():
        issue(pos_next_ref, 1 - slot)

    drain(pos_ref, slot)

    w = wts_ref[...]
    x = (x_ref[...] + w[:, 0:1] * _load_token_tiles(rows_ref.at[slot, 0], t)
         + w[:, 1:2] * _load_token_tiles(rows_ref.at[slot, 1], t))
    gate = _sigmoid(_dot(_rms(x, pg_ref[...]).astype(BF16), wgate_ref[...]))
    x = x + _dot(p_ref[...].astype(BF16), wproj_ref[...]) * gate
    o_ref[...] = _rms(x, fg_ref[...])


def _moe_combine_out(pos_flat, x, wts, p1, ys, wproj, pg, wgate, fg):
    t = COMBINE_TILE
    n = N_TOK // t
    tok = lambda w: pl.BlockSpec((t, w), lambda i: (i, 0))
    return pl.pallas_call(
        _combine_kernel,
        grid=(n,),
        in_specs=[pl.BlockSpec((2 * t,), lambda i: (i,), memory_space=pltpu.SMEM),
                  pl.BlockSpec((2 * t,), lambda i: (jnp.minimum(i + 1, n - 1),), memory_space=pltpu.SMEM),
                  tok(D_MODEL), tok(2), tok(PLE_DIM), pl.BlockSpec(memory_space=pl.ANY),
                  _resident((PLE_DIM, D_MODEL)), _resident((1, D_MODEL)), _resident((D_MODEL, D_MODEL)),
                  _resident((1, D_MODEL))],
        out_specs=tok(D_MODEL),
        out_shape=jax.ShapeDtypeStruct((N_TOK, D_MODEL), F32),
        scratch_shapes=[pltpu.VMEM((2, 2, t * SUBLANES, LANES), F32), pltpu.SemaphoreType.DMA((2,))],
        compiler_params=pltpu.CompilerParams(dimension_semantics=("arbitrary",), vmem_limit_bytes=VMEM_LIMIT),
        name="moe_combine_out",
    )(pos_flat, pos_flat, x, wts, p1, ys, wproj, pg, wgate, fg)


def _rope_tables():
    half = HEAD_DIM // 2
    freqs = ROPE_THETA ** (-jnp.arange(0, HEAD_DIM, 2, dtype=F32) / HEAD_DIM)
    ang = jnp.arange(SEQ, dtype=F32)[:, None] * freqs[None, :]
    cos, sin = jnp.cos(ang), jnp.sin(ang)
    reps = LANES // HEAD_DIM
    cos_t = jnp.tile(jnp.concatenate([cos, cos], axis=1), (1, reps))
    sin_t = jnp.tile(jnp.concatenate([-sin, sin], axis=1), (1, reps))
    del half
    return cos_t, sin_t, cos.T, sin.T


def _routing_plan(idx, rank, counts):
    tm = EXPERT_TILE
    tiles = (counts + tm - 1) // tm
    tile_end = jnp.cumsum(tiles)
    group_start = (tile_end - tiles) * tm
    pos = (group_start[idx] + rank).reshape(-1).astype(jnp.int32)
    n_valid = tile_end[-1]
    tile_ids = jnp.minimum(jnp.arange(N_ROW_TILES, dtype=jnp.int32), n_valid - 1)
    tile_expert = jnp.sum(tile_ids[:, None] >= tile_end[None, :], axis=1).astype(jnp.int32)
    tile_first = (tile_end - tiles)[tile_expert]
    tile_rows = jnp.clip(counts[tile_expert] - (tile_ids - tile_first) * tm, 0, tm).astype(jnp.int32)
    pad_start = (group_start + counts).astype(jnp.int32)
    pad_len = (tiles * tm - counts).astype(jnp.int32)
    return pos, tile_expert, tile_rows, n_valid.reshape(1).astype(jnp.int32), pad_start, pad_len


def kernel(x, p, mix_norm_g, ffn_norm_g, gmlp_w_in, gmlp_b_in, gmlp_ln_g, gmlp_ln_b, gmlp_w_s, gmlp_b_s, gmlp_w_out, gmlp_b_out, kv_norm_g, w_kv, b_kv, attn_w_q, attn_b_q, attn_sinks, attn_w_o, attn_b_o, ffn_w_gate, ffn_w_up, ffn_w_down, moe_w_router, moe_w_gate, moe_w_up, moe_w_down, ple_w_proj, ple_norm_g, ple_w_gate, final_norm_g):
    row = lambda a: a.reshape(1, -1)
    bf = lambda a: a.astype(BF16)
    xf = x.reshape(N_TOK, D_MODEL)
    pf = p.reshape(2, N_TOK, PLE_DIM)

    x1 = _gmlp_mixer(xf, row(mix_norm_g[0]), bf(gmlp_w_in[0]), row(gmlp_b_in[0]), row(gmlp_ln_g[0]),
                     row(gmlp_ln_b[0]), gmlp_w_s[0], gmlp_b_s[0][:, :, None], bf(gmlp_w_out[0]),
                     row(gmlp_b_out[0]))

    col = lambda a: a.reshape(-1, 1)
    x3, qt, k, vt = _ffn_ple_qkv(x1, pf[0], _rope_tables(), row(ffn_norm_g[0]), bf(ffn_w_gate[0]),
                                 bf(ffn_w_up[0]), bf(ffn_w_down[0]), bf(ple_w_proj[0]), row(ple_norm_g[0]),
                                 bf(ple_w_gate[0]), row(mix_norm_g[1]), bf(attn_w_q[0].T), col(attn_b_q[0]),
                                 row(kv_norm_g), bf(w_kv[:, :KV_WIDTH]), row(b_kv[:KV_WIDTH]),
                                 bf(w_kv[:, KV_WIDTH:].T), col(b_kv[KV_WIDTH:]))

    x4 = _swa_attention(attn_sinks[0], x3, qt, k, vt, bf(attn_w_o[0]), row(attn_b_o[0]))

    w_router = jnp.pad(moe_w_router[0], ((0, 0), (0, LANES - N_EXPERTS)))
    hn, idx, wts, rank, counts = _moe_router(x4, row(ffn_norm_g[1]), w_router)
    pos, tile_expert, tile_rows, n_valid, pad_start, pad_len = _routing_plan(
        idx, rank, counts[0, :N_EXPERTS].astype(jnp.int32))

    xs = _moe_dispatch(pad_start, pad_len, n_valid, pos, hn)
    ys = _moe_experts(tile_expert, tile_rows, n_valid, xs, moe_w_gate[0], moe_w_up[0], moe_w_down[0])
    out = _moe_combine_out(pos, x4, wts, pf[1], ys, bf(ple_w_proj[1]), row(ple_norm_g[1]), bf(ple_w_gate[1]),
                           row(final_norm_g))
    return out.reshape(BATCH, SEQ, D_MODEL)
```

```python
import functools
import math

import jax
import jax.numpy as jnp
from jax import lax
from jax.experimental import pallas as pl
from jax.experimental.pallas import tpu as pltpu

F32 = jnp.float32
BF16 = jnp.bfloat16

D_MODEL = 1024
BATCH = 4
SEQ = 4096
N_TOK = BATCH * SEQ

CHUNK = 128
GMLP_FFN = 6 * D_MODEL
GMLP_HALF = GMLP_FFN // 2
GMLP_GROUPS = 8
GMLP_GROUP_DIM = GMLP_HALF // GMLP_GROUPS

N_HEADS = 16
N_KV_HEADS = 4
HEAD_DIM = 64
KV_REP = N_HEADS // N_KV_HEADS
KV_WIDTH = N_KV_HEADS * HEAD_DIM
WINDOW = 128
ROPE_THETA = 10000.0

D_FF_DENSE = 2816
N_EXPERTS = 8
D_FF_EXPERT = 3584
PLE_DIM = 256

EPS = 1e-6
MASK_VALUE = -1e30

LANES = 128
SUBLANES = 8
assert D_MODEL == SUBLANES * LANES

GMLP_TILE = 256
FFN_TILE = 256
ATTN_TILE = 512
ROUTER_TILE = 512
DISPATCH_TILE = 1024
ROW_COPY_UNROLL = 8
EXPERT_TILE = 512
EXPERT_SUB = 512
EXPERT_FF_BLOCK = 1792
COMBINE_TILE = 256

SORTED_ROWS = 2 * N_TOK + N_EXPERTS * EXPERT_TILE
N_ROW_TILES = SORTED_ROWS // EXPERT_TILE

VMEM_LIMIT = 56 * 1024 * 1024


def _ple_input(t, layer):
    return pl.BlockSpec((None, t, PLE_DIM), lambda i: (layer, i, 0))


def _resident(shape):
    zeros = (0,) * len(shape)
    return pl.BlockSpec(shape, lambda *_: zeros, pipeline_mode=pl.Buffered(1))


def _rms(x, g):
    return x * lax.rsqrt(jnp.mean(x * x, axis=-1, keepdims=True) + EPS) * g


def _gelu_tanh(x):
    c = math.sqrt(2.0 / math.pi)
    return 0.5 * x * (1.0 + jnp.tanh(c * (x + 0.044715 * (x * x * x))))


def _silu(x):
    return x * (1.0 / (1.0 + jnp.exp(-x)))


def _sigmoid(x):
    return 1.0 / (1.0 + jnp.exp(-x))


def _dot(a, b):
    return jnp.dot(a, b, preferred_element_type=F32)


def _store_token_tiles(ref, x):
    t = x.shape[0]
    for s in range(SUBLANES):
        ref[pl.ds(s, t, stride=SUBLANES), :] = x[:, s * LANES:(s + 1) * LANES]


def _tokens(ref, first, n):
    return ref.at[pl.ds(first * SUBLANES, n * SUBLANES)]


def _token(ref, r):
    return _tokens(ref, r, 1)


def _load_token_tiles(ref, t):
    return jnp.concatenate([ref[pl.ds(s, t, stride=SUBLANES), :] for s in range(SUBLANES)], axis=1)


def _gmlp_kernel(x_ref, g_ref, win_ref, bin_ref, lng_ref, lnb_ref, ws_ref, bs_ref,
                 wout_ref, bout_ref, o_ref):
    x = x_ref[...]
    h = _rms(x, g_ref[...]).astype(BF16)
    v = _gelu_tanh(_dot(h, win_ref[:, GMLP_HALF:]) + bin_ref[:, GMLP_HALF:])
    mu = jnp.mean(v, axis=-1, keepdims=True)
    vc = v - mu
    var = jnp.mean(vc * vc, axis=-1, keepdims=True)
    vn = (vc * lax.rsqrt(var + EPS) * lng_ref[...] + lnb_ref[...]).astype(BF16)

    row = lax.broadcasted_iota(jnp.int32, (CHUNK, CHUNK), 0)
    col = lax.broadcasted_iota(jnp.int32, (CHUNK, CHUNK), 1)
    causal = col <= row

    pair_w = 2 * GMLP_GROUP_DIM
    acc = x + bout_ref[...]
    for pair in range(GMLP_GROUPS // 2):
        c0 = pair * pair_w
        u = _gelu_tanh(_dot(h, win_ref[:, c0:c0 + pair_w]) + bin_ref[:, c0:c0 + pair_w])
        parts = []
        for gi in range(2 * pair, 2 * pair + 2):
            ws = jnp.where(causal, ws_ref[gi], 0.0).astype(BF16)
            bs = bs_ref[gi]
            vg = vn[:, gi * GMLP_GROUP_DIM:(gi + 1) * GMLP_GROUP_DIM]
            rows = [_dot(ws, vg[c * CHUNK:(c + 1) * CHUNK]) + bs for c in range(GMLP_TILE // CHUNK)]
            parts.append(jnp.concatenate(rows, axis=0))
        mixed = jnp.concatenate(parts, axis=1)
        gated = (u * mixed).astype(BF16)
        acc = acc + _dot(gated, wout_ref[c0:c0 + pair_w, :])
    o_ref[...] = acc


def _gmlp_mixer(x, g, w_in, b_in, ln_g, ln_b, w_s, b_s, w_out, b_out):
    t = GMLP_TILE
    tok = pl.BlockSpec((t, D_MODEL), lambda i: (i, 0))
    return pl.pallas_call(
        _gmlp_kernel,
        grid=(N_TOK // t,),
        in_specs=[tok, _resident((1, D_MODEL)), _resident((D_MODEL, GMLP_FFN)), _resident((1, GMLP_FFN)),
                  _resident((1, GMLP_HALF)), _resident((1, GMLP_HALF)),
                  _resident((GMLP_GROUPS, CHUNK, CHUNK)), _resident((GMLP_GROUPS, CHUNK, 1)),
                  _resident((GMLP_HALF, D_MODEL)), _resident((1, D_MODEL))],
        out_specs=tok,
        out_shape=jax.ShapeDtypeStruct((N_TOK, D_MODEL), F32),
        compiler_params=pltpu.CompilerParams(dimension_semantics=("arbitrary",), vmem_limit_bytes=VMEM_LIMIT),
        name="gmlp_mixer",
    )(x, g, w_in, b_in, ln_g, ln_b, w_s, b_s, w_out, b_out)


def _rope_slices(t, cos, sin_signed, first_half):
    outs = []
    for j in range(t.shape[1] // LANES):
        s = t[:, j * LANES:(j + 1) * LANES]
        partner = jnp.where(first_half, pltpu.roll(s, LANES - HEAD_DIM // 2, 1), pltpu.roll(s, HEAD_DIM // 2, 1))
        outs.append(s * cos + partner * sin_signed)
    return jnp.concatenate(outs, axis=1)


def _dot_nt(a, b):
    return lax.dot_general(a, b, (((1,), (1,)), ((), ())), preferred_element_type=F32)


def _ffn_ple_qkv_kernel(x_ref, p_ref, cos_ref, sin_ref, cost_ref, sint_ref, fg_ref, wg_ref, wu_ref, wd_ref,
                        wproj_ref, pg_ref, wgate_ref, mg_ref, wqt_ref, bq_ref, kg_ref, wk_ref, bk_ref,
                        wvt_ref, bv_ref, x_out, qt_out, k_out, vt_out):
    x = x_ref[...]
    hn = _rms(x, fg_ref[...]).astype(BF16)
    a = (_silu(_dot(hn, wg_ref[...])) * _dot(hn, wu_ref[...])).astype(BF16)
    x = x + _dot(a, wd_ref[...])
    gate = _sigmoid(_dot(_rms(x, pg_ref[...]).astype(BF16), wgate_ref[...]))
    x = x + _dot(p_ref[...].astype(BF16), wproj_ref[...]) * gate
    x_out[...] = x

    qt = _dot_nt(wqt_ref[...], _rms(x, mg_ref[...]).astype(BF16)) + bq_ref[...]
    cos_t = cost_ref[...]
    sin_t = sint_ref[...]
    half = HEAD_DIM // 2
    scale = 1.0 / math.sqrt(HEAD_DIM)
    for h in range(N_HEADS):
        t1 = qt[h * HEAD_DIM:h * HEAD_DIM + half]
        t2 = qt[h * HEAD_DIM + half:(h + 1) * HEAD_DIM]
        qt_out[h * HEAD_DIM:h * HEAD_DIM + half, :] = ((t1 * cos_t - t2 * sin_t) * scale).astype(BF16)
        qt_out[h * HEAD_DIM + half:(h + 1) * HEAD_DIM, :] = ((t2 * cos_t + t1 * sin_t) * scale).astype(BF16)

    hkv = _rms(x, kg_ref[...]).astype(BF16)
    lane = lax.broadcasted_iota(jnp.int32, cos_ref.shape, 1)
    first_half = (lane % HEAD_DIM) < half
    k = _rope_slices(_dot(hkv, wk_ref[...]) + bk_ref[...], cos_ref[...], sin_ref[...], first_half)
    k_out[...] = k.astype(BF16)
    vt_out[...] = (_dot_nt(wvt_ref[...], hkv) + bv_ref[...]).astype(BF16)


def _ffn_ple_qkv(x, p0, rope_tabs, fg, wg, wu, wd, wproj, pg, wgate, mg, wqt, bq_col, kg, wk, bk, wvt, bv_col):
    t = FFN_TILE
    per_seq = SEQ // t
    half = HEAD_DIM // 2
    tok = lambda w: pl.BlockSpec((t, w), lambda i: (i, 0))
    feat = lambda w: pl.BlockSpec((w, t), lambda i: (0, i))
    rope = pl.BlockSpec((t, LANES), lambda i: (i % per_seq, 0))
    rope_t = pl.BlockSpec((half, t), lambda i: (0, i % per_seq))
    cos_t, sin_t, cos_tt, sin_tt = rope_tabs
    return pl.pallas_call(
        _ffn_ple_qkv_kernel,
        grid=(N_TOK // t,),
        in_specs=[tok(D_MODEL), _ple_input(t, 0), rope, rope, rope_t, rope_t,
                  _resident((1, D_MODEL)), _resident((D_MODEL, D_FF_DENSE)), _resident((D_MODEL, D_FF_DENSE)),
                  _resident((D_FF_DENSE, D_MODEL)),
                  _resident((PLE_DIM, D_MODEL)), _resident((1, D_MODEL)), _resident((D_MODEL, D_MODEL)),
                  _resident((1, D_MODEL)), _resident((D_MODEL, D_MODEL)), _resident((D_MODEL, 1)),
                  _resident((1, D_MODEL)), _resident((D_MODEL, KV_WIDTH)), _resident((1, KV_WIDTH)),
                  _resident((KV_WIDTH, D_MODEL)), _resident((KV_WIDTH, 1))],
        out_specs=[tok(D_MODEL), feat(D_MODEL), tok(KV_WIDTH), feat(KV_WIDTH)],
        out_shape=[jax.ShapeDtypeStruct((N_TOK, D_MODEL), F32), jax.ShapeDtypeStruct((D_MODEL, N_TOK), BF16),
                   jax.ShapeDtypeStruct((N_TOK, KV_WIDTH), BF16), jax.ShapeDtypeStruct((KV_WIDTH, N_TOK), BF16)],
        compiler_params=pltpu.CompilerParams(dimension_semantics=("arbitrary",), vmem_limit_bytes=VMEM_LIMIT),
        name="ffn_ple_qkv",
    )(x, p0, cos_t, sin_t, cos_tt, sin_tt, fg, wg, wu, wd, wproj, pg, wgate, mg, wqt, bq_col, kg, wk, bk,
      wvt, bv_col)


def _attn_kernel(sink_ref, x_ref, qt_ref, kp_ref, kc_ref, vtp_ref, vtc_ref, wo_ref, bo_ref, o_ref, attnt_ref):
    i = pl.program_id(0)
    seq_start = (i % (SEQ // ATTN_TILE)) == 0
    kk = jnp.concatenate([kp_ref[...], kc_ref[...]], axis=0)
    vvt = jnp.concatenate([vtp_ref[...], vtc_ref[...]], axis=1)

    width = KV_REP * WINDOW
    kj = lax.broadcasted_iota(jnp.int32, (2 * WINDOW, width), 0)
    qi = lax.broadcasted_iota(jnp.int32, (2 * WINDOW, width), 1) % WINDOW
    band = (kj > qi) & (kj <= qi + WINDOW)
    first_band = band & ((kj >= WINDOW) | jnp.logical_not(seq_start))
    rep = lax.broadcasted_iota(jnp.int32, (1, width), 1) // WINDOW

    for kh in range(N_KV_HEADS):
        feats = slice(kh * HEAD_DIM, (kh + 1) * HEAD_DIM)
        sink = jnp.zeros((1, width), F32)
        for r in range(KV_REP):
            sink = jnp.where(rep == r, sink_ref[kh * KV_REP + r], sink)
        for sb in range(ATTN_TILE // WINDOW):
            toks = slice(sb * WINDOW, (sb + 1) * WINDOW)
            keys = slice(sb * WINDOW, sb * WINDOW + 2 * WINDOW)
            qt = jnp.concatenate(
                [qt_ref[(kh * KV_REP + r) * HEAD_DIM:(kh * KV_REP + r + 1) * HEAD_DIM, toks] for r in range(KV_REP)],
                axis=1)
            s = _dot(kk[keys, feats], qt)
            s = jnp.where(first_band if sb == 0 else band, s, MASK_VALUE)
            m = jnp.maximum(jnp.max(s, axis=0, keepdims=True), sink)
            pr = jnp.exp(s - m)
            den = jnp.sum(pr, axis=0, keepdims=True) + jnp.exp(sink - m)
            ot = _dot(vvt[feats, keys], pr.astype(BF16)) * (1.0 / den)
            for r in range(KV_REP):
                h = kh * KV_REP + r
                attnt_ref[h * HEAD_DIM:(h + 1) * HEAD_DIM, toks] = ot[:, r * WINDOW:(r + 1) * WINDOW].astype(BF16)
    attn_out = lax.dot_general(attnt_ref[...], wo_ref[...], (((0,), (0,)), ((), ())), preferred_element_type=F32)
    o_ref[...] = x_ref[...] + attn_out + bo_ref[...]


def _swa_attention(sinks, x, qt, k, vt, wo, bo):
    t = ATTN_TILE
    blocks_per_tile = t // WINDOW
    tok = lambda w: pl.BlockSpec((t, w), lambda i: (i, 0))
    feat = lambda w: pl.BlockSpec((w, t), lambda i: (0, i))
    prev_block = lambda i: jnp.maximum(i * blocks_per_tile - 1, 0)
    k_prev = pl.BlockSpec((WINDOW, KV_WIDTH), lambda i: (prev_block(i), 0))
    vt_prev = pl.BlockSpec((KV_WIDTH, WINDOW), lambda i: (0, prev_block(i)))
    return pl.pallas_call(
        _attn_kernel,
        grid=(N_TOK // t,),
        in_specs=[pl.BlockSpec(memory_space=pltpu.SMEM), tok(D_MODEL), feat(D_MODEL), k_prev, tok(KV_WIDTH),
                  vt_prev, feat(KV_WIDTH), _resident((D_MODEL, D_MODEL)), _resident((1, D_MODEL))],
        out_specs=tok(D_MODEL),
        out_shape=jax.ShapeDtypeStruct((N_TOK, D_MODEL), F32),
        scratch_shapes=[pltpu.VMEM((D_MODEL, t), BF16)],
        compiler_params=pltpu.CompilerParams(dimension_semantics=("arbitrary",), vmem_limit_bytes=VMEM_LIMIT),
        name="swa_attention",
    )(sinks, x, qt, k, k, vt, vt, wo, bo)


def _router_kernel(x_ref, g_ref, wr_ref, hn_ref, idx_ref, wts_ref, rank_ref, cnt_ref):
    t = ROUTER_TILE

    @pl.when(pl.program_id(0) == 0)
    def _():
        cnt_ref[...] = jnp.zeros_like(cnt_ref)

    hn = _rms(x_ref[...], g_ref[...])
    _store_token_tiles(hn_ref, hn)
    hn_hi = hn.astype(BF16)
    hn_lo = (hn - hn_hi.astype(F32)).astype(BF16)
    wr = wr_ref[...]
    wr_hi = wr.astype(BF16)
    wr_lo = (wr - wr_hi.astype(F32)).astype(BF16)
    logits = _dot(hn_hi, wr_hi) + (_dot(hn_hi, wr_lo) + _dot(hn_lo, wr_hi))
    lane = lax.broadcasted_iota(jnp.int32, (t, LANES), 1)
    logits = jnp.where(lane < N_EXPERTS, logits, -jnp.inf)

    m1 = jnp.max(logits, axis=-1, keepdims=True)
    i1 = jnp.min(jnp.where(logits == m1, lane, LANES), axis=-1, keepdims=True)
    rest = jnp.where(lane == i1, -jnp.inf, logits)
    m2 = jnp.max(rest, axis=-1, keepdims=True)
    i2 = jnp.min(jnp.where(rest == m2, lane, LANES), axis=-1, keepdims=True)
    e2 = jnp.exp(m2 - m1)
    w1 = 1.0 / (1.0 + e2)
    w2 = e2 / (1.0 + e2)

    chosen = (lane == i1) | (lane == i2)
    onehot = jnp.where(chosen, 1.0, 0.0).astype(BF16)
    r = lax.broadcasted_iota(jnp.int32, (t, t), 0)
    c = lax.broadcasted_iota(jnp.int32, (t, t), 1)
    before = jnp.where(c < r, 1.0, 0.0).astype(BF16)
    seen = _dot(before, onehot) + cnt_ref[...]
    rank1 = jnp.sum(jnp.where(lane == i1, seen, 0.0), axis=-1, keepdims=True)
    rank2 = jnp.sum(jnp.where(lane == i2, seen, 0.0), axis=-1, keepdims=True)
    cnt_ref[...] = cnt_ref[...] + jnp.sum(onehot.astype(F32), axis=0, keepdims=True)

    idx_ref[...] = jnp.concatenate([i1, i2], axis=1)
    wts_ref[...] = jnp.concatenate([w1, w2], axis=1)
    rank_ref[...] = jnp.concatenate([rank1, rank2], axis=1).astype(jnp.int32)


def _moe_router(x, g, w_router_padded):
    t = ROUTER_TILE
    tok = lambda w: pl.BlockSpec((t, w), lambda i: (i, 0))
    return pl.pallas_call(
        _router_kernel,
        grid=(N_TOK // t,),
        in_specs=[tok(D_MODEL), _resident((1, D_MODEL)), _resident((D_MODEL, LANES))],
        out_specs=[pl.BlockSpec((t * SUBLANES, LANES), lambda i: (i, 0)), tok(2), tok(2), tok(2),
                   pl.BlockSpec((1, LANES), lambda i: (0, 0))],
        out_shape=[jax.ShapeDtypeStruct((N_TOK * SUBLANES, LANES), F32), jax.ShapeDtypeStruct((N_TOK, 2), jnp.int32),
                   jax.ShapeDtypeStruct((N_TOK, 2), F32), jax.ShapeDtypeStruct((N_TOK, 2), jnp.int32),
                   jax.ShapeDtypeStruct((1, LANES), F32)],
        compiler_params=pltpu.CompilerParams(dimension_semantics=("arbitrary",), vmem_limit_bytes=VMEM_LIMIT),
        name="moe_router",
    )(x, g, w_router_padded)


def _dispatch_kernel(pad_start_ref, pad_len_ref, nv_ref, pos_ref, hn_ref, xs_ref, zero_ref, sem):
    def row_copy(t, k):
        return pltpu.make_async_copy(_token(hn_ref, t), _token(xs_ref, pos_ref[2 * t + k]), sem)

    def issue(t, carry):
        row_copy(t, 0).start(priority=0)
        row_copy(t, 1).start(priority=1)
        return carry

    def drain(t, carry):
        row_copy(t, 0).wait()
        row_copy(t, 1).wait()
        return carry

    lax.fori_loop(0, DISPATCH_TILE, issue, 0, unroll=ROW_COPY_UNROLL)
    lax.fori_loop(0, DISPATCH_TILE, drain, 0, unroll=ROW_COPY_UNROLL)

    @pl.when(pl.program_id(0) == pl.num_programs(0) - 1)
    def _():
        zero_ref[...] = jnp.zeros_like(zero_ref)
        for e in range(N_EXPERTS):
            def zero_row(r, e=e):
                return pltpu.make_async_copy(_token(zero_ref, 0), _token(xs_ref, pad_start_ref[e] + r), sem)

            def row_issue(r, carry, zero_row=zero_row):
                zero_row(r).start()
                return carry

            def row_drain(r, carry, zero_row=zero_row):
                zero_row(r).wait()
                return carry

            lax.fori_loop(0, pad_len_ref[e], row_issue, 0)
            lax.fori_loop(0, pad_len_ref[e], row_drain, 0)

        def zero_tile(i):
            return pltpu.make_async_copy(zero_ref, _tokens(xs_ref, i * EXPERT_TILE, EXPERT_TILE), sem)

        def tile_issue(i, carry):
            zero_tile(i).start()
            return carry

        def tile_drain(i, carry):
            zero_tile(i).wait()
            return carry

        lax.fori_loop(nv_ref[0], N_ROW_TILES, tile_issue, 0)
        lax.fori_loop(nv_ref[0], N_ROW_TILES, tile_drain, 0)


def _moe_dispatch(pad_start, pad_len, n_valid, pos_flat, hn):
    t = DISPATCH_TILE
    grid_spec = pltpu.PrefetchScalarGridSpec(
        num_scalar_prefetch=3,
        grid=(N_TOK // t,),
        in_specs=[pl.BlockSpec((2 * t,), lambda i, ps, pn, nv: (i,), memory_space=pltpu.SMEM),
                  pl.BlockSpec((t * SUBLANES, LANES), lambda i, ps, pn, nv: (i, 0))],
        out_specs=pl.BlockSpec(memory_space=pl.ANY),
        scratch_shapes=[pltpu.VMEM((EXPERT_TILE * SUBLANES, LANES), F32), pltpu.SemaphoreType.DMA(())],
    )
    return pl.pallas_call(
        _dispatch_kernel,
        grid_spec=grid_spec,
        out_shape=jax.ShapeDtypeStruct((SORTED_ROWS * SUBLANES, LANES), F32),
        compiler_params=pltpu.CompilerParams(dimension_semantics=("arbitrary",), has_side_effects=True,
                                             vmem_limit_bytes=VMEM_LIMIT),
        name="moe_dispatch",
    )(pad_start, pad_len, n_valid, pos_flat, hn)


def _experts_kernel(te_ref, nr_ref, nv_ref, xs_ref, wg_ref, wu_ref, wd_ref, ys_ref, xb_ref, acc_ref):
    del te_ref
    i = pl.program_id(0)
    j = pl.program_id(1)
    last = pl.num_programs(1) - 1
    sub = EXPERT_SUB
    n_rows = nr_ref[i]

    @pl.when(i < nv_ref[0])
    def _():
        for sb in range(EXPERT_TILE // sub):
            rows = slice(sb * sub, (sb + 1) * sub)
            tiles = lambda s, sb=sb: pl.ds(sb * sub * SUBLANES + s, sub, stride=SUBLANES)

            @pl.when(sb * sub < n_rows)
            def _(rows=rows, tiles=tiles):
                @pl.when(j == 0)
                def _():
                    for s in range(SUBLANES):
                        xb_ref[rows, s * LANES:(s + 1) * LANES] = xs_ref[tiles(s), :].astype(BF16)

                xb = xb_ref[rows, :]
                a = (_silu(_dot(xb, wg_ref[...])) * _dot(xb, wu_ref[...])).astype(BF16)
                y = _dot(a, wd_ref[...])

                @pl.when(j == 0)
                def _():
                    acc_ref[rows, :] = y

                @pl.when((j > 0) & (j < last))
                def _():
                    acc_ref[rows, :] += y

                @pl.when(j == last)
                def _():
                    total = acc_ref[rows, :] + y
                    for s in range(SUBLANES):
                        ys_ref[tiles(s), :] = total[:, s * LANES:(s + 1) * LANES]

            @pl.when((sb * sub >= n_rows) & (j == last))
            def _(sb=sb):
                ys_ref[sb * sub * SUBLANES:(sb + 1) * sub * SUBLANES, :] = jnp.zeros((sub * SUBLANES, LANES), F32)

    @pl.when((i >= nv_ref[0]) & (j == last))
    def _():
        ys_ref[...] = jnp.zeros_like(ys_ref)


def _moe_experts(tile_expert, tile_rows, n_valid, xs, wg, wu, wd):
    tm, fb = EXPERT_TILE, EXPERT_FF_BLOCK
    n_fb = D_FF_EXPERT // fb
    assert n_fb >= 2
    rows = lambda i, j, te, nr, nv: jnp.maximum(jnp.minimum(i, nv[0] - 1), 0)
    ff = lambda i, j, te, nr, nv: jnp.where(i < nv[0], j, n_fb - 1)
    grid_spec = pltpu.PrefetchScalarGridSpec(
        num_scalar_prefetch=3,
        grid=(N_ROW_TILES, n_fb),
        in_specs=[pl.BlockSpec((tm * SUBLANES, LANES), lambda i, j, te, nr, nv: (rows(i, j, te, nr, nv), 0)),
                  pl.BlockSpec((None, D_MODEL, fb), lambda i, j, te, nr, nv: (te[i], 0, ff(i, j, te, nr, nv))),
                  pl.BlockSpec((None, D_MODEL, fb), lambda i, j, te, nr, nv: (te[i], 0, ff(i, j, te, nr, nv))),
                  pl.BlockSpec((None, fb, D_MODEL), lambda i, j, te, nr, nv: (te[i], ff(i, j, te, nr, nv), 0))],
        out_specs=pl.BlockSpec((tm * SUBLANES, LANES), lambda i, j, te, nr, nv: (i, 0)),
        scratch_shapes=[pltpu.VMEM((tm, D_MODEL), BF16), pltpu.VMEM((tm, D_MODEL), F32)],
    )
    return pl.pallas_call(
        _experts_kernel,
        grid_spec=grid_spec,
        out_shape=jax.ShapeDtypeStruct((SORTED_ROWS * SUBLANES, LANES), F32),
        compiler_params=pltpu.CompilerParams(dimension_semantics=("arbitrary", "arbitrary"),
                                             vmem_limit_bytes=VMEM_LIMIT),
        name="moe_experts",
    )(tile_expert, tile_rows, n_valid, xs, wg, wu, wd)


def _combine_kernel(pos_ref, pos_next_ref, x_ref, wts_ref, p_ref, ys_ref, wproj_ref, pg_ref, wgate_ref, fg_ref,
                    o_ref, rows_ref, sems):
    t = COMBINE_TILE
    i = pl.program_id(0)
    slot = i % 2

    def row_copy(pref, slot, r, k):
        return pltpu.make_async_copy(_token(ys_ref, pref[2 * r + k]), _token(rows_ref.at[slot, k], r),
                                     sems.at[slot])

    def issue(pref, slot):
        def body(r, carry):
            row_copy(pref, slot, r, 0).start(priority=0)
            row_copy(pref, slot, r, 1).start(priority=1)
            return carry
        lax.fori_loop(0, t, body, 0, unroll=ROW_COPY_UNROLL)

    def drain(pref, slot):
        def body(r, carry):
            row_copy(pref, slot, r, 0).wait()
            row_copy(pref, slot, r, 1).wait()
            return carry
        lax.fori_loop(0, t, body, 0, unroll=ROW_COPY_UNROLL)

    @pl.when(i == 0)
    def _():
        issue(pos_ref, 0)

    @pl.when(i + 1 < pl.num_programs(0))
    def _():
        issue(pos_next_ref, 1 - slot)

    drain(pos_ref, slot)

    w = wts_ref[...]
    x = (x_ref[...] + w[:, 0:1] * _load_token_tiles(rows_ref.at[slot, 0], t)
         + w[:, 1:2] * _load_token_tiles(rows_ref.at[slot, 1], t))
    gate = _sigmoid(_dot(_rms(x, pg_ref[...]).astype(BF16), wgate_ref[...]))
    x = x + _dot(p_ref[...].astype(BF16), wproj_ref[...]) * gate
    o_ref[...] = _rms(x, fg_ref[...])


def _moe_combine_out(pos_flat, x, wts, p1, ys, wproj, pg, wgate, fg):
    t = COMBINE_TILE
    n = N_TOK // t
    tok = lambda w: pl.BlockSpec((t, w), lambda i: (i, 0))
    return pl.pallas_call(
        _combine_kernel,
        grid=(n,),
        in_specs=[pl.BlockSpec((2 * t,), lambda i: (i,), memory_space=pltpu.SMEM),
                  pl.BlockSpec((2 * t,), lambda i: (jnp.minimum(i + 1, n - 1),), memory_space=pltpu.SMEM),
                  tok(D_MODEL), tok(2), _ple_input(t, 1), pl.BlockSpec(memory_space=pl.ANY),
                  _resident((PLE_DIM, D_MODEL)), _resident((1, D_MODEL)), _resident((D_MODEL, D_MODEL)),
                  _resident((1, D_MODEL))],
        out_specs=tok(D_MODEL),
        out_shape=jax.ShapeDtypeStruct((N_TOK, D_MODEL), F32),
        scratch_shapes=[pltpu.VMEM((2, 2, t * SUBLANES, LANES), F32), pltpu.SemaphoreType.DMA((2,))],
        compiler_params=pltpu.CompilerParams(dimension_semantics=("arbitrary",), vmem_limit_bytes=VMEM_LIMIT),
        name="moe_combine_out",
    )(pos_flat, pos_flat, x, wts, p1, ys, wproj, pg, wgate, fg)


def _rope_tables():
    half = HEAD_DIM // 2
    freqs = ROPE_THETA ** (-jnp.arange(0, HEAD_DIM, 2, dtype=F32) / HEAD_DIM)
    ang = jnp.arange(SEQ, dtype=F32)[:, None] * freqs[None, :]
    cos, sin = jnp.cos(ang), jnp.sin(ang)
    reps = LANES // HEAD_DIM
    cos_t = jnp.tile(jnp.concatenate([cos, cos], axis=1), (1, reps))
    sin_t = jnp.tile(jnp.concatenate([-sin, sin], axis=1), (1, reps))
    del half
    return cos_t, sin_t, cos.T, sin.T


def _routing_plan(idx, rank, counts):
    tm = EXPERT_TILE
    tiles = (counts + tm - 1) // tm
    tile_end = jnp.cumsum(tiles)
    group_start = (tile_end - tiles) * tm
    pos = (group_start[idx] + rank).reshape(-1).astype(jnp.int32)
    n_valid = tile_end[-1]
    tile_ids = jnp.minimum(jnp.arange(N_ROW_TILES, dtype=jnp.int32), n_valid - 1)
    tile_expert = jnp.sum(tile_ids[:, None] >= tile_end[None, :], axis=1).astype(jnp.int32)
    tile_first = (tile_end - tiles)[tile_expert]
    tile_rows = jnp.clip(counts[tile_expert] - (tile_ids - tile_first) * tm, 0, tm).astype(jnp.int32)
    pad_start = (group_start + counts).astype(jnp.int32)
    pad_len = (tiles * tm - counts).astype(jnp.int32)
    return pos, tile_expert, tile_rows, n_valid.reshape(1).astype(jnp.int32), pad_start, pad_len


def kernel(x, p, mix_norm_g, ffn_norm_g, gmlp_w_in, gmlp_b_in, gmlp_ln_g, gmlp_ln_b, gmlp_w_s, gmlp_b_s, gmlp_w_out, gmlp_b_out, kv_norm_g, w_kv, b_kv, attn_w_q, attn_b_q, attn_sinks, attn_w_o, attn_b_o, ffn_w_gate, ffn_w_up, ffn_w_down, moe_w_router, moe_w_gate, moe_w_up, moe_w_down, ple_w_proj, ple_norm_g, ple_w_gate, final_norm_g):
    row = lambda a: a.reshape(1, -1)
    bf = lambda a: a.astype(BF16)
    xf = x.reshape(N_TOK, D_MODEL)
    pf = p.reshape(2, N_TOK, PLE_DIM)

    x1 = _gmlp_mixer(xf, row(mix_norm_g[0]), bf(gmlp_w_in[0]), row(gmlp_b_in[0]), row(gmlp_ln_g[0]),
                     row(gmlp_ln_b[0]), gmlp_w_s[0], gmlp_b_s[0][:, :, None], bf(gmlp_w_out[0]),
                     row(gmlp_b_out[0]))

    col = lambda a: a.reshape(-1, 1)
    x3, qt, k, vt = _ffn_ple_qkv(x1, pf, _rope_tables(), row(ffn_norm_g[0]), bf(ffn_w_gate[0]),
                                 bf(ffn_w_up[0]), bf(ffn_w_down[0]), bf(ple_w_proj[0]), row(ple_norm_g[0]),
                                 bf(ple_w_gate[0]), row(mix_norm_g[1]), bf(attn_w_q[0].T), col(attn_b_q[0]),
                                 row(kv_norm_g), bf(w_kv[:, :KV_WIDTH]), row(b_kv[:KV_WIDTH]),
                                 bf(w_kv[:, KV_WIDTH:].T), col(b_kv[KV_WIDTH:]))

    x4 = _swa_attention(attn_sinks[0], x3, qt, k, vt, bf(attn_w_o[0]), row(attn_b_o[0]))

    w_router = jnp.pad(moe_w_router[0], ((0, 0), (0, LANES - N_EXPERTS)))
    hn, idx, wts, rank, counts = _moe_router(x4, row(ffn_norm_g[1]), w_router)
    pos, tile_expert, tile_rows, n_valid, pad_start, pad_len = _routing_plan(
        idx, rank, counts[0, :N_EXPERTS].astype(jnp.int32))

    xs = _moe_dispatch(pad_start, pad_len, n_valid, pos, hn)
    ys = _moe_experts(tile_expert, tile_rows, n_valid, xs, bf(moe_w_gate[0]), bf(moe_w_up[0]),
                      bf(moe_w_down[0]))
    out = _moe_combine_out(pos, x4, wts, pf, ys, bf(ple_w_proj[1]), row(ple_norm_g[1]), bf(ple_w_gate[1]),
                           row(final_norm_g))
    return out.reshape(BATCH, SEQ, D_MODEL)
```

```python
import functools
import math

import jax
import jax.numpy as jnp
from jax import lax
from jax.experimental import pallas as pl
from jax.experimental.pallas import tpu as pltpu

F32 = jnp.float32
BF16 = jnp.bfloat16

D_MODEL = 1024
BATCH = 4
SEQ = 4096
N_TOK = BATCH * SEQ

CHUNK = 128
GMLP_FFN = 6 * D_MODEL
GMLP_HALF = GMLP_FFN // 2
GMLP_GROUPS = 8
GMLP_GROUP_DIM = GMLP_HALF // GMLP_GROUPS

N_HEADS = 16
N_KV_HEADS = 4
HEAD_DIM = 64
KV_REP = N_HEADS // N_KV_HEADS
KV_WIDTH = N_KV_HEADS * HEAD_DIM
WINDOW = 128
ROPE_THETA = 10000.0

D_FF_DENSE = 2816
N_EXPERTS = 8
D_FF_EXPERT = 3584
PLE_DIM = 256

EPS = 1e-6
MASK_VALUE = -1e30

LANES = 128
SUBLANES = 8
assert D_MODEL == SUBLANES * LANES

GMLP_TILE = 256
FFN_TILE = 256
ATTN_TILE = 512
ROUTER_TILE = 512
DISPATCH_TILE = 1024
ROW_COPY_UNROLL = 8
EXPERT_TILE = 1024
EXPERT_SUB = 512
EXPERT_FF_BLOCK = 896
COMBINE_TILE = 256

SORTED_ROWS = 2 * N_TOK + N_EXPERTS * EXPERT_TILE
N_ROW_TILES = SORTED_ROWS // EXPERT_TILE

VMEM_LIMIT = 56 * 1024 * 1024


def _ple_input(t, layer):
    return pl.BlockSpec((None, t, PLE_DIM), lambda i: (layer, i, 0))


def _resident(shape):
    zeros = (0,) * len(shape)
    return pl.BlockSpec(shape, lambda *_: zeros, pipeline_mode=pl.Buffered(1))


def _rms(x, g):
    return x * lax.rsqrt(jnp.mean(x * x, axis=-1, keepdims=True) + EPS) * g


def _gelu_tanh(x):
    c = math.sqrt(2.0 / math.pi)
    return 0.5 * x * (1.0 + jnp.tanh(c * (x + 0.044715 * (x * x * x))))


def _silu(x):
    return x * (1.0 / (1.0 + jnp.exp(-x)))


def _sigmoid(x):
    return 1.0 / (1.0 + jnp.exp(-x))


def _dot(a, b):
    return jnp.dot(a, b, preferred_element_type=F32)


def _store_token_tiles(ref, x):
    t = x.shape[0]
    for s in range(SUBLANES):
        ref[pl.ds(s, t, stride=SUBLANES), :] = x[:, s * LANES:(s + 1) * LANES]


def _tokens(ref, first, n):
    return ref.at[pl.ds(first * SUBLANES, n * SUBLANES)]


def _token(ref, r):
    return _tokens(ref, r, 1)


def _load_token_tiles(ref, t):
    return jnp.concatenate([ref[pl.ds(s, t, stride=SUBLANES), :] for s in range(SUBLANES)], axis=1)


def _gmlp_kernel(x_ref, g_ref, win_ref, bin_ref, lng_ref, lnb_ref, ws_ref, bs_ref,
                 wout_ref, bout_ref, moe_g_ref, moe_u_ref, o_ref, moe_gb_ref, moe_ub_ref):
    moe_gb_ref[...] = moe_g_ref[...].astype(BF16)
    moe_ub_ref[...] = moe_u_ref[...].astype(BF16)

    x = x_ref[...]
    h = _rms(x, g_ref[...]).astype(BF16)
    v = _gelu_tanh(_dot(h, win_ref[:, GMLP_HALF:]) + bin_ref[:, GMLP_HALF:])
    mu = jnp.mean(v, axis=-1, keepdims=True)
    vc = v - mu
    var = jnp.mean(vc * vc, axis=-1, keepdims=True)
    vn = (vc * lax.rsqrt(var + EPS) * lng_ref[...] + lnb_ref[...]).astype(BF16)

    row = lax.broadcasted_iota(jnp.int32, (CHUNK, CHUNK), 0)
    col = lax.broadcasted_iota(jnp.int32, (CHUNK, CHUNK), 1)
    causal = col <= row

    pair_w = 2 * GMLP_GROUP_DIM
    acc = x + bout_ref[...]
    for pair in range(GMLP_GROUPS // 2):
        c0 = pair * pair_w
        u = _gelu_tanh(_dot(h, win_ref[:, c0:c0 + pair_w]) + bin_ref[:, c0:c0 + pair_w])
        parts = []
        for gi in range(2 * pair, 2 * pair + 2):
            ws = jnp.where(causal, ws_ref[gi], 0.0).astype(BF16)
            bs = bs_ref[gi]
            vg = vn[:, gi * GMLP_GROUP_DIM:(gi + 1) * GMLP_GROUP_DIM]
            rows = [_dot(ws, vg[c * CHUNK:(c + 1) * CHUNK]) + bs for c in range(GMLP_TILE // CHUNK)]
            parts.append(jnp.concatenate(rows, axis=0))
        mixed = jnp.concatenate(parts, axis=1)
        gated = (u * mixed).astype(BF16)
        acc = acc + _dot(gated, wout_ref[c0:c0 + pair_w, :])
    o_ref[...] = acc


def _cast_slab(rows, width, steps):
    assert rows % (steps * 16) == 0
    return pl.BlockSpec((rows // steps, width), lambda i: (i, 0))


def _gmlp_mixer(x, g, w_in, b_in, ln_g, ln_b, w_s, b_s, w_out, b_out, moe_w_gate, moe_w_up):
    t = GMLP_TILE
    steps = N_TOK // t
    tok = pl.BlockSpec((t, D_MODEL), lambda i: (i, 0))
    moe_rows = N_EXPERTS * D_MODEL
    slab = _cast_slab(moe_rows, D_FF_EXPERT, steps)
    moe_bf16 = jax.ShapeDtypeStruct((moe_rows, D_FF_EXPERT), BF16)
    return pl.pallas_call(
        _gmlp_kernel,
        grid=(steps,),
        in_specs=[tok, _resident((1, D_MODEL)), _resident((D_MODEL, GMLP_FFN)), _resident((1, GMLP_FFN)),
                  _resident((1, GMLP_HALF)), _resident((1, GMLP_HALF)),
                  _resident((GMLP_GROUPS, CHUNK, CHUNK)), _resident((GMLP_GROUPS, CHUNK, 1)),
                  _resident((GMLP_HALF, D_MODEL)), _resident((1, D_MODEL)), slab, slab],
        out_specs=[tok, slab, slab],
        out_shape=[jax.ShapeDtypeStruct((N_TOK, D_MODEL), F32), moe_bf16, moe_bf16],
        compiler_params=pltpu.CompilerParams(dimension_semantics=("arbitrary",), vmem_limit_bytes=VMEM_LIMIT),
        name="gmlp_mixer",
    )(x, g, w_in, b_in, ln_g, ln_b, w_s, b_s, w_out, b_out,
      moe_w_gate.reshape(moe_rows, D_FF_EXPERT), moe_w_up.reshape(moe_rows, D_FF_EXPERT))


def _rope_slices(t, cos, sin_signed, first_half):
    outs = []
    for j in range(t.shape[1] // LANES):
        s = t[:, j * LANES:(j + 1) * LANES]
        partner = jnp.where(first_half, pltpu.roll(s, LANES - HEAD_DIM // 2, 1), pltpu.roll(s, HEAD_DIM // 2, 1))
        outs.append(s * cos + partner * sin_signed)
    return jnp.concatenate(outs, axis=1)


def _dot_nt(a, b):
    return lax.dot_general(a, b, (((1,), (1,)), ((), ())), preferred_element_type=F32)


def _ffn_ple_qkv_kernel(x_ref, p_ref, cos_ref, sin_ref, cost_ref, sint_ref, fg_ref, wg_ref, wu_ref, wd_ref,
                        wproj_ref, pg_ref, wgate_ref, mg_ref, wqt_ref, bq_ref, kg_ref, wk_ref, bk_ref,
                        wvt_ref, bv_ref, moe_d_ref, x_out, qt_out, k_out, vt_out, moe_db_ref):
    moe_db_ref[...] = moe_d_ref[...].astype(BF16)

    x = x_ref[...]
    hn = _rms(x, fg_ref[...]).astype(BF16)
    a = (_silu(_dot(hn, wg_ref[...])) * _dot(hn, wu_ref[...])).astype(BF16)
    x = x + _dot(a, wd_ref[...])
    gate = _sigmoid(_dot(_rms(x, pg_ref[...]).astype(BF16), wgate_ref[...]))
    x = x + _dot(p_ref[...].astype(BF16), wproj_ref[...]) * gate
    x_out[...] = x

    qt = _dot_nt(wqt_ref[...], _rms(x, mg_ref[...]).astype(BF16)) + bq_ref[...]
    cos_t = cost_ref[...]
    sin_t = sint_ref[...]
    half = HEAD_DIM // 2
    scale = 1.0 / math.sqrt(HEAD_DIM)
    for h in range(N_HEADS):
        t1 = qt[h * HEAD_DIM:h * HEAD_DIM + half]
        t2 = qt[h * HEAD_DIM + half:(h + 1) * HEAD_DIM]
        qt_out[h * HEAD_DIM:h * HEAD_DIM + half, :] = ((t1 * cos_t - t2 * sin_t) * scale).astype(BF16)
        qt_out[h * HEAD_DIM + half:(h + 1) * HEAD_DIM, :] = ((t2 * cos_t + t1 * sin_t) * scale).astype(BF16)

    hkv = _rms(x, kg_ref[...]).astype(BF16)
    lane = lax.broadcasted_iota(jnp.int32, cos_ref.shape, 1)
    first_half = (lane % HEAD_DIM) < half
    k = _rope_slices(_dot(hkv, wk_ref[...]) + bk_ref[...], cos_ref[...], sin_ref[...], first_half)
    k_out[...] = k.astype(BF16)
    vt_out[...] = (_dot_nt(wvt_ref[...], hkv) + bv_ref[...]).astype(BF16)


def _ffn_ple_qkv(x, p0, rope_tabs, fg, wg, wu, wd, wproj, pg, wgate, mg, wqt, bq_col, kg, wk, bk, wvt, bv_col,
                 moe_w_down):
    t = FFN_TILE
    moe_rows = N_EXPERTS * D_FF_EXPERT
    slab = _cast_slab(moe_rows, D_MODEL, N_TOK // t)
    per_seq = SEQ // t
    half = HEAD_DIM // 2
    tok = lambda w: pl.BlockSpec((t, w), lambda i: (i, 0))
    feat = lambda w: pl.BlockSpec((w, t), lambda i: (0, i))
    rope = pl.BlockSpec((t, LANES), lambda i: (i % per_seq, 0))
    rope_t = pl.BlockSpec((half, t), lambda i: (0, i % per_seq))
    cos_t, sin_t, cos_tt, sin_tt = rope_tabs
    return pl.pallas_call(
        _ffn_ple_qkv_kernel,
        grid=(N_TOK // t,),
        in_specs=[tok(D_MODEL), _ple_input(t, 0), rope, rope, rope_t, rope_t,
                  _resident((1, D_MODEL)), _resident((D_MODEL, D_FF_DENSE)), _resident((D_MODEL, D_FF_DENSE)),
                  _resident((D_FF_DENSE, D_MODEL)),
                  _resident((PLE_DIM, D_MODEL)), _resident((1, D_MODEL)), _resident((D_MODEL, D_MODEL)),
                  _resident((1, D_MODEL)), _resident((D_MODEL, D_MODEL)), _resident((D_MODEL, 1)),
                  _resident((1, D_MODEL)), _resident((D_MODEL, KV_WIDTH)), _resident((1, KV_WIDTH)),
                  _resident((KV_WIDTH, D_MODEL)), _resident((KV_WIDTH, 1)), slab],
        out_specs=[tok(D_MODEL), feat(D_MODEL), tok(KV_WIDTH), feat(KV_WIDTH), slab],
        out_shape=[jax.ShapeDtypeStruct((N_TOK, D_MODEL), F32), jax.ShapeDtypeStruct((D_MODEL, N_TOK), BF16),
                   jax.ShapeDtypeStruct((N_TOK, KV_WIDTH), BF16), jax.ShapeDtypeStruct((KV_WIDTH, N_TOK), BF16),
                   jax.ShapeDtypeStruct((moe_rows, D_MODEL), BF16)],
        compiler_params=pltpu.CompilerParams(dimension_semantics=("arbitrary",), vmem_limit_bytes=VMEM_LIMIT),
        name="ffn_ple_qkv",
    )(x, p0, cos_t, sin_t, cos_tt, sin_tt, fg, wg, wu, wd, wproj, pg, wgate, mg, wqt, bq_col, kg, wk, bk,
      wvt, bv_col, moe_w_down.reshape(moe_rows, D_MODEL))


def _attn_kernel(sink_ref, x_ref, qt_ref, kp_ref, kc_ref, vtp_ref, vtc_ref, wo_ref, bo_ref, o_ref, attnt_ref):
    i = pl.program_id(0)
    seq_start = (i % (SEQ // ATTN_TILE)) == 0
    kk = jnp.concatenate([kp_ref[...], kc_ref[...]], axis=0)
    vvt = jnp.concatenate([vtp_ref[...], vtc_ref[...]], axis=1)

    width = KV_REP * WINDOW
    kj = lax.broadcasted_iota(jnp.int32, (2 * WINDOW, width), 0)
    qi = lax.broadcasted_iota(jnp.int32, (2 * WINDOW, width), 1) % WINDOW
    band = (kj > qi) & (kj <= qi + WINDOW)
    first_band = band & ((kj >= WINDOW) | jnp.logical_not(seq_start))
    rep = lax.broadcasted_iota(jnp.int32, (1, width), 1) // WINDOW

    for kh in range(N_KV_HEADS):
        feats = slice(kh * HEAD_DIM, (kh + 1) * HEAD_DIM)
        sink = jnp.zeros((1, width), F32)
        for r in range(KV_REP):
            sink = jnp.where(rep == r, sink_ref[kh * KV_REP + r], sink)
        for sb in range(ATTN_TILE // WINDOW):
            toks = slice(sb * WINDOW, (sb + 1) * WINDOW)
            keys = slice(sb * WINDOW, sb * WINDOW + 2 * WINDOW)
            qt = jnp.concatenate(
                [qt_ref[(kh * KV_REP + r) * HEAD_DIM:(kh * KV_REP + r + 1) * HEAD_DIM, toks] for r in range(KV_REP)],
                axis=1)
            s = _dot(kk[keys, feats], qt)
            s = jnp.where(first_band if sb == 0 else band, s, MASK_VALUE)
            m = jnp.maximum(jnp.max(s, axis=0, keepdims=True), sink)
            pr = jnp.exp(s - m)
            den = jnp.sum(pr, axis=0, keepdims=True) + jnp.exp(sink - m)
            ot = _dot(vvt[feats, keys], pr.astype(BF16)) * (1.0 / den)
            for r in range(KV_REP):
                h = kh * KV_REP + r
                attnt_ref[h * HEAD_DIM:(h + 1) * HEAD_DIM, toks] = ot[:, r * WINDOW:(r + 1) * WINDOW].astype(BF16)
    attn_out = lax.dot_general(attnt_ref[...], wo_ref[...], (((0,), (0,)), ((), ())), preferred_element_type=F32)
    o_ref[...] = x_ref[...] + attn_out + bo_ref[...]


def _swa_attention(sinks, x, qt, k, vt, wo, bo):
    t = ATTN_TILE
    blocks_per_tile = t // WINDOW
    tok = lambda w: pl.BlockSpec((t, w), lambda i: (i, 0))
    feat = lambda w: pl.BlockSpec((w, t), lambda i: (0, i))
    prev_block = lambda i: jnp.maximum(i * blocks_per_tile - 1, 0)
    k_prev = pl.BlockSpec((WINDOW, KV_WIDTH), lambda i: (prev_block(i), 0))
    vt_prev = pl.BlockSpec((KV_WIDTH, WINDOW), lambda i: (0, prev_block(i)))
    return pl.pallas_call(
        _attn_kernel,
        grid=(N_TOK // t,),
        in_specs=[pl.BlockSpec(memory_space=pltpu.SMEM), tok(D_MODEL), feat(D_MODEL), k_prev, tok(KV_WIDTH),
                  vt_prev, feat(KV_WIDTH), _resident((D_MODEL, D_MODEL)), _resident((1, D_MODEL))],
        out_specs=tok(D_MODEL),
        out_shape=jax.ShapeDtypeStruct((N_TOK, D_MODEL), F32),
        scratch_shapes=[pltpu.VMEM((D_MODEL, t), BF16)],
        compiler_params=pltpu.CompilerParams(dimension_semantics=("arbitrary",), vmem_limit_bytes=VMEM_LIMIT),
        name="swa_attention",
    )(sinks, x, qt, k, k, vt, vt, wo, bo)


def _router_kernel(x_ref, g_ref, wr_ref, hn_ref, idx_ref, wts_ref, rank_ref, cnt_ref):
    t = ROUTER_TILE

    @pl.when(pl.program_id(0) == 0)
    def _():
        cnt_ref[...] = jnp.zeros_like(cnt_ref)

    hn = _rms(x_ref[...], g_ref[...])
    _store_token_tiles(hn_ref, hn)
    hn_hi = hn.astype(BF16)
    hn_lo = (hn - hn_hi.astype(F32)).astype(BF16)
    wr = wr_ref[...]
    wr_hi = wr.astype(BF16)
    wr_lo = (wr - wr_hi.astype(F32)).astype(BF16)
    logits = _dot(hn_hi, wr_hi) + (_dot(hn_hi, wr_lo) + _dot(hn_lo, wr_hi))
    lane = lax.broadcasted_iota(jnp.int32, (t, LANES), 1)
    logits = jnp.where(lane < N_EXPERTS, logits, -jnp.inf)

    m1 = jnp.max(logits, axis=-1, keepdims=True)
    i1 = jnp.min(jnp.where(logits == m1, lane, LANES), axis=-1, keepdims=True)
    rest = jnp.where(lane == i1, -jnp.inf, logits)
    m2 = jnp.max(rest, axis=-1, keepdims=True)
    i2 = jnp.min(jnp.where(rest == m2, lane, LANES), axis=-1, keepdims=True)
    e2 = jnp.exp(m2 - m1)
    w1 = 1.0 / (1.0 + e2)
    w2 = e2 / (1.0 + e2)

    chosen = (lane == i1) | (lane == i2)
    onehot = jnp.where(chosen, 1.0, 0.0).astype(BF16)
    r = lax.broadcasted_iota(jnp.int32, (t, t), 0)
    c = lax.broadcasted_iota(jnp.int32, (t, t), 1)
    before = jnp.where(c < r, 1.0, 0.0).astype(BF16)
    seen = _dot(before, onehot) + cnt_ref[...]
    rank1 = jnp.sum(jnp.where(lane == i1, seen, 0.0), axis=-1, keepdims=True)
    rank2 = jnp.sum(jnp.where(lane == i2, seen, 0.0), axis=-1, keepdims=True)
    cnt_ref[...] = cnt_ref[...] + jnp.sum(onehot.astype(F32), axis=0, keepdims=True)

    idx_ref[...] = jnp.concatenate([i1, i2], axis=1)
    wts_ref[...] = jnp.concatenate([w1, w2], axis=1)
    rank_ref[...] = jnp.concatenate([rank1, rank2], axis=1).astype(jnp.int32)


def _moe_router(x, g, w_router_padded):
    t = ROUTER_TILE
    tok = lambda w: pl.BlockSpec((t, w), lambda i: (i, 0))
    return pl.pallas_call(
        _router_kernel,
        grid=(N_TOK // t,),
        in_specs=[tok(D_MODEL), _resident((1, D_MODEL)), _resident((D_MODEL, LANES))],
        out_specs=[pl.BlockSpec((t * SUBLANES, LANES), lambda i: (i, 0)), tok(2), tok(2), tok(2),
                   pl.BlockSpec((1, LANES), lambda i: (0, 0))],
        out_shape=[jax.ShapeDtypeStruct((N_TOK * SUBLANES, LANES), F32), jax.ShapeDtypeStruct((N_TOK, 2), jnp.int32),
                   jax.ShapeDtypeStruct((N_TOK, 2), F32), jax.ShapeDtypeStruct((N_TOK, 2), jnp.int32),
                   jax.ShapeDtypeStruct((1, LANES), F32)],
        compiler_params=pltpu.CompilerParams(dimension_semantics=("arbitrary",), vmem_limit_bytes=VMEM_LIMIT),
        name="moe_router",
    )(x, g, w_router_padded)


def _dispatch_kernel(pad_start_ref, pad_len_ref, nv_ref, pos_ref, hn_ref, xs_ref, zero_ref, sem):
    def row_copy(t, k):
        return pltpu.make_async_copy(_token(hn_ref, t), _token(xs_ref, pos_ref[2 * t + k]), sem)

    def issue(t, carry):
        row_copy(t, 0).start(priority=0)
        row_copy(t, 1).start(priority=1)
        return carry

    def drain(t, carry):
        row_copy(t, 0).wait()
        row_copy(t, 1).wait()
        return carry

    lax.fori_loop(0, DISPATCH_TILE, issue, 0, unroll=ROW_COPY_UNROLL)
    lax.fori_loop(0, DISPATCH_TILE, drain, 0, unroll=ROW_COPY_UNROLL)

    @pl.when(pl.program_id(0) == pl.num_programs(0) - 1)
    def _():
        zero_ref[...] = jnp.zeros_like(zero_ref)
        for e in range(N_EXPERTS):
            def zero_row(r, e=e):
                return pltpu.make_async_copy(_token(zero_ref, 0), _token(xs_ref, pad_start_ref[e] + r), sem)

            def row_issue(r, carry, zero_row=zero_row):
                zero_row(r).start()
                return carry

            def row_drain(r, carry, zero_row=zero_row):
                zero_row(r).wait()
                return carry

            lax.fori_loop(0, pad_len_ref[e], row_issue, 0)
            lax.fori_loop(0, pad_len_ref[e], row_drain, 0)

        def zero_tile(i):
            return pltpu.make_async_copy(zero_ref, _tokens(xs_ref, i * EXPERT_TILE, EXPERT_TILE), sem)

        def tile_issue(i, carry):
            zero_tile(i).start()
            return carry

        def tile_drain(i, carry):
            zero_tile(i).wait()
            return carry

        lax.fori_loop(nv_ref[0], N_ROW_TILES, tile_issue, 0)
        lax.fori_loop(nv_ref[0], N_ROW_TILES, tile_drain, 0)


def _moe_dispatch(pad_start, pad_len, n_valid, pos_flat, hn):
    t = DISPATCH_TILE
    grid_spec = pltpu.PrefetchScalarGridSpec(
        num_scalar_prefetch=3,
        grid=(N_TOK // t,),
        in_specs=[pl.BlockSpec((2 * t,), lambda i, ps, pn, nv: (i,), memory_space=pltpu.SMEM),
                  pl.BlockSpec((t * SUBLANES, LANES), lambda i, ps, pn, nv: (i, 0))],
        out_specs=pl.BlockSpec(memory_space=pl.ANY),
        scratch_shapes=[pltpu.VMEM((EXPERT_TILE * SUBLANES, LANES), F32), pltpu.SemaphoreType.DMA(())],
    )
    return pl.pallas_call(
        _dispatch_kernel,
        grid_spec=grid_spec,
        out_shape=jax.ShapeDtypeStruct((SORTED_ROWS * SUBLANES, LANES), F32),
        compiler_params=pltpu.CompilerParams(dimension_semantics=("arbitrary",), has_side_effects=True,
                                             vmem_limit_bytes=VMEM_LIMIT),
        name="moe_dispatch",
    )(pad_start, pad_len, n_valid, pos_flat, hn)


def _experts_kernel(te_ref, nr_ref, nv_ref, xs_ref, wg_ref, wu_ref, wd_ref, ys_ref, xb_ref, acc_ref):
    del te_ref
    i = pl.program_id(0)
    j = pl.program_id(1)
    last = pl.num_programs(1) - 1
    sub = EXPERT_SUB
    n_rows = nr_ref[i]

    @pl.when(i < nv_ref[0])
    def _():
        for sb in range(EXPERT_TILE // sub):
            rows = slice(sb * sub, (sb + 1) * sub)
            tiles = lambda s, sb=sb: pl.ds(sb * sub * SUBLANES + s, sub, stride=SUBLANES)

            @pl.when(sb * sub < n_rows)
            def _(rows=rows, tiles=tiles):
                @pl.when(j == 0)
                def _():
                    for s in range(SUBLANES):
                        xb_ref[rows, s * LANES:(s + 1) * LANES] = xs_ref[tiles(s), :].astype(BF16)

                xb = xb_ref[rows, :]
                a = (_silu(_dot(xb, wg_ref[...])) * _dot(xb, wu_ref[...])).astype(BF16)
                y = _dot(a, wd_ref[...])

                @pl.when(j == 0)
                def _():
                    acc_ref[rows, :] = y

                @pl.when((j > 0) & (j < last))
                def _():
                    acc_ref[rows, :] += y

                @pl.when(j == last)
                def _():
                    total = acc_ref[rows, :] + y
                    for s in range(SUBLANES):
                        ys_ref[tiles(s), :] = total[:, s * LANES:(s + 1) * LANES]

            @pl.when((sb * sub >= n_rows) & (j == last))
            def _(sb=sb):
                ys_ref[sb * sub * SUBLANES:(sb + 1) * sub * SUBLANES, :] = jnp.zeros((sub * SUBLANES, LANES), F32)

    @pl.when((i >= nv_ref[0]) & (j == last))
    def _():
        ys_ref[...] = jnp.zeros_like(ys_ref)


def _moe_experts(tile_expert, tile_rows, n_valid, xs, wg, wu, wd):
    tm, fb = EXPERT_TILE, EXPERT_FF_BLOCK
    n_fb = D_FF_EXPERT // fb
    assert n_fb >= 2
    rows = lambda i, j, te, nr, nv: jnp.maximum(jnp.minimum(i, nv[0] - 1), 0)
    ff = lambda i, j, te, nr, nv: jnp.where(i < nv[0], j, n_fb - 1)
    grid_spec = pltpu.PrefetchScalarGridSpec(
        num_scalar_prefetch=3,
        grid=(N_ROW_TILES, n_fb),
        in_specs=[pl.BlockSpec((tm * SUBLANES, LANES), lambda i, j, te, nr, nv: (rows(i, j, te, nr, nv), 0)),
                  pl.BlockSpec((None, D_MODEL, fb), lambda i, j, te, nr, nv: (te[i], 0, ff(i, j, te, nr, nv))),
                  pl.BlockSpec((None, D_MODEL, fb), lambda i, j, te, nr, nv: (te[i], 0, ff(i, j, te, nr, nv))),
                  pl.BlockSpec((None, fb, D_MODEL), lambda i, j, te, nr, nv: (te[i], ff(i, j, te, nr, nv), 0))],
        out_specs=pl.BlockSpec((tm * SUBLANES, LANES), lambda i, j, te, nr, nv: (i, 0)),
        scratch_shapes=[pltpu.VMEM((tm, D_MODEL), BF16), pltpu.VMEM((tm, D_MODEL), F32)],
    )
    return pl.pallas_call(
        _experts_kernel,
        grid_spec=grid_spec,
        out_shape=jax.ShapeDtypeStruct((SORTED_ROWS * SUBLANES, LANES), F32),
        compiler_params=pltpu.CompilerParams(dimension_semantics=("arbitrary", "arbitrary"),
                                             vmem_limit_bytes=VMEM_LIMIT),
        name="moe_experts",
    )(tile_expert, tile_rows, n_valid, xs, wg, wu, wd)


def _combine_kernel(pos_ref, pos_next_ref, x_ref, wts_ref, p_ref, ys_ref, wproj_ref, pg_ref, wgate_ref, fg_ref,
                    o_ref, rows_ref, sems):
    t = COMBINE_TILE
    i = pl.program_id(0)
    slot = i % 2

    def row_copy(pref, slot, r, k):
        return pltpu.make_async_copy(_token(ys_ref, pref[2 * r + k]), _token(rows_ref.at[slot, k], r),
                                     sems.at[slot])

    def issue(pref, slot):
        def body(r, carry):
            row_copy(pref, slot, r, 0).start(priority=0)
            row_copy(pref, slot, r, 1).start(priority=1)
            return carry
        lax.fori_loop(0, t, body, 0, unroll=ROW_COPY_UNROLL)

    def drain(pref, slot):
        def body(r, carry):
            row_copy(pref, slot, r, 0).wait()
            row_copy(pref, slot, r, 1).wait()
            return carry
        lax.fori_loop(0, t, body, 0, unroll=ROW_COPY_UNROLL)

    @pl.when(i == 0)
    def _():
        issue(pos_ref, 0)

    @pl.when(i + 1 < pl.num_programs(0))
    def _():
        issue(pos_next_ref, 1 - slot)

    drain(pos_ref, slot)

    w = wts_ref[...]
    x = (x_ref[...] + w[:, 0:1] * _load_token_tiles(rows_ref.at[slot, 0], t)
         + w[:, 1:2] * _load_token_tiles(rows_ref.at[slot, 1], t))
    gate = _sigmoid(_dot(_rms(x, pg_ref[...]).astype(BF16), wgate_ref[...]))
    x = x + _dot(p_ref[...].astype(BF16), wproj_ref[...]) * gate
    o_ref[...] = _rms(x, fg_ref[...])


def _moe_combine_out(pos_flat, x, wts, p1, ys, wproj, pg, wgate, fg):
    t = COMBINE_TILE
    n = N_TOK // t
    tok = lambda w: pl.BlockSpec((t, w), lambda i: (i, 0))
    return pl.pallas_call(
        _combine_kernel,
        grid=(n,),
        in_specs=[pl.BlockSpec((2 * t,), lambda i: (i,), memory_space=pltpu.SMEM),
                  pl.BlockSpec((2 * t,), lambda i: (jnp.minimum(i + 1, n - 1),), memory_space=pltpu.SMEM),
                  tok(D_MODEL), tok(2), _ple_input(t, 1), pl.BlockSpec(memory_space=pl.ANY),
                  _resident((PLE_DIM, D_MODEL)), _resident((1, D_MODEL)), _resident((D_MODEL, D_MODEL)),
                  _resident((1, D_MODEL))],
        out_specs=tok(D_MODEL),
        out_shape=jax.ShapeDtypeStruct((N_TOK, D_MODEL), F32),
        scratch_shapes=[pltpu.VMEM((2, 2, t * SUBLANES, LANES), F32), pltpu.SemaphoreType.DMA((2,))],
        compiler_params=pltpu.CompilerParams(dimension_semantics=("arbitrary",), vmem_limit_bytes=VMEM_LIMIT),
        name="moe_combine_out",
    )(pos_flat, pos_flat, x, wts, p1, ys, wproj, pg, wgate, fg)


def _rope_tables():
    half = HEAD_DIM // 2
    freqs = ROPE_THETA ** (-jnp.arange(0, HEAD_DIM, 2, dtype=F32) / HEAD_DIM)
    ang = jnp.arange(SEQ, dtype=F32)[:, None] * freqs[None, :]
    cos, sin = jnp.cos(ang), jnp.sin(ang)
    reps = LANES // HEAD_DIM
    cos_t = jnp.tile(jnp.concatenate([cos, cos], axis=1), (1, reps))
    sin_t = jnp.tile(jnp.concatenate([-sin, sin], axis=1), (1, reps))
    del half
    return cos_t, sin_t, cos.T, sin.T


def _routing_plan(idx, rank, counts):
    tm = EXPERT_TILE
    tiles = (counts + tm - 1) // tm
    tile_end = jnp.cumsum(tiles)
    group_start = (tile_end - tiles) * tm
    pos = (group_start[idx] + rank).reshape(-1).astype(jnp.int32)
    n_valid = tile_end[-1]
    tile_ids = jnp.minimum(jnp.arange(N_ROW_TILES, dtype=jnp.int32), n_valid - 1)
    tile_expert = jnp.sum(tile_ids[:, None] >= tile_end[None, :], axis=1).astype(jnp.int32)
    tile_first = (tile_end - tiles)[tile_expert]
    tile_rows = jnp.clip(counts[tile_expert] - (tile_ids - tile_first) * tm, 0, tm).astype(jnp.int32)
    pad_start = (group_start + counts).astype(jnp.int32)
    pad_len = (tiles * tm - counts).astype(jnp.int32)
    return pos, tile_expert, tile_rows, n_valid.reshape(1).astype(jnp.int32), pad_start, pad_len


def kernel(x, p, mix_norm_g, ffn_norm_g, gmlp_w_in, gmlp_b_in, gmlp_ln_g, gmlp_ln_b, gmlp_w_s, gmlp_b_s, gmlp_w_out, gmlp_b_out, kv_norm_g, w_kv, b_kv, attn_w_q, attn_b_q, attn_sinks, attn_w_o, attn_b_o, ffn_w_gate, ffn_w_up, ffn_w_down, moe_w_router, moe_w_gate, moe_w_up, moe_w_down, ple_w_proj, ple_norm_g, ple_w_gate, final_norm_g):
    row = lambda a: a.reshape(1, -1)
    bf = lambda a: a.astype(BF16)
    xf = x.reshape(N_TOK, D_MODEL)
    pf = p.reshape(2, N_TOK, PLE_DIM)

    x1, moe_gate_bf, moe_up_bf = _gmlp_mixer(
        xf, row(mix_norm_g[0]), bf(gmlp_w_in[0]), row(gmlp_b_in[0]), row(gmlp_ln_g[0]), row(gmlp_ln_b[0]),
        gmlp_w_s[0], gmlp_b_s[0][:, :, None], bf(gmlp_w_out[0]), row(gmlp_b_out[0]), moe_w_gate[0], moe_w_up[0])

    col = lambda a: a.reshape(-1, 1)
    x3, qt, k, vt, moe_down_bf = _ffn_ple_qkv(x1, pf, _rope_tables(), row(ffn_norm_g[0]), bf(ffn_w_gate[0]),
                                 bf(ffn_w_up[0]), bf(ffn_w_down[0]), bf(ple_w_proj[0]), row(ple_norm_g[0]),
                                 bf(ple_w_gate[0]), row(mix_norm_g[1]), bf(attn_w_q[0].T), col(attn_b_q[0]),
                                 row(kv_norm_g), bf(w_kv[:, :KV_WIDTH]), row(b_kv[:KV_WIDTH]),
                                 bf(w_kv[:, KV_WIDTH:].T), col(b_kv[KV_WIDTH:]), moe_w_down[0])

    x4 = _swa_attention(attn_sinks[0], x3, qt, k, vt, bf(attn_w_o[0]), row(attn_b_o[0]))

    w_router = jnp.pad(moe_w_router[0], ((0, 0), (0, LANES - N_EXPERTS)))
    hn, idx, wts, rank, counts = _moe_router(x4, row(ffn_norm_g[1]), w_router)
    pos, tile_expert, tile_rows, n_valid, pad_start, pad_len = _routing_plan(
        idx, rank, counts[0, :N_EXPERTS].astype(jnp.int32))

    xs = _moe_dispatch(pad_start, pad_len, n_valid, pos, hn)
    expert_w = lambda a, k, n: a.reshape(N_EXPERTS, k, n)
    ys = _moe_experts(tile_expert, tile_rows, n_valid, xs, expert_w(moe_gate_bf, D_MODEL, D_FF_EXPERT),
                      expert_w(moe_up_bf, D_MODEL, D_FF_EXPERT), expert_w(moe_down_bf, D_FF_EXPERT, D_MODEL))
    out = _moe_combine_out(pos, x4, wts, pf, ys, bf(ple_w_proj[1]), row(ple_norm_g[1]), bf(ple_w_gate[1]),
                           row(final_norm_g))
    return out.reshape(BATCH, SEQ, D_MODEL)
```

```python
import functools
import math

import jax
import jax.numpy as jnp
from jax import lax
from jax.experimental import pallas as pl
from jax.experimental.pallas import tpu as pltpu

F32 = jnp.float32
BF16 = jnp.bfloat16

D_MODEL = 1024
BATCH = 4
SEQ = 4096
N_TOK = BATCH * SEQ

CHUNK = 128
GMLP_FFN = 6 * D_MODEL
GMLP_HALF = GMLP_FFN // 2
GMLP_GROUPS = 8
GMLP_GROUP_DIM = GMLP_HALF // GMLP_GROUPS

N_HEADS = 16
N_KV_HEADS = 4
HEAD_DIM = 64
KV_REP = N_HEADS // N_KV_HEADS
KV_WIDTH = N_KV_HEADS * HEAD_DIM
WINDOW = 128
ROPE_THETA = 10000.0

D_FF_DENSE = 2816
N_EXPERTS = 8
D_FF_EXPERT = 3584
PLE_DIM = 256

EPS = 1e-6
MASK_VALUE = -1e30

LANES = 128
SUBLANES = 8
assert D_MODEL == SUBLANES * LANES

GMLP_TILE = 256
FFN_TILE = 256
ATTN_TILE = 512
ROUTER_TILE = 512
DISPATCH_TILE = 1024
ROW_COPY_UNROLL = 8
EXPERT_TILE = 1024
EXPERT_SUB = 512
EXPERT_FF_BLOCK = 1792
COMBINE_TILE = 256

SORTED_ROWS = 2 * N_TOK + N_EXPERTS * EXPERT_TILE
N_ROW_TILES = SORTED_ROWS // EXPERT_TILE

VMEM_LIMIT = 56 * 1024 * 1024


def _ple_input(t, layer):
    return pl.BlockSpec((None, t, PLE_DIM), lambda i: (layer, i, 0))


def _resident(shape):
    zeros = (0,) * len(shape)
    return pl.BlockSpec(shape, lambda *_: zeros, pipeline_mode=pl.Buffered(1))


def _rms(x, g):
    return x * lax.rsqrt(jnp.mean(x * x, axis=-1, keepdims=True) + EPS) * g


def _gelu_tanh(x):
    c = math.sqrt(2.0 / math.pi)
    return 0.5 * x * (1.0 + jnp.tanh(c * (x + 0.044715 * (x * x * x))))


def _silu(x):
    return x * (1.0 / (1.0 + jnp.exp(-x)))


def _sigmoid(x):
    return 1.0 / (1.0 + jnp.exp(-x))


def _dot(a, b):
    return jnp.dot(a, b, preferred_element_type=F32)


def _store_token_tiles(ref, x):
    t = x.shape[0]
    for s in range(SUBLANES):
        ref[pl.ds(s, t, stride=SUBLANES), :] = x[:, s * LANES:(s + 1) * LANES]


def _tokens(ref, first, n):
    return ref.at[pl.ds(first * SUBLANES, n * SUBLANES)]


def _token(ref, r):
    return _tokens(ref, r, 1)


def _load_token_tiles(ref, t):
    return jnp.concatenate([ref[pl.ds(s, t, stride=SUBLANES), :] for s in range(SUBLANES)], axis=1)


def _gmlp_kernel(x_ref, g_ref, win_ref, bin_ref, lng_ref, lnb_ref, ws_ref, bs_ref,
                 wout_ref, bout_ref, moe_g_ref, moe_u_ref, o_ref, moe_gb_ref, moe_ub_ref):
    moe_gb_ref[...] = moe_g_ref[...].astype(BF16)
    moe_ub_ref[...] = moe_u_ref[...].astype(BF16)

    x = x_ref[...]
    h = _rms(x, g_ref[...]).astype(BF16)
    v = _gelu_tanh(_dot(h, win_ref[:, GMLP_HALF:]) + bin_ref[:, GMLP_HALF:])
    mu = jnp.mean(v, axis=-1, keepdims=True)
    vc = v - mu
    var = jnp.mean(vc * vc, axis=-1, keepdims=True)
    vn = (vc * lax.rsqrt(var + EPS) * lng_ref[...] + lnb_ref[...]).astype(BF16)

    row = lax.broadcasted_iota(jnp.int32, (CHUNK, CHUNK), 0)
    col = lax.broadcasted_iota(jnp.int32, (CHUNK, CHUNK), 1)
    causal = col <= row

    pair_w = 2 * GMLP_GROUP_DIM
    acc = x + bout_ref[...]
    for pair in range(GMLP_GROUPS // 2):
        c0 = pair * pair_w
        u = _gelu_tanh(_dot(h, win_ref[:, c0:c0 + pair_w]) + bin_ref[:, c0:c0 + pair_w])
        parts = []
        for gi in range(2 * pair, 2 * pair + 2):
            ws = jnp.where(causal, ws_ref[gi], 0.0).astype(BF16)
            bs = bs_ref[gi]
            vg = vn[:, gi * GMLP_GROUP_DIM:(gi + 1) * GMLP_GROUP_DIM]
            rows = [_dot(ws, vg[c * CHUNK:(c + 1) * CHUNK]) + bs for c in range(GMLP_TILE // CHUNK)]
            parts.append(jnp.concatenate(rows, axis=0))
        mixed = jnp.concatenate(parts, axis=1)
        gated = (u * mixed).astype(BF16)
        acc = acc + _dot(gated, wout_ref[c0:c0 + pair_w, :])
    o_ref[...] = acc


def _cast_slab(rows, width, steps):
    assert rows % (steps * 16) == 0
    return pl.BlockSpec((rows // steps, width), lambda i: (i, 0))


def _gmlp_mixer(x, g, w_in, b_in, ln_g, ln_b, w_s, b_s, w_out, b_out, moe_w_gate, moe_w_up):
    t = GMLP_TILE
    steps = N_TOK // t
    tok = pl.BlockSpec((t, D_MODEL), lambda i: (i, 0))
    moe_rows = N_EXPERTS * D_MODEL
    slab = _cast_slab(moe_rows, D_FF_EXPERT, steps)
    moe_bf16 = jax.ShapeDtypeStruct((moe_rows, D_FF_EXPERT), BF16)
    return pl.pallas_call(
        _gmlp_kernel,
        grid=(steps,),
        in_specs=[tok, _resident((1, D_MODEL)), _resident((D_MODEL, GMLP_FFN)), _resident((1, GMLP_FFN)),
                  _resident((1, GMLP_HALF)), _resident((1, GMLP_HALF)),
                  _resident((GMLP_GROUPS, CHUNK, CHUNK)), _resident((GMLP_GROUPS, CHUNK, 1)),
                  _resident((GMLP_HALF, D_MODEL)), _resident((1, D_MODEL)), slab, slab],
        out_specs=[tok, slab, slab],
        out_shape=[jax.ShapeDtypeStruct((N_TOK, D_MODEL), F32), moe_bf16, moe_bf16],
        compiler_params=pltpu.CompilerParams(dimension_semantics=("arbitrary",), vmem_limit_bytes=VMEM_LIMIT),
        name="gmlp_mixer",
    )(x, g, w_in, b_in, ln_g, ln_b, w_s, b_s, w_out, b_out,
      moe_w_gate.reshape(moe_rows, D_FF_EXPERT), moe_w_up.reshape(moe_rows, D_FF_EXPERT))


def _rope_slices(t, cos, sin_signed, first_half):
    outs = []
    for j in range(t.shape[1] // LANES):
        s = t[:, j * LANES:(j + 1) * LANES]
        partner = jnp.where(first_half, pltpu.roll(s, LANES - HEAD_DIM // 2, 1), pltpu.roll(s, HEAD_DIM // 2, 1))
        outs.append(s * cos + partner * sin_signed)
    return jnp.concatenate(outs, axis=1)


def _dot_nt(a, b):
    return lax.dot_general(a, b, (((1,), (1,)), ((), ())), preferred_element_type=F32)


def _ffn_ple_qkv_kernel(x_ref, p_ref, cos_ref, sin_ref, cost_ref, sint_ref, fg_ref, wg_ref, wu_ref, wd_ref,
                        wproj_ref, pg_ref, wgate_ref, mg_ref, wqt_ref, bq_ref, kg_ref, wk_ref, bk_ref,
                        wvt_ref, bv_ref, moe_d_ref, x_out, qt_out, k_out, vt_out, moe_db_ref):
    moe_db_ref[...] = moe_d_ref[...].astype(BF16)

    x = x_ref[...]
    hn = _rms(x, fg_ref[...]).astype(BF16)
    a = (_silu(_dot(hn, wg_ref[...])) * _dot(hn, wu_ref[...])).astype(BF16)
    x = x + _dot(a, wd_ref[...])
    gate = _sigmoid(_dot(_rms(x, pg_ref[...]).astype(BF16), wgate_ref[...]))
    x = x + _dot(p_ref[...].astype(BF16), wproj_ref[...]) * gate
    x_out[...] = x

    qt = _dot_nt(wqt_ref[...], _rms(x, mg_ref[...]).astype(BF16)) + bq_ref[...]
    cos_t = cost_ref[...]
    sin_t = sint_ref[...]
    half = HEAD_DIM // 2
    scale = 1.0 / math.sqrt(HEAD_DIM)
    for h in range(N_HEADS):
        t1 = qt[h * HEAD_DIM:h * HEAD_DIM + half]
        t2 = qt[h * HEAD_DIM + half:(h + 1) * HEAD_DIM]
        qt_out[h * HEAD_DIM:h * HEAD_DIM + half, :] = ((t1 * cos_t - t2 * sin_t) * scale).astype(BF16)
        qt_out[h * HEAD_DIM + half:(h + 1) * HEAD_DIM, :] = ((t2 * cos_t + t1 * sin_t) * scale).astype(BF16)

    hkv = _rms(x, kg_ref[...]).astype(BF16)
    lane = lax.broadcasted_iota(jnp.int32, cos_ref.shape, 1)
    first_half = (lane % HEAD_DIM) < half
    k = _rope_slices(_dot(hkv, wk_ref[...]) + bk_ref[...], cos_ref[...], sin_ref[...], first_half)
    k_out[...] = k.astype(BF16)
    vt_out[...] = (_dot_nt(wvt_ref[...], hkv) + bv_ref[...]).astype(BF16)


def _ffn_ple_qkv(x, p0, rope_tabs, fg, wg, wu, wd, wproj, pg, wgate, mg, wqt, bq_col, kg, wk, bk, wvt, bv_col,
                 moe_w_down):
    t = FFN_TILE
    moe_rows = N_EXPERTS * D_FF_EXPERT
    slab = _cast_slab(moe_rows, D_MODEL, N_TOK // t)
    per_seq = SEQ // t
    half = HEAD_DIM // 2
    tok = lambda w: pl.BlockSpec((t, w), lambda i: (i, 0))
    feat = lambda w: pl.BlockSpec((w, t), lambda i: (0, i))
    rope = pl.BlockSpec((t, LANES), lambda i: (i % per_seq, 0))
    rope_t = pl.BlockSpec((half, t), lambda i: (0, i % per_seq))
    cos_t, sin_t, cos_tt, sin_tt = rope_tabs
    return pl.pallas_call(
        _ffn_ple_qkv_kernel,
        grid=(N_TOK // t,),
        in_specs=[tok(D_MODEL), _ple_input(t, 0), rope, rope, rope_t, rope_t,
                  _resident((1, D_MODEL)), _resident((D_MODEL, D_FF_DENSE)), _resident((D_MODEL, D_FF_DENSE)),
                  _resident((D_FF_DENSE, D_MODEL)),
                  _resident((PLE_DIM, D_MODEL)), _resident((1, D_MODEL)), _resident((D_MODEL, D_MODEL)),
                  _resident((1, D_MODEL)), _resident((D_MODEL, D_MODEL)), _resident((D_MODEL, 1)),
                  _resident((1, D_MODEL)), _resident((D_MODEL, KV_WIDTH)), _resident((1, KV_WIDTH)),
                  _resident((KV_WIDTH, D_MODEL)), _resident((KV_WIDTH, 1)), slab],
        out_specs=[tok(D_MODEL), feat(D_MODEL), tok(KV_WIDTH), feat(KV_WIDTH), slab],
        out_shape=[jax.ShapeDtypeStruct((N_TOK, D_MODEL), F32), jax.ShapeDtypeStruct((D_MODEL, N_TOK), BF16),
                   jax.ShapeDtypeStruct((N_TOK, KV_WIDTH), BF16), jax.ShapeDtypeStruct((KV_WIDTH, N_TOK), BF16),
                   jax.ShapeDtypeStruct((moe_rows, D_MODEL), BF16)],
        compiler_params=pltpu.CompilerParams(dimension_semantics=("arbitrary",), vmem_limit_bytes=VMEM_LIMIT),
        name="ffn_ple_qkv",
    )(x, p0, cos_t, sin_t, cos_tt, sin_tt, fg, wg, wu, wd, wproj, pg, wgate, mg, wqt, bq_col, kg, wk, bk,
      wvt, bv_col, moe_w_down.reshape(moe_rows, D_MODEL))


def _attn_kernel(sink_ref, x_ref, qt_ref, kp_ref, kc_ref, vtp_ref, vtc_ref, wo_ref, bo_ref, o_ref, attnt_ref):
    i = pl.program_id(0)
    seq_start = (i % (SEQ // ATTN_TILE)) == 0
    kk = jnp.concatenate([kp_ref[...], kc_ref[...]], axis=0)
    vvt = jnp.concatenate([vtp_ref[...], vtc_ref[...]], axis=1)

    width = KV_REP * WINDOW
    kj = lax.broadcasted_iota(jnp.int32, (2 * WINDOW, width), 0)
    qi = lax.broadcasted_iota(jnp.int32, (2 * WINDOW, width), 1) % WINDOW
    band = (kj > qi) & (kj <= qi + WINDOW)
    first_band = band & ((kj >= WINDOW) | jnp.logical_not(seq_start))
    rep = lax.broadcasted_iota(jnp.int32, (1, width), 1) // WINDOW

    for kh in range(N_KV_HEADS):
        feats = slice(kh * HEAD_DIM, (kh + 1) * HEAD_DIM)
        sink = jnp.zeros((1, width), F32)
        for r in range(KV_REP):
            sink = jnp.where(rep == r, sink_ref[kh * KV_REP + r], sink)
        for sb in range(ATTN_TILE // WINDOW):
            toks = slice(sb * WINDOW, (sb + 1) * WINDOW)
            keys = slice(sb * WINDOW, sb * WINDOW + 2 * WINDOW)
            qt = jnp.concatenate(
                [qt_ref[(kh * KV_REP + r) * HEAD_DIM:(kh * KV_REP + r + 1) * HEAD_DIM, toks] for r in range(KV_REP)],
                axis=1)
            s = _dot(kk[keys, feats], qt)
            s = jnp.where(first_band if sb == 0 else band, s, MASK_VALUE)
            m = jnp.maximum(jnp.max(s, axis=0, keepdims=True), sink)
            pr = jnp.exp(s - m)
            den = jnp.sum(pr, axis=0, keepdims=True) + jnp.exp(sink - m)
            ot = _dot(vvt[feats, keys], pr.astype(BF16)) * (1.0 / den)
            for r in range(KV_REP):
                h = kh * KV_REP + r
                attnt_ref[h * HEAD_DIM:(h + 1) * HEAD_DIM, toks] = ot[:, r * WINDOW:(r + 1) * WINDOW].astype(BF16)
    attn_out = lax.dot_general(attnt_ref[...], wo_ref[...], (((0,), (0,)), ((), ())), preferred_element_type=F32)
    o_ref[...] = x_ref[...] + attn_out + bo_ref[...]


def _swa_attention(sinks, x, qt, k, vt, wo, bo):
    t = ATTN_TILE
    blocks_per_tile = t // WINDOW
    tok = lambda w: pl.BlockSpec((t, w), lambda i: (i, 0))
    feat = lambda w: pl.BlockSpec((w, t), lambda i: (0, i))
    prev_block = lambda i: jnp.maximum(i * blocks_per_tile - 1, 0)
    k_prev = pl.BlockSpec((WINDOW, KV_WIDTH), lambda i: (prev_block(i), 0))
    vt_prev = pl.BlockSpec((KV_WIDTH, WINDOW), lambda i: (0, prev_block(i)))
    return pl.pallas_call(
        _attn_kernel,
        grid=(N_TOK // t,),
        in_specs=[pl.BlockSpec(memory_space=pltpu.SMEM), tok(D_MODEL), feat(D_MODEL), k_prev, tok(KV_WIDTH),
                  vt_prev, feat(KV_WIDTH), _resident((D_MODEL, D_MODEL)), _resident((1, D_MODEL))],
        out_specs=tok(D_MODEL),
        out_shape=jax.ShapeDtypeStruct((N_TOK, D_MODEL), F32),
        scratch_shapes=[pltpu.VMEM((D_MODEL, t), BF16)],
        compiler_params=pltpu.CompilerParams(dimension_semantics=("arbitrary",), vmem_limit_bytes=VMEM_LIMIT),
        name="swa_attention",
    )(sinks, x, qt, k, k, vt, vt, wo, bo)


def _router_kernel(x_ref, g_ref, wr_ref, hn_ref, idx_ref, wts_ref, rank_ref, cnt_ref):
    t = ROUTER_TILE

    @pl.when(pl.program_id(0) == 0)
    def _():
        cnt_ref[...] = jnp.zeros_like(cnt_ref)

    hn = _rms(x_ref[...], g_ref[...])
    _store_token_tiles(hn_ref, hn)
    hn_hi = hn.astype(BF16)
    hn_lo = (hn - hn_hi.astype(F32)).astype(BF16)
    wr = wr_ref[...]
    wr_hi = wr.astype(BF16)
    wr_lo = (wr - wr_hi.astype(F32)).astype(BF16)
    logits = _dot(hn_hi, wr_hi) + (_dot(hn_hi, wr_lo) + _dot(hn_lo, wr_hi))
    lane = lax.broadcasted_iota(jnp.int32, (t, LANES), 1)
    logits = jnp.where(lane < N_EXPERTS, logits, -jnp.inf)

    m1 = jnp.max(logits, axis=-1, keepdims=True)
    i1 = jnp.min(jnp.where(logits == m1, lane, LANES), axis=-1, keepdims=True)
    rest = jnp.where(lane == i1, -jnp.inf, logits)
    m2 = jnp.max(rest, axis=-1, keepdims=True)
    i2 = jnp.min(jnp.where(rest == m2, lane, LANES), axis=-1, keepdims=True)
    e2 = jnp.exp(m2 - m1)
    w1 = 1.0 / (1.0 + e2)
    w2 = e2 / (1.0 + e2)

    chosen = (lane == i1) | (lane == i2)
    onehot = jnp.where(chosen, 1.0, 0.0).astype(BF16)
    r = lax.broadcasted_iota(jnp.int32, (t, t), 0)
    c = lax.broadcasted_iota(jnp.int32, (t, t), 1)
    before = jnp.where(c < r, 1.0, 0.0).astype(BF16)
    seen = _dot(before, onehot) + cnt_ref[...]
    rank1 = jnp.sum(jnp.where(lane == i1, seen, 0.0), axis=-1, keepdims=True)
    rank2 = jnp.sum(jnp.where(lane == i2, seen, 0.0), axis=-1, keepdims=True)
    cnt_ref[...] = cnt_ref[...] + jnp.sum(onehot.astype(F32), axis=0, keepdims=True)

    idx_ref[...] = jnp.concatenate([i1, i2], axis=1)
    wts_ref[...] = jnp.concatenate([w1, w2], axis=1)
    rank_ref[...] = jnp.concatenate([rank1, rank2], axis=1).astype(jnp.int32)


def _moe_router(x, g, w_router_padded):
    t = ROUTER_TILE
    tok = lambda w: pl.BlockSpec((t, w), lambda i: (i, 0))
    return pl.pallas_call(
        _router_kernel,
        grid=(N_TOK // t,),
        in_specs=[tok(D_MODEL), _resident((1, D_MODEL)), _resident((D_MODEL, LANES))],
        out_specs=[pl.BlockSpec((t * SUBLANES, LANES), lambda i: (i, 0)), tok(2), tok(2), tok(2),
                   pl.BlockSpec((1, LANES), lambda i: (0, 0))],
        out_shape=[jax.ShapeDtypeStruct((N_TOK * SUBLANES, LANES), F32), jax.ShapeDtypeStruct((N_TOK, 2), jnp.int32),
                   jax.ShapeDtypeStruct((N_TOK, 2), F32), jax.ShapeDtypeStruct((N_TOK, 2), jnp.int32),
                   jax.ShapeDtypeStruct((1, LANES), F32)],
        compiler_params=pltpu.CompilerParams(dimension_semantics=("arbitrary",), vmem_limit_bytes=VMEM_LIMIT),
        name="moe_router",
    )(x, g, w_router_padded)


def _dispatch_kernel(pad_start_ref, pad_len_ref, pad_blocks_ref, nv_ref, pos_ref, hn_ref, xs_ref, zero_ref, sem):
    def row_copy(t, k):
        return pltpu.make_async_copy(_token(hn_ref, t), _token(xs_ref, pos_ref[2 * t + k]), sem)

    def issue(t, carry):
        row_copy(t, 0).start(priority=0)
        row_copy(t, 1).start(priority=1)
        return carry

    def drain(t, carry):
        row_copy(t, 0).wait()
        row_copy(t, 1).wait()
        return carry

    lax.fori_loop(0, DISPATCH_TILE, issue, 0, unroll=ROW_COPY_UNROLL)
    lax.fori_loop(0, DISPATCH_TILE, drain, 0, unroll=ROW_COPY_UNROLL)

    @pl.when(pl.program_id(0) == pl.num_programs(0) - 1)
    def _():
        zero_ref[...] = jnp.zeros_like(zero_ref)
        for e in range(N_EXPERTS):
            def zero_row(r, e=e):
                return pltpu.make_async_copy(_token(zero_ref, 0), _token(xs_ref, pad_start_ref[e] + r), sem)

            def row_issue(r, carry, zero_row=zero_row):
                zero_row(r).start()
                return carry

            def row_drain(r, carry, zero_row=zero_row):
                zero_row(r).wait()
                return carry

            lax.fori_loop(0, pad_len_ref[e], row_issue, 0)
            lax.fori_loop(0, pad_len_ref[e], row_drain, 0)

        def zero_block(first_token):
            return pltpu.make_async_copy(zero_ref, _tokens(xs_ref, first_token, EXPERT_SUB), sem)

        def fill_blocks(first_token, n_blocks):
            def block_issue(b, carry):
                zero_block(first_token + b * EXPERT_SUB).start()
                return carry

            def block_drain(b, carry):
                zero_block(first_token + b * EXPERT_SUB).wait()
                return carry

            lax.fori_loop(0, n_blocks, block_issue, 0)
            lax.fori_loop(0, n_blocks, block_drain, 0)

        for e in range(N_EXPERTS):
            fill_blocks(pad_start_ref[e] + pad_len_ref[e], pad_blocks_ref[e])
        fill_blocks(nv_ref[0] * EXPERT_TILE, (N_ROW_TILES - nv_ref[0]) * (EXPERT_TILE // EXPERT_SUB))


def _moe_dispatch(pad_start, pad_len, pad_blocks, n_valid, pos_flat, hn):
    t = DISPATCH_TILE
    grid_spec = pltpu.PrefetchScalarGridSpec(
        num_scalar_prefetch=4,
        grid=(N_TOK // t,),
        in_specs=[pl.BlockSpec((2 * t,), lambda i, ps, pn, pb, nv: (i,), memory_space=pltpu.SMEM),
                  pl.BlockSpec((t * SUBLANES, LANES), lambda i, ps, pn, pb, nv: (i, 0))],
        out_specs=pl.BlockSpec(memory_space=pl.ANY),
        scratch_shapes=[pltpu.VMEM((EXPERT_SUB * SUBLANES, LANES), F32), pltpu.SemaphoreType.DMA(())],
    )
    return pl.pallas_call(
        _dispatch_kernel,
        grid_spec=grid_spec,
        out_shape=jax.ShapeDtypeStruct((SORTED_ROWS * SUBLANES, LANES), F32),
        compiler_params=pltpu.CompilerParams(dimension_semantics=("arbitrary",), has_side_effects=True,
                                             vmem_limit_bytes=VMEM_LIMIT),
        name="moe_dispatch",
    )(pad_start, pad_len, pad_blocks, n_valid, pos_flat, hn)


def _experts_kernel(te_ref, nr_ref, nv_ref, xs_ref, wg_ref, wu_ref, wd_ref, ys_ref, xb_ref, acc_ref):
    del te_ref
    i = pl.program_id(0)
    j = pl.program_id(1)
    last = pl.num_programs(1) - 1
    sub = EXPERT_SUB
    n_rows = nr_ref[i]

    @pl.when(i < nv_ref[0])
    def _():
        for sb in range(EXPERT_TILE // sub):
            rows = slice(sb * sub, (sb + 1) * sub)
            tiles = lambda s, sb=sb: pl.ds(sb * sub * SUBLANES + s, sub, stride=SUBLANES)

            @pl.when(sb * sub < n_rows)
            def _(rows=rows, tiles=tiles):
                @pl.when(j == 0)
                def _():
                    for s in range(SUBLANES):
                        xb_ref[rows, s * LANES:(s + 1) * LANES] = xs_ref[tiles(s), :].astype(BF16)

                xb = xb_ref[rows, :]
                a = (_silu(_dot(xb, wg_ref[...])) * _dot(xb, wu_ref[...])).astype(BF16)
                y = _dot(a, wd_ref[...])

                @pl.when(j == 0)
                def _():
                    acc_ref[rows, :] = y

                @pl.when((j > 0) & (j < last))
                def _():
                    acc_ref[rows, :] += y

                @pl.when(j == last)
                def _():
                    total = acc_ref[rows, :] + y
                    for s in range(SUBLANES):
                        ys_ref[tiles(s), :] = total[:, s * LANES:(s + 1) * LANES]

            @pl.when((sb * sub >= n_rows) & (j == last))
            def _(sb=sb):
                ys_ref[sb * sub * SUBLANES:(sb + 1) * sub * SUBLANES, :] = jnp.zeros((sub * SUBLANES, LANES), F32)

    @pl.when((i >= nv_ref[0]) & (j == last))
    def _():
        ys_ref[...] = jnp.zeros_like(ys_ref)


def _moe_experts(tile_expert, tile_rows, n_valid, xs, wg, wu, wd):
    tm, fb = EXPERT_TILE, EXPERT_FF_BLOCK
    n_fb = D_FF_EXPERT // fb
    assert n_fb >= 2
    rows = lambda i, j, te, nr, nv: jnp.maximum(jnp.minimum(i, nv[0] - 1), 0)
    ff = lambda i, j, te, nr, nv: jnp.where(i < nv[0], j, n_fb - 1)
    grid_spec = pltpu.PrefetchScalarGridSpec(
        num_scalar_prefetch=3,
        grid=(N_ROW_TILES, n_fb),
        in_specs=[pl.BlockSpec((tm * SUBLANES, LANES), lambda i, j, te, nr, nv: (rows(i, j, te, nr, nv), 0)),
                  pl.BlockSpec((None, D_MODEL, fb), lambda i, j, te, nr, nv: (te[i], 0, ff(i, j, te, nr, nv))),
                  pl.BlockSpec((None, D_MODEL, fb), lambda i, j, te, nr, nv: (te[i], 0, ff(i, j, te, nr, nv))),
                  pl.BlockSpec((None, fb, D_MODEL), lambda i, j, te, nr, nv: (te[i], ff(i, j, te, nr, nv), 0))],
        out_specs=pl.BlockSpec((tm * SUBLANES, LANES), lambda i, j, te, nr, nv: (i, 0)),
        scratch_shapes=[pltpu.VMEM((tm, D_MODEL), BF16), pltpu.VMEM((tm, D_MODEL), F32)],
    )
    return pl.pallas_call(
        _experts_kernel,
        grid_spec=grid_spec,
        out_shape=jax.ShapeDtypeStruct((SORTED_ROWS * SUBLANES, LANES), F32),
        compiler_params=pltpu.CompilerParams(dimension_semantics=("arbitrary", "arbitrary"),
                                             vmem_limit_bytes=VMEM_LIMIT),
        name="moe_experts",
    )(tile_expert, tile_rows, n_valid, xs, wg, wu, wd)


def _combine_kernel(pos_ref, pos_next_ref, x_ref, wts_ref, p_ref, ys_ref, wproj_ref, pg_ref, wgate_ref, fg_ref,
                    o_ref, rows_ref, sems):
    t = COMBINE_TILE
    i = pl.program_id(0)
    slot = i % 2

    def row_copy(pref, slot, r, k):
        return pltpu.make_async_copy(_token(ys_ref, pref[2 * r + k]), _token(rows_ref.at[slot, k], r),
                                     sems.at[slot])

    def issue(pref, slot):
        def body(r, carry):
            row_copy(pref, slot, r, 0).start(priority=0)
            row_copy(pref, slot, r, 1).start(priority=1)
            return carry
        lax.fori_loop(0, t, body, 0, unroll=ROW_COPY_UNROLL)

    def drain(pref, slot):
        def body(r, carry):
            row_copy(pref, slot, r, 0).wait()
            row_copy(pref, slot, r, 1).wait()
            return carry
        lax.fori_loop(0, t, body, 0, unroll=ROW_COPY_UNROLL)

    @pl.when(i == 0)
    def _():
        issue(pos_ref, 0)

    @pl.when(i + 1 < pl.num_programs(0))
    def _():
        issue(pos_next_ref, 1 - slot)

    drain(pos_ref, slot)

    w = wts_ref[...]
    x = (x_ref[...] + w[:, 0:1] * _load_token_tiles(rows_ref.at[slot, 0], t)
         + w[:, 1:2] * _load_token_tiles(rows_ref.at[slot, 1], t))
    gate = _sigmoid(_dot(_rms(x, pg_ref[...]).astype(BF16), wgate_ref[...]))
    x = x + _dot(p_ref[...].astype(BF16), wproj_ref[...]) * gate
    o_ref[...] = _rms(x, fg_ref[...])


def _moe_combine_out(pos_flat, x, wts, p1, ys, wproj, pg, wgate, fg):
    t = COMBINE_TILE
    n = N_TOK // t
    tok = lambda w: pl.BlockSpec((t, w), lambda i: (i, 0))
    return pl.pallas_call(
        _combine_kernel,
        grid=(n,),
        in_specs=[pl.BlockSpec((2 * t,), lambda i: (i,), memory_space=pltpu.SMEM),
                  pl.BlockSpec((2 * t,), lambda i: (jnp.minimum(i + 1, n - 1),), memory_space=pltpu.SMEM),
                  tok(D_MODEL), tok(2), _ple_input(t, 1), pl.BlockSpec(memory_space=pl.ANY),
                  _resident((PLE_DIM, D_MODEL)), _resident((1, D_MODEL)), _resident((D_MODEL, D_MODEL)),
                  _resident((1, D_MODEL))],
        out_specs=tok(D_MODEL),
        out_shape=jax.ShapeDtypeStruct((N_TOK, D_MODEL), F32),
        scratch_shapes=[pltpu.VMEM((2, 2, t * SUBLANES, LANES), F32), pltpu.SemaphoreType.DMA((2,))],
        compiler_params=pltpu.CompilerParams(dimension_semantics=("arbitrary",), vmem_limit_bytes=VMEM_LIMIT),
        name="moe_combine_out",
    )(pos_flat, pos_flat, x, wts, p1, ys, wproj, pg, wgate, fg)


def _rope_tables():
    half = HEAD_DIM // 2
    freqs = ROPE_THETA ** (-jnp.arange(0, HEAD_DIM, 2, dtype=F32) / HEAD_DIM)
    ang = jnp.arange(SEQ, dtype=F32)[:, None] * freqs[None, :]
    cos, sin = jnp.cos(ang), jnp.sin(ang)
    reps = LANES // HEAD_DIM
    cos_t = jnp.tile(jnp.concatenate([cos, cos], axis=1), (1, reps))
    sin_t = jnp.tile(jnp.concatenate([-sin, sin], axis=1), (1, reps))
    del half
    return cos_t, sin_t, cos.T, sin.T


def _routing_plan(idx, rank, counts):
    tm = EXPERT_TILE
    tiles = (counts + tm - 1) // tm
    tile_end = jnp.cumsum(tiles)
    group_start = (tile_end - tiles) * tm
    pos = (group_start[idx] + rank).reshape(-1).astype(jnp.int32)
    n_valid = tile_end[-1]
    tile_ids = jnp.minimum(jnp.arange(N_ROW_TILES, dtype=jnp.int32), n_valid - 1)
    tile_expert = jnp.sum(tile_ids[:, None] >= tile_end[None, :], axis=1).astype(jnp.int32)
    tile_first = (tile_end - tiles)[tile_expert]
    tile_rows = jnp.clip(counts[tile_expert] - (tile_ids - tile_first) * tm, 0, tm).astype(jnp.int32)
    sub = EXPERT_SUB
    pad_start = (group_start + counts).astype(jnp.int32)
    pad_len = ((counts + sub - 1) // sub * sub - counts).astype(jnp.int32)
    pad_blocks = ((tiles * tm - counts - pad_len) // sub).astype(jnp.int32)
    return pos, tile_expert, tile_rows, n_valid.reshape(1).astype(jnp.int32), pad_start, pad_len, pad_blocks


def kernel(x, p, mix_norm_g, ffn_norm_g, gmlp_w_in, gmlp_b_in, gmlp_ln_g, gmlp_ln_b, gmlp_w_s, gmlp_b_s, gmlp_w_out, gmlp_b_out, kv_norm_g, w_kv, b_kv, attn_w_q, attn_b_q, attn_sinks, attn_w_o, attn_b_o, ffn_w_gate, ffn_w_up, ffn_w_down, moe_w_router, moe_w_gate, moe_w_up, moe_w_down, ple_w_proj, ple_norm_g, ple_w_gate, final_norm_g):
    row = lambda a: a.reshape(1, -1)
    bf = lambda a: a.astype(BF16)
    xf = x.reshape(N_TOK, D_MODEL)
    pf = p.reshape(2, N_TOK, PLE_DIM)

    x1, moe_gate_bf, moe_up_bf = _gmlp_mixer(
        xf, row(mix_norm_g[0]), bf(gmlp_w_in[0]), row(gmlp_b_in[0]), row(gmlp_ln_g[0]), row(gmlp_ln_b[0]),
        gmlp_w_s[0], gmlp_b_s[0][:, :, None], bf(gmlp_w_out[0]), row(gmlp_b_out[0]), moe_w_gate[0], moe_w_up[0])

    col = lambda a: a.reshape(-1, 1)
    x3, qt, k, vt, moe_down_bf = _ffn_ple_qkv(x1, pf, _rope_tables(), row(ffn_norm_g[0]), bf(ffn_w_gate[0]),
                                 bf(ffn_w_up[0]), bf(ffn_w_down[0]), bf(ple_w_proj[0]), row(ple_norm_g[0]),
                                 bf(ple_w_gate[0]), row(mix_norm_g[1]), bf(attn_w_q[0].T), col(attn_b_q[0]),
                                 row(kv_norm_g), bf(w_kv[:, :KV_WIDTH]), row(b_kv[:KV_WIDTH]),
                                 bf(w_kv[:, KV_WIDTH:].T), col(b_kv[KV_WIDTH:]), moe_w_down[0])

    x4 = _swa_attention(attn_sinks[0], x3, qt, k, vt, bf(attn_w_o[0]), row(attn_b_o[0]))

    w_router = jnp.pad(moe_w_router[0], ((0, 0), (0, LANES - N_EXPERTS)))
    hn, idx, wts, rank, counts = _moe_router(x4, row(ffn_norm_g[1]), w_router)
    pos, tile_expert, tile_rows, n_valid, pad_start, pad_len, pad_blocks = _routing_plan(
        idx, rank, counts[0, :N_EXPERTS].astype(jnp.int32))

    xs = _moe_dispatch(pad_start, pad_len, pad_blocks, n_valid, pos, hn)
    expert_w = lambda a, k, n: a.reshape(N_EXPERTS, k, n)
    ys = _moe_experts(tile_expert, tile_rows, n_valid, xs, expert_w(moe_gate_bf, D_MODEL, D_FF_EXPERT),
                      expert_w(moe_up_bf, D_MODEL, D_FF_EXPERT), expert_w(moe_down_bf, D_FF_EXPERT, D_MODEL))
    out = _moe_combine_out(pos, x4, wts, pf, ys, bf(ple_w_proj[1]), row(ple_norm_g[1]), bf(ple_w_gate[1]),
                           row(final_norm_g))
    return out.reshape(BATCH, SEQ, D_MODEL)
```

```python
import functools
import math

import jax
import jax.numpy as jnp
from jax import lax
from jax.experimental import pallas as pl
from jax.experimental.pallas import tpu as pltpu

F32 = jnp.float32
BF16 = jnp.bfloat16

D_MODEL = 1024
BATCH = 4
SEQ = 4096
N_TOK = BATCH * SEQ

CHUNK = 128
GMLP_FFN = 6 * D_MODEL
GMLP_HALF = GMLP_FFN // 2
GMLP_GROUPS = 8
GMLP_GROUP_DIM = GMLP_HALF // GMLP_GROUPS

N_HEADS = 16
N_KV_HEADS = 4
HEAD_DIM = 64
KV_REP = N_HEADS // N_KV_HEADS
KV_WIDTH = N_KV_HEADS * HEAD_DIM
WINDOW = 128
ROPE_THETA = 10000.0

D_FF_DENSE = 2816
N_EXPERTS = 8
D_FF_EXPERT = 3584
PLE_DIM = 256

EPS = 1e-6
MASK_VALUE = -1e30

LANES = 128
SUBLANES = 8
assert D_MODEL == SUBLANES * LANES

GMLP_TILE = 256
FFN_TILE = 256
ATTN_TILE = 512
ROUTER_TILE = 512
DISPATCH_TILE = 1024
ROW_COPY_UNROLL = 8
EXPERT_TILE = 1024
EXPERT_SUB = 512
EXPERT_FF_BLOCK = 1792
COMBINE_TILE = 256

SORTED_ROWS = 2 * N_TOK + N_EXPERTS * EXPERT_TILE
N_ROW_TILES = SORTED_ROWS // EXPERT_TILE

VMEM_LIMIT = 56 * 1024 * 1024


def _ple_input(t, layer):
    return pl.BlockSpec((None, t, PLE_DIM), lambda i: (layer, i, 0))


def _resident(shape):
    zeros = (0,) * len(shape)
    return pl.BlockSpec(shape, lambda *_: zeros, pipeline_mode=pl.Buffered(1))


def _rms(x, g):
    return x * lax.rsqrt(jnp.mean(x * x, axis=-1, keepdims=True) + EPS) * g


def _gelu_tanh(x):
    c = math.sqrt(2.0 / math.pi)
    return 0.5 * x * (1.0 + jnp.tanh(c * (x + 0.044715 * (x * x * x))))


def _silu(x):
    return x * (1.0 / (1.0 + jnp.exp(-x)))


def _sigmoid(x):
    return 1.0 / (1.0 + jnp.exp(-x))


def _dot(a, b):
    return jnp.dot(a, b, preferred_element_type=F32)


def _store_token_tiles(ref, x):
    t = x.shape[0]
    for s in range(SUBLANES):
        ref[pl.ds(s, t, stride=SUBLANES), :] = x[:, s * LANES:(s + 1) * LANES]


def _tokens(ref, first, n):
    return ref.at[pl.ds(first * SUBLANES, n * SUBLANES)]


def _token(ref, r):
    return _tokens(ref, r, 1)


def _load_token_tiles(ref, t):
    return jnp.concatenate([ref[pl.ds(s, t, stride=SUBLANES), :] for s in range(SUBLANES)], axis=1)


def _gmlp_kernel(x_ref, g_ref, win_ref, bin_ref, lng_ref, lnb_ref, ws_ref, bs_ref,
                 wout_ref, bout_ref, moe_g_ref, moe_u_ref, o_ref, moe_gb_ref, moe_ub_ref):
    moe_gb_ref[...] = moe_g_ref[...].astype(BF16)
    moe_ub_ref[...] = moe_u_ref[...].astype(BF16)

    x = x_ref[...]
    h = _rms(x, g_ref[...]).astype(BF16)
    pair_w = 2 * GMLP_GROUP_DIM
    n_pairs = GMLP_GROUPS // 2

    def u_product(pair):
        c0 = pair * pair_w
        return _dot(h, win_ref[:, c0:c0 + pair_w]) + bin_ref[:, c0:c0 + pair_w]

    v_pre = _dot(h, win_ref[:, GMLP_HALF:]) + bin_ref[:, GMLP_HALF:]
    u_pre = u_product(0)
    v = _gelu_tanh(v_pre)
    mu = jnp.mean(v, axis=-1, keepdims=True)
    vc = v - mu
    var = jnp.mean(vc * vc, axis=-1, keepdims=True)
    vn = (vc * lax.rsqrt(var + EPS) * lng_ref[...] + lnb_ref[...]).astype(BF16)

    row = lax.broadcasted_iota(jnp.int32, (CHUNK, CHUNK), 0)
    col = lax.broadcasted_iota(jnp.int32, (CHUNK, CHUNK), 1)
    causal = col <= row

    acc = x + bout_ref[...]
    for pair in range(n_pairs):
        c0 = pair * pair_w
        u_next = u_product(pair + 1) if pair + 1 < n_pairs else None
        u = _gelu_tanh(u_pre)
        parts = []
        for gi in range(2 * pair, 2 * pair + 2):
            ws = jnp.where(causal, ws_ref[gi], 0.0).astype(BF16)
            bs = bs_ref[gi]
            vg = vn[:, gi * GMLP_GROUP_DIM:(gi + 1) * GMLP_GROUP_DIM]
            rows = [_dot(ws, vg[c * CHUNK:(c + 1) * CHUNK]) + bs for c in range(GMLP_TILE // CHUNK)]
            parts.append(jnp.concatenate(rows, axis=0))
        mixed = jnp.concatenate(parts, axis=1)
        gated = (u * mixed).astype(BF16)
        acc = acc + _dot(gated, wout_ref[c0:c0 + pair_w, :])
        u_pre = u_next
    o_ref[...] = acc


def _cast_slab(rows, width, steps):
    assert rows % (steps * 16) == 0
    return pl.BlockSpec((rows // steps, width), lambda i: (i, 0))


def _gmlp_mixer(x, g, w_in, b_in, ln_g, ln_b, w_s, b_s, w_out, b_out, moe_w_gate, moe_w_up):
    t = GMLP_TILE
    steps = N_TOK // t
    tok = pl.BlockSpec((t, D_MODEL), lambda i: (i, 0))
    moe_rows = N_EXPERTS * D_MODEL
    slab = _cast_slab(moe_rows, D_FF_EXPERT, steps)
    moe_bf16 = jax.ShapeDtypeStruct((moe_rows, D_FF_EXPERT), BF16)
    return pl.pallas_call(
        _gmlp_kernel,
        grid=(steps,),
        in_specs=[tok, _resident((1, D_MODEL)), _resident((D_MODEL, GMLP_FFN)), _resident((1, GMLP_FFN)),
                  _resident((1, GMLP_HALF)), _resident((1, GMLP_HALF)),
                  _resident((GMLP_GROUPS, CHUNK, CHUNK)), _resident((GMLP_GROUPS, CHUNK, 1)),
                  _resident((GMLP_HALF, D_MODEL)), _resident((1, D_MODEL)), slab, slab],
        out_specs=[tok, slab, slab],
        out_shape=[jax.ShapeDtypeStruct((N_TOK, D_MODEL), F32), moe_bf16, moe_bf16],
        compiler_params=pltpu.CompilerParams(dimension_semantics=("arbitrary",), vmem_limit_bytes=VMEM_LIMIT),
        name="gmlp_mixer",
    )(x, g, w_in, b_in, ln_g, ln_b, w_s, b_s, w_out, b_out,
      moe_w_gate.reshape(moe_rows, D_FF_EXPERT), moe_w_up.reshape(moe_rows, D_FF_EXPERT))


def _rope_slices(t, cos, sin_signed, first_half):
    outs = []
    for j in range(t.shape[1] // LANES):
        s = t[:, j * LANES:(j + 1) * LANES]
        partner = jnp.where(first_half, pltpu.roll(s, LANES - HEAD_DIM // 2, 1), pltpu.roll(s, HEAD_DIM // 2, 1))
        outs.append(s * cos + partner * sin_signed)
    return jnp.concatenate(outs, axis=1)


def _dot_nt(a, b):
    return lax.dot_general(a, b, (((1,), (1,)), ((), ())), preferred_element_type=F32)


def _ffn_ple_qkv_kernel(x_ref, p_ref, cos_ref, sin_ref, cost_ref, sint_ref, fg_ref, wg_ref, wu_ref, wd_ref,
                        wproj_ref, pg_ref, wgate_ref, mg_ref, wqt_ref, bq_ref, kg_ref, wk_ref, bk_ref,
                        wvt_ref, bv_ref, moe_d_ref, x_out, qt_out, k_out, vt_out, moe_db_ref):
    moe_db_ref[...] = moe_d_ref[...].astype(BF16)

    x = x_ref[...]
    hn = _rms(x, fg_ref[...]).astype(BF16)
    a = (_silu(_dot(hn, wg_ref[...])) * _dot(hn, wu_ref[...])).astype(BF16)
    x = x + _dot(a, wd_ref[...])
    gate = _sigmoid(_dot(_rms(x, pg_ref[...]).astype(BF16), wgate_ref[...]))
    x = x + _dot(p_ref[...].astype(BF16), wproj_ref[...]) * gate
    x_out[...] = x

    qt = _dot_nt(wqt_ref[...], _rms(x, mg_ref[...]).astype(BF16)) + bq_ref[...]
    cos_t = cost_ref[...]
    sin_t = sint_ref[...]
    half = HEAD_DIM // 2
    scale = 1.0 / math.sqrt(HEAD_DIM)
    for h in range(N_HEADS):
        t1 = qt[h * HEAD_DIM:h * HEAD_DIM + half]
        t2 = qt[h * HEAD_DIM + half:(h + 1) * HEAD_DIM]
        qt_out[h * HEAD_DIM:h * HEAD_DIM + half, :] = ((t1 * cos_t - t2 * sin_t) * scale).astype(BF16)
        qt_out[h * HEAD_DIM + half:(h + 1) * HEAD_DIM, :] = ((t2 * cos_t + t1 * sin_t) * scale).astype(BF16)

    hkv = _rms(x, kg_ref[...]).astype(BF16)
    lane = lax.broadcasted_iota(jnp.int32, cos_ref.shape, 1)
    first_half = (lane % HEAD_DIM) < half
    k = _rope_slices(_dot(hkv, wk_ref[...]) + bk_ref[...], cos_ref[...], sin_ref[...], first_half)
    k_out[...] = k.astype(BF16)
    vt_out[...] = (_dot_nt(wvt_ref[...], hkv) + bv_ref[...]).astype(BF16)


def _ffn_ple_qkv(x, p0, rope_tabs, fg, wg, wu, wd, wproj, pg, wgate, mg, wqt, bq_col, kg, wk, bk, wvt, bv_col,
                 moe_w_down):
    t = FFN_TILE
    moe_rows = N_EXPERTS * D_FF_EXPERT
    slab = _cast_slab(moe_rows, D_MODEL, N_TOK // t)
    per_seq = SEQ // t
    half = HEAD_DIM // 2
    tok = lambda w: pl.BlockSpec((t, w), lambda i: (i, 0))
    feat = lambda w: pl.BlockSpec((w, t), lambda i: (0, i))
    rope = pl.BlockSpec((t, LANES), lambda i: (i % per_seq, 0))
    rope_t = pl.BlockSpec((half, t), lambda i: (0, i % per_seq))
    cos_t, sin_t, cos_tt, sin_tt = rope_tabs
    return pl.pallas_call(
        _ffn_ple_qkv_kernel,
        grid=(N_TOK // t,),
        in_specs=[tok(D_MODEL), _ple_input(t, 0), rope, rope, rope_t, rope_t,
                  _resident((1, D_MODEL)), _resident((D_MODEL, D_FF_DENSE)), _resident((D_MODEL, D_FF_DENSE)),
                  _resident((D_FF_DENSE, D_MODEL)),
                  _resident((PLE_DIM, D_MODEL)), _resident((1, D_MODEL)), _resident((D_MODEL, D_MODEL)),
                  _resident((1, D_MODEL)), _resident((D_MODEL, D_MODEL)), _resident((D_MODEL, 1)),
                  _resident((1, D_MODEL)), _resident((D_MODEL, KV_WIDTH)), _resident((1, KV_WIDTH)),
                  _resident((KV_WIDTH, D_MODEL)), _resident((KV_WIDTH, 1)), slab],
        out_specs=[tok(D_MODEL), feat(D_MODEL), tok(KV_WIDTH), feat(KV_WIDTH), slab],
        out_shape=[jax.ShapeDtypeStruct((N_TOK, D_MODEL), F32), jax.ShapeDtypeStruct((D_MODEL, N_TOK), BF16),
                   jax.ShapeDtypeStruct((N_TOK, KV_WIDTH), BF16), jax.ShapeDtypeStruct((KV_WIDTH, N_TOK), BF16),
                   jax.ShapeDtypeStruct((moe_rows, D_MODEL), BF16)],
        compiler_params=pltpu.CompilerParams(dimension_semantics=("arbitrary",), vmem_limit_bytes=VMEM_LIMIT),
        name="ffn_ple_qkv",
    )(x, p0, cos_t, sin_t, cos_tt, sin_tt, fg, wg, wu, wd, wproj, pg, wgate, mg, wqt, bq_col, kg, wk, bk,
      wvt, bv_col, moe_w_down.reshape(moe_rows, D_MODEL))


def _attn_kernel(sink_ref, x_ref, qt_ref, kp_ref, kc_ref, vtp_ref, vtc_ref, wo_ref, bo_ref, o_ref, attnt_ref):
    i = pl.program_id(0)
    seq_start = (i % (SEQ // ATTN_TILE)) == 0
    kk = jnp.concatenate([kp_ref[...], kc_ref[...]], axis=0)
    vvt = jnp.concatenate([vtp_ref[...], vtc_ref[...]], axis=1)

    width = KV_REP * WINDOW
    kj = lax.broadcasted_iota(jnp.int32, (2 * WINDOW, width), 0)
    qi = lax.broadcasted_iota(jnp.int32, (2 * WINDOW, width), 1) % WINDOW
    band = (kj > qi) & (kj <= qi + WINDOW)
    first_band = band & ((kj >= WINDOW) | jnp.logical_not(seq_start))
    rep = lax.broadcasted_iota(jnp.int32, (1, width), 1) // WINDOW

    def sinks_of(kh):
        sink = jnp.zeros((1, width), F32)
        for r in range(KV_REP):
            sink = jnp.where(rep == r, sink_ref[kh * KV_REP + r], sink)
        return sink

    def scores(kh, sb):
        feats = slice(kh * HEAD_DIM, (kh + 1) * HEAD_DIM)
        toks = slice(sb * WINDOW, (sb + 1) * WINDOW)
        keys = slice(sb * WINDOW, sb * WINDOW + 2 * WINDOW)
        qt = jnp.concatenate(
            [qt_ref[(kh * KV_REP + r) * HEAD_DIM:(kh * KV_REP + r + 1) * HEAD_DIM, toks] for r in range(KV_REP)],
            axis=1)
        s = _dot(kk[keys, feats], qt)
        return jnp.where(first_band if sb == 0 else band, s, MASK_VALUE)

    def finish(kh, sb, s, sink):
        feats = slice(kh * HEAD_DIM, (kh + 1) * HEAD_DIM)
        toks = slice(sb * WINDOW, (sb + 1) * WINDOW)
        keys = slice(sb * WINDOW, sb * WINDOW + 2 * WINDOW)
        m = jnp.maximum(jnp.max(s, axis=0, keepdims=True), sink)
        pr = jnp.exp(s - m)
        den = jnp.sum(pr, axis=0, keepdims=True) + jnp.exp(sink - m)
        ot = _dot(vvt[feats, keys], pr.astype(BF16)) * (1.0 / den)
        for r in range(KV_REP):
            h = kh * KV_REP + r
            attnt_ref[h * HEAD_DIM:(h + 1) * HEAD_DIM, toks] = ot[:, r * WINDOW:(r + 1) * WINDOW].astype(BF16)

    units = [(kh, sb) for kh in range(N_KV_HEADS) for sb in range(ATTN_TILE // WINDOW)]
    s_cur = scores(*units[0])
    for n, (kh, sb) in enumerate(units):
        s_next = scores(*units[n + 1]) if n + 1 < len(units) else None
        finish(kh, sb, s_cur, sinks_of(kh))
        s_cur = s_next
    attn_out = lax.dot_general(attnt_ref[...], wo_ref[...], (((0,), (0,)), ((), ())), preferred_element_type=F32)
    o_ref[...] = x_ref[...] + attn_out + bo_ref[...]


def _swa_attention(sinks, x, qt, k, vt, wo, bo):
    t = ATTN_TILE
    blocks_per_tile = t // WINDOW
    tok = lambda w: pl.BlockSpec((t, w), lambda i: (i, 0))
    feat = lambda w: pl.BlockSpec((w, t), lambda i: (0, i))
    prev_block = lambda i: jnp.maximum(i * blocks_per_tile - 1, 0)
    k_prev = pl.BlockSpec((WINDOW, KV_WIDTH), lambda i: (prev_block(i), 0))
    vt_prev = pl.BlockSpec((KV_WIDTH, WINDOW), lambda i: (0, prev_block(i)))
    return pl.pallas_call(
        _attn_kernel,
        grid=(N_TOK // t,),
        in_specs=[pl.BlockSpec(memory_space=pltpu.SMEM), tok(D_MODEL), feat(D_MODEL), k_prev, tok(KV_WIDTH),
                  vt_prev, feat(KV_WIDTH), _resident((D_MODEL, D_MODEL)), _resident((1, D_MODEL))],
        out_specs=tok(D_MODEL),
        out_shape=jax.ShapeDtypeStruct((N_TOK, D_MODEL), F32),
        scratch_shapes=[pltpu.VMEM((D_MODEL, t), BF16)],
        compiler_params=pltpu.CompilerParams(dimension_semantics=("arbitrary",), vmem_limit_bytes=VMEM_LIMIT),
        name="swa_attention",
    )(sinks, x, qt, k, k, vt, vt, wo, bo)


def _router_kernel(x_ref, g_ref, wr_ref, hn_ref, idx_ref, wts_ref, rank_ref, cnt_ref):
    t = ROUTER_TILE

    @pl.when(pl.program_id(0) == 0)
    def _():
        cnt_ref[...] = jnp.zeros_like(cnt_ref)

    hn = _rms(x_ref[...], g_ref[...])
    _store_token_tiles(hn_ref, hn)
    hn_hi = hn.astype(BF16)
    hn_lo = (hn - hn_hi.astype(F32)).astype(BF16)
    wr = wr_ref[...]
    wr_hi = wr.astype(BF16)
    wr_lo = (wr - wr_hi.astype(F32)).astype(BF16)
    logits = _dot(hn_hi, wr_hi) + (_dot(hn_hi, wr_lo) + _dot(hn_lo, wr_hi))
    lane = lax.broadcasted_iota(jnp.int32, (t, LANES), 1)
    logits = jnp.where(lane < N_EXPERTS, logits, -jnp.inf)

    m1 = jnp.max(logits, axis=-1, keepdims=True)
    i1 = jnp.min(jnp.where(logits == m1, lane, LANES), axis=-1, keepdims=True)
    rest = jnp.where(lane == i1, -jnp.inf, logits)
    m2 = jnp.max(rest, axis=-1, keepdims=True)
    i2 = jnp.min(jnp.where(rest == m2, lane, LANES), axis=-1, keepdims=True)
    e2 = jnp.exp(m2 - m1)
    w1 = 1.0 / (1.0 + e2)
    w2 = e2 / (1.0 + e2)

    chosen = (lane == i1) | (lane == i2)
    onehot = jnp.where(chosen, 1.0, 0.0).astype(BF16)
    r = lax.broadcasted_iota(jnp.int32, (t, t), 0)
    c = lax.broadcasted_iota(jnp.int32, (t, t), 1)
    before = jnp.where(c < r, 1.0, 0.0).astype(BF16)
    seen = _dot(before, onehot) + cnt_ref[...]
    rank1 = jnp.sum(jnp.where(lane == i1, seen, 0.0), axis=-1, keepdims=True)
    rank2 = jnp.sum(jnp.where(lane == i2, seen, 0.0), axis=-1, keepdims=True)
    cnt_ref[...] = cnt_ref[...] + jnp.sum(onehot.astype(F32), axis=0, keepdims=True)

    idx_ref[...] = jnp.concatenate([i1, i2], axis=1)
    wts_ref[...] = jnp.concatenate([w1, w2], axis=1)
    rank_ref[...] = jnp.concatenate([rank1, rank2], axis=1).astype(jnp.int32)


def _moe_router(x, g, w_router_padded):
    t = ROUTER_TILE
    tok = lambda w: pl.BlockSpec((t, w), lambda i: (i, 0))
    return pl.pallas_call(
        _router_kernel,
        grid=(N_TOK // t,),
        in_specs=[tok(D_MODEL), _resident((1, D_MODEL)), _resident((D_MODEL, LANES))],
        out_specs=[pl.BlockSpec((t * SUBLANES, LANES), lambda i: (i, 0)), tok(2), tok(2), tok(2),
                   pl.BlockSpec((1, LANES), lambda i: (0, 0))],
        out_shape=[jax.ShapeDtypeStruct((N_TOK * SUBLANES, LANES), F32), jax.ShapeDtypeStruct((N_TOK, 2), jnp.int32),
                   jax.ShapeDtypeStruct((N_TOK, 2), F32), jax.ShapeDtypeStruct((N_TOK, 2), jnp.int32),
                   jax.ShapeDtypeStruct((1, LANES), F32)],
        compiler_params=pltpu.CompilerParams(dimension_semantics=("arbitrary",), vmem_limit_bytes=VMEM_LIMIT),
        name="moe_router",
    )(x, g, w_router_padded)


def _dispatch_kernel(pad_start_ref, pad_len_ref, pad_blocks_ref, nv_ref, pos_ref, hn_ref, xs_ref, zero_ref, sem):
    def row_copy(t, k):
        return pltpu.make_async_copy(_token(hn_ref, t), _token(xs_ref, pos_ref[2 * t + k]), sem)

    def issue(t, carry):
        row_copy(t, 0).start(priority=0)
        row_copy(t, 1).start(priority=1)
        return carry

    def drain(t, carry):
        row_copy(t, 0).wait()
        row_copy(t, 1).wait()
        return carry

    lax.fori_loop(0, DISPATCH_TILE, issue, 0, unroll=ROW_COPY_UNROLL)
    lax.fori_loop(0, DISPATCH_TILE, drain, 0, unroll=ROW_COPY_UNROLL)

    @pl.when(pl.program_id(0) == pl.num_programs(0) - 1)
    def _():
        zero_ref[...] = jnp.zeros_like(zero_ref)
        for e in range(N_EXPERTS):
            def zero_row(r, e=e):
                return pltpu.make_async_copy(_token(zero_ref, 0), _token(xs_ref, pad_start_ref[e] + r), sem)

            def row_issue(r, carry, zero_row=zero_row):
                zero_row(r).start()
                return carry

            def row_drain(r, carry, zero_row=zero_row):
                zero_row(r).wait()
                return carry

            lax.fori_loop(0, pad_len_ref[e], row_issue, 0)
            lax.fori_loop(0, pad_len_ref[e], row_drain, 0)

        def zero_block(first_token):
            return pltpu.make_async_copy(zero_ref, _tokens(xs_ref, first_token, EXPERT_SUB), sem)

        def fill_blocks(first_token, n_blocks):
            def block_issue(b, carry):
                zero_block(first_token + b * EXPERT_SUB).start()
                return carry

            def block_drain(b, carry):
                zero_block(first_token + b * EXPERT_SUB).wait()
                return carry

            lax.fori_loop(0, n_blocks, block_issue, 0)
            lax.fori_loop(0, n_blocks, block_drain, 0)

        for e in range(N_EXPERTS):
            fill_blocks(pad_start_ref[e] + pad_len_ref[e], pad_blocks_ref[e])
        fill_blocks(nv_ref[0] * EXPERT_TILE, (N_ROW_TILES - nv_ref[0]) * (EXPERT_TILE // EXPERT_SUB))


def _moe_dispatch(pad_start, pad_len, pad_blocks, n_valid, pos_flat, hn):
    t = DISPATCH_TILE
    grid_spec = pltpu.PrefetchScalarGridSpec(
        num_scalar_prefetch=4,
        grid=(N_TOK // t,),
        in_specs=[pl.BlockSpec((2 * t,), lambda i, ps, pn, pb, nv: (i,), memory_space=pltpu.SMEM),
                  pl.BlockSpec((t * SUBLANES, LANES), lambda i, ps, pn, pb, nv: (i, 0))],
        out_specs=pl.BlockSpec(memory_space=pl.ANY),
        scratch_shapes=[pltpu.VMEM((EXPERT_SUB * SUBLANES, LANES), F32), pltpu.SemaphoreType.DMA(())],
    )
    return pl.pallas_call(
        _dispatch_kernel,
        grid_spec=grid_spec,
        out_shape=jax.ShapeDtypeStruct((SORTED_ROWS * SUBLANES, LANES), F32),
        compiler_params=pltpu.CompilerParams(dimension_semantics=("arbitrary",), has_side_effects=True,
                                             vmem_limit_bytes=VMEM_LIMIT),
        name="moe_dispatch",
    )(pad_start, pad_len, pad_blocks, n_valid, pos_flat, hn)


def _experts_kernel(te_ref, nr_ref, nv_ref, xs_ref, wg_ref, wu_ref, wd_ref, ys_ref, acc_ref):
    del te_ref
    i = pl.program_id(0)
    j = pl.program_id(1)
    last = pl.num_programs(1) - 1
    sub = EXPERT_SUB
    n_rows = nr_ref[i]

    @pl.when((i == 0) & (j == 0))
    def _():
        acc_ref[...] = jnp.zeros_like(acc_ref)

    @pl.when(i < nv_ref[0])
    def _():
        for sb in range(EXPERT_TILE // sub):
            rows = slice(sb * sub, (sb + 1) * sub)
            tiles = lambda s, sb=sb: pl.ds(sb * sub * SUBLANES + s, sub, stride=SUBLANES)

            @pl.when(sb * sub < n_rows)
            def _(rows=rows, tiles=tiles):
                xb = jnp.concatenate([xs_ref[tiles(s), :].astype(BF16) for s in range(SUBLANES)], axis=1)
                a = (_silu(_dot(xb, wg_ref[...])) * _dot(xb, wu_ref[...])).astype(BF16)
                total = _dot(a, wd_ref[...]) + jnp.where(j == 0, 0.0, acc_ref[rows, :])
                acc_ref[rows, :] = total
                for s in range(SUBLANES):
                    ys_ref[tiles(s), :] = total[:, s * LANES:(s + 1) * LANES]

            @pl.when((sb * sub >= n_rows) & (j == last))
            def _(sb=sb):
                ys_ref[sb * sub * SUBLANES:(sb + 1) * sub * SUBLANES, :] = jnp.zeros((sub * SUBLANES, LANES), F32)

    @pl.when((i >= nv_ref[0]) & (j == last))
    def _():
        ys_ref[...] = jnp.zeros_like(ys_ref)


def _moe_experts(tile_expert, tile_rows, n_valid, xs, wg, wu, wd):
    tm, fb = EXPERT_TILE, EXPERT_FF_BLOCK
    n_fb = D_FF_EXPERT // fb
    assert n_fb >= 2
    rows = lambda i, j, te, nr, nv: jnp.maximum(jnp.minimum(i, nv[0] - 1), 0)
    ff = lambda i, j, te, nr, nv: jnp.where(i < nv[0], j, n_fb - 1)
    grid_spec = pltpu.PrefetchScalarGridSpec(
        num_scalar_prefetch=3,
        grid=(N_ROW_TILES, n_fb),
        in_specs=[pl.BlockSpec((tm * SUBLANES, LANES), lambda i, j, te, nr, nv: (rows(i, j, te, nr, nv), 0)),
                  pl.BlockSpec((None, D_MODEL, fb), lambda i, j, te, nr, nv: (te[i], 0, ff(i, j, te, nr, nv))),
                  pl.BlockSpec((None, D_MODEL, fb), lambda i, j, te, nr, nv: (te[i], 0, ff(i, j, te, nr, nv))),
                  pl.BlockSpec((None, fb, D_MODEL), lambda i, j, te, nr, nv: (te[i], ff(i, j, te, nr, nv), 0))],
        out_specs=pl.BlockSpec((tm * SUBLANES, LANES), lambda i, j, te, nr, nv: (i, 0)),
        scratch_shapes=[pltpu.VMEM((tm, D_MODEL), F32)],
    )
    return pl.pallas_call(
        _experts_kernel,
        grid_spec=grid_spec,
        out_shape=jax.ShapeDtypeStruct((SORTED_ROWS * SUBLANES, LANES), F32),
        compiler_params=pltpu.CompilerParams(dimension_semantics=("arbitrary", "arbitrary"),
                                             vmem_limit_bytes=VMEM_LIMIT),
        name="moe_experts",
    )(tile_expert, tile_rows, n_valid, xs, wg, wu, wd)


def _combine_kernel(pos_ref, pos_next_ref, x_ref, wts_ref, p_ref, ys_ref, wproj_ref, pg_ref, wgate_ref, fg_ref,
                    o_ref, rows_ref, sems):
    t = COMBINE_TILE
    i = pl.program_id(0)
    slot = i % 2

    def row_copy(pref, slot, r, k):
        return pltpu.make_async_copy(_token(ys_ref, pref[2 * r + k]), _token(rows_ref.at[slot, k], r),
                                     sems.at[slot])

    def issue(pref, slot):
        def body(r, carry):
            row_copy(pref, slot, r, 0).start(priority=0)
            row_copy(pref, slot, r, 1).start(priority=1)
            return carry
        lax.fori_loop(0, t, body, 0, unroll=ROW_COPY_UNROLL)

    def drain(pref, slot):
        def body(r, carry):
            row_copy(pref, slot, r, 0).wait()
            row_copy(pref, slot, r, 1).wait()
            return carry
        lax.fori_loop(0, t, body, 0, unroll=ROW_COPY_UNROLL)

    @pl.when(i == 0)
    def _():
        issue(pos_ref, 0)

    @pl.when(i + 1 < pl.num_programs(0))
    def _():
        issue(pos_next_ref, 1 - slot)

    drain(pos_ref, slot)

    half = t // 2
    ple = _dot(p_ref[...].astype(BF16), wproj_ref[...])
    xs, gate_in = [], []
    for hf in range(2):
        rows = slice(hf * half, (hf + 1) * half)
        w = wts_ref[rows, :]
        y0 = _load_token_tiles(_tokens(rows_ref.at[slot, 0], hf * half, half), half)
        y1 = _load_token_tiles(_tokens(rows_ref.at[slot, 1], hf * half, half), half)
        x = x_ref[rows, :] + w[:, 0:1] * y0 + w[:, 1:2] * y1
        xs.append(x)
        gate_in.append(_rms(x, pg_ref[...]).astype(BF16))
    for hf in range(2):
        rows = slice(hf * half, (hf + 1) * half)
        gate = _sigmoid(_dot(gate_in[hf], wgate_ref[...]))
        o_ref[rows, :] = _rms(xs[hf] + ple[rows, :] * gate, fg_ref[...])


def _moe_combine_out(pos_flat, x, wts, p1, ys, wproj, pg, wgate, fg):
    t = COMBINE_TILE
    n = N_TOK // t
    tok = lambda w: pl.BlockSpec((t, w), lambda i: (i, 0))
    return pl.pallas_call(
        _combine_kernel,
        grid=(n,),
        in_specs=[pl.BlockSpec((2 * t,), lambda i: (i,), memory_space=pltpu.SMEM),
                  pl.BlockSpec((2 * t,), lambda i: (jnp.minimum(i + 1, n - 1),), memory_space=pltpu.SMEM),
                  tok(D_MODEL), tok(2), _ple_input(t, 1), pl.BlockSpec(memory_space=pl.ANY),
                  _resident((PLE_DIM, D_MODEL)), _resident((1, D_MODEL)), _resident((D_MODEL, D_MODEL)),
                  _resident((1, D_MODEL))],
        out_specs=tok(D_MODEL),
        out_shape=jax.ShapeDtypeStruct((N_TOK, D_MODEL), F32),
        scratch_shapes=[pltpu.VMEM((2, 2, t * SUBLANES, LANES), F32), pltpu.SemaphoreType.DMA((2,))],
        compiler_params=pltpu.CompilerParams(dimension_semantics=("arbitrary",), vmem_limit_bytes=VMEM_LIMIT),
        name="moe_combine_out",
    )(pos_flat, pos_flat, x, wts, p1, ys, wproj, pg, wgate, fg)


def _rope_tables():
    half = HEAD_DIM // 2
    freqs = ROPE_THETA ** (-jnp.arange(0, HEAD_DIM, 2, dtype=F32) / HEAD_DIM)
    ang = jnp.arange(SEQ, dtype=F32)[:, None] * freqs[None, :]
    cos, sin = jnp.cos(ang), jnp.sin(ang)
    reps = LANES // HEAD_DIM
    cos_t = jnp.tile(jnp.concatenate([cos, cos], axis=1), (1, reps))
    sin_t = jnp.tile(jnp.concatenate([-sin, sin], axis=1), (1, reps))
    del half
    return cos_t, sin_t, cos.T, sin.T


def _routing_plan(idx, rank, counts):
    tm = EXPERT_TILE
    tiles = (counts + tm - 1) // tm
    tile_end = jnp.cumsum(tiles)
    group_start = (tile_end - tiles) * tm
    pos = (group_start[idx] + rank).reshape(-1).astype(jnp.int32)
    n_valid = tile_end[-1]
    tile_ids = jnp.minimum(jnp.arange(N_ROW_TILES, dtype=jnp.int32), n_valid - 1)
    tile_expert = jnp.sum(tile_ids[:, None] >= tile_end[None, :], axis=1).astype(jnp.int32)
    tile_first = (tile_end - tiles)[tile_expert]
    tile_rows = jnp.clip(counts[tile_expert] - (tile_ids - tile_first) * tm, 0, tm).astype(jnp.int32)
    sub = EXPERT_SUB
    pad_start = (group_start + counts).astype(jnp.int32)
    pad_len = ((counts + sub - 1) // sub * sub - counts).astype(jnp.int32)
    pad_blocks = ((tiles * tm - counts - pad_len) // sub).astype(jnp.int32)
    return pos, tile_expert, tile_rows, n_valid.reshape(1).astype(jnp.int32), pad_start, pad_len, pad_blocks


def kernel(x, p, mix_norm_g, ffn_norm_g, gmlp_w_in, gmlp_b_in, gmlp_ln_g, gmlp_ln_b, gmlp_w_s, gmlp_b_s, gmlp_w_out, gmlp_b_out, kv_norm_g, w_kv, b_kv, attn_w_q, attn_b_q, attn_sinks, attn_w_o, attn_b_o, ffn_w_gate, ffn_w_up, ffn_w_down, moe_w_router, moe_w_gate, moe_w_up, moe_w_down, ple_w_proj, ple_norm_g, ple_w_gate, final_norm_g):
    row = lambda a: a.reshape(1, -1)
    bf = lambda a: a.astype(BF16)
    xf = x.reshape(N_TOK, D_MODEL)
    pf = p.reshape(2, N_TOK, PLE_DIM)

    x1, moe_gate_bf, moe_up_bf = _gmlp_mixer(
        xf, row(mix_norm_g[0]), bf(gmlp_w_in[0]), row(gmlp_b_in[0]), row(gmlp_ln_g[0]), row(gmlp_ln_b[0]),
        gmlp_w_s[0], gmlp_b_s[0][:, :, None], bf(gmlp_w_out[0]), row(gmlp_b_out[0]), moe_w_gate[0], moe_w_up[0])

    col = lambda a: a.reshape(-1, 1)
    x3, qt, k, vt, moe_down_bf = _ffn_ple_qkv(x1, pf, _rope_tables(), row(ffn_norm_g[0]), bf(ffn_w_gate[0]),
                                 bf(ffn_w_up[0]), bf(ffn_w_down[0]), bf(ple_w_proj[0]), row(ple_norm_g[0]),
                                 bf(ple_w_gate[0]), row(mix_norm_g[1]), bf(attn_w_q[0].T), col(attn_b_q[0]),
                                 row(kv_norm_g), bf(w_kv[:, :KV_WIDTH]), row(b_kv[:KV_WIDTH]),
                                 bf(w_kv[:, KV_WIDTH:].T), col(b_kv[KV_WIDTH:]), moe_w_down[0])

    x4 = _swa_attention(attn_sinks[0], x3, qt, k, vt, bf(attn_w_o[0]), row(attn_b_o[0]))

    w_router = jnp.pad(moe_w_router[0], ((0, 0), (0, LANES - N_EXPERTS)))
    hn, idx, wts, rank, counts = _moe_router(x4, row(ffn_norm_g[1]), w_router)
    pos, tile_expert, tile_rows, n_valid, pad_start, pad_len, pad_blocks = _routing_plan(
        idx, rank, counts[0, :N_EXPERTS].astype(jnp.int32))

    xs = _moe_dispatch(pad_start, pad_len, pad_blocks, n_valid, pos, hn)
    expert_w = lambda a, k, n: a.reshape(N_EXPERTS, k, n)
    ys = _moe_experts(tile_expert, tile_rows, n_valid, xs, expert_w(moe_gate_bf, D_MODEL, D_FF_EXPERT),
                      expert_w(moe_up_bf, D_MODEL, D_FF_EXPERT), expert_w(moe_down_bf, D_FF_EXPERT, D_MODEL))
    out = _moe_combine_out(pos, x4, wts, pf, ys, bf(ple_w_proj[1]), row(ple_norm_g[1]), bf(ple_w_gate[1]),
                           row(final_norm_g))
    return out.reshape(BATCH, SEQ, D_MODEL)
```

```python
import functools
import math

import jax
import jax.numpy as jnp
from jax import lax
from jax.experimental import pallas as pl
from jax.experimental.pallas import tpu as pltpu

F32 = jnp.float32
BF16 = jnp.bfloat16

D_MODEL = 1024
BATCH = 4
SEQ = 4096
N_TOK = BATCH * SEQ

CHUNK = 128
GMLP_FFN = 6 * D_MODEL
GMLP_HALF = GMLP_FFN // 2
GMLP_GROUPS = 8
GMLP_GROUP_DIM = GMLP_HALF // GMLP_GROUPS

N_HEADS = 16
N_KV_HEADS = 4
HEAD_DIM = 64
KV_REP = N_HEADS // N_KV_HEADS
KV_WIDTH = N_KV_HEADS * HEAD_DIM
WINDOW = 128
ROPE_THETA = 10000.0

D_FF_DENSE = 2816
N_EXPERTS = 8
D_FF_EXPERT = 3584
PLE_DIM = 256

EPS = 1e-6
MASK_VALUE = -1e30

LANES = 128
SUBLANES = 8
assert D_MODEL == SUBLANES * LANES

GMLP_TILE = 256
FFN_TILE = 256
ATTN_TILE = 1024
ROUTER_TILE = 512
DISPATCH_TILE = 1024
ROW_COPY_UNROLL = 8
EXPERT_TILE = 1024
EXPERT_SUB = 256
EXPERT_FF_BLOCK = 1792
COMBINE_TILE = 512

SORTED_ROWS = 2 * N_TOK + N_EXPERTS * EXPERT_TILE
N_ROW_TILES = SORTED_ROWS // EXPERT_TILE

VMEM_LIMIT = 56 * 1024 * 1024


def _ple_input(t, layer):
    return pl.BlockSpec((None, t, PLE_DIM), lambda i: (layer, i, 0))


def _resident(shape):
    zeros = (0,) * len(shape)
    return pl.BlockSpec(shape, lambda *_: zeros, pipeline_mode=pl.Buffered(1))


def _rms(x, g):
    return x * lax.rsqrt(jnp.mean(x * x, axis=-1, keepdims=True) + EPS) * g


def _gelu_tanh(x):
    c = math.sqrt(2.0 / math.pi)
    return 0.5 * x * (1.0 + jnp.tanh(c * (x + 0.044715 * (x * x * x))))


def _silu(x):
    return x * (1.0 / (1.0 + jnp.exp(-x)))


def _sigmoid(x):
    return 1.0 / (1.0 + jnp.exp(-x))


def _dot(a, b):
    return jnp.dot(a, b, preferred_element_type=F32)


def _store_token_tiles(ref, x):
    t = x.shape[0]
    for s in range(SUBLANES):
        ref[pl.ds(s, t, stride=SUBLANES), :] = x[:, s * LANES:(s + 1) * LANES]


def _tokens(ref, first, n):
    return ref.at[pl.ds(first * SUBLANES, n * SUBLANES)]


def _token(ref, r):
    return _tokens(ref, r, 1)


def _load_token_tiles(ref, t):
    return jnp.concatenate([ref[pl.ds(s, t, stride=SUBLANES), :] for s in range(SUBLANES)], axis=1)


def _gmlp_kernel(x_ref, g_ref, win_ref, bin_ref, lng_ref, lnb_ref, ws_ref, bs_ref,
                 wout_ref, bout_ref, moe_g_ref, moe_u_ref, o_ref, moe_gb_ref, moe_ub_ref):
    moe_gb_ref[...] = moe_g_ref[...].astype(BF16)
    moe_ub_ref[...] = moe_u_ref[...].astype(BF16)

    x = x_ref[...]
    h = _rms(x, g_ref[...]).astype(BF16)
    pair_w = 2 * GMLP_GROUP_DIM
    n_pairs = GMLP_GROUPS // 2

    def u_product(pair):
        c0 = pair * pair_w
        return _dot(h, win_ref[:, c0:c0 + pair_w]) + bin_ref[:, c0:c0 + pair_w]

    v_pre = _dot(h, win_ref[:, GMLP_HALF:]) + bin_ref[:, GMLP_HALF:]
    u_pre = u_product(0)
    v = _gelu_tanh(v_pre)
    mu = jnp.mean(v, axis=-1, keepdims=True)
    vc = v - mu
    var = jnp.mean(vc * vc, axis=-1, keepdims=True)
    vn = (vc * lax.rsqrt(var + EPS) * lng_ref[...] + lnb_ref[...]).astype(BF16)

    row = lax.broadcasted_iota(jnp.int32, (CHUNK, CHUNK), 0)
    col = lax.broadcasted_iota(jnp.int32, (CHUNK, CHUNK), 1)
    causal = col <= row

    acc = x + bout_ref[...]
    for pair in range(n_pairs):
        c0 = pair * pair_w
        u_next = u_product(pair + 1) if pair + 1 < n_pairs else None
        u = _gelu_tanh(u_pre)
        parts = []
        for gi in range(2 * pair, 2 * pair + 2):
            ws = jnp.where(causal, ws_ref[gi], 0.0).astype(BF16)
            bs = bs_ref[gi]
            vg = vn[:, gi * GMLP_GROUP_DIM:(gi + 1) * GMLP_GROUP_DIM]
            rows = [_dot(ws, vg[c * CHUNK:(c + 1) * CHUNK]) + bs for c in range(GMLP_TILE // CHUNK)]
            parts.append(jnp.concatenate(rows, axis=0))
        mixed = jnp.concatenate(parts, axis=1)
        gated = (u * mixed).astype(BF16)
        acc = acc + _dot(gated, wout_ref[c0:c0 + pair_w, :])
        u_pre = u_next
    o_ref[...] = acc


def _cast_slab(rows, width, steps):
    assert rows % (steps * 16) == 0
    return pl.BlockSpec((rows // steps, width), lambda i: (i, 0))


def _gmlp_mixer(x, g, w_in, b_in, ln_g, ln_b, w_s, b_s, w_out, b_out, moe_w_gate, moe_w_up):
    t = GMLP_TILE
    steps = N_TOK // t
    tok = pl.BlockSpec((t, D_MODEL), lambda i: (i, 0))
    moe_rows = N_EXPERTS * D_MODEL
    slab = _cast_slab(moe_rows, D_FF_EXPERT, steps)
    moe_bf16 = jax.ShapeDtypeStruct((moe_rows, D_FF_EXPERT), BF16)
    return pl.pallas_call(
        _gmlp_kernel,
        grid=(steps,),
        in_specs=[tok, _resident((1, D_MODEL)), _resident((D_MODEL, GMLP_FFN)), _resident((1, GMLP_FFN)),
                  _resident((1, GMLP_HALF)), _resident((1, GMLP_HALF)),
                  _resident((GMLP_GROUPS, CHUNK, CHUNK)), _resident((GMLP_GROUPS, CHUNK, 1)),
                  _resident((GMLP_HALF, D_MODEL)), _resident((1, D_MODEL)), slab, slab],
        out_specs=[tok, slab, slab],
        out_shape=[jax.ShapeDtypeStruct((N_TOK, D_MODEL), F32), moe_bf16, moe_bf16],
        compiler_params=pltpu.CompilerParams(dimension_semantics=("arbitrary",), vmem_limit_bytes=VMEM_LIMIT),
        name="gmlp_mixer",
    )(x, g, w_in, b_in, ln_g, ln_b, w_s, b_s, w_out, b_out,
      moe_w_gate.reshape(moe_rows, D_FF_EXPERT), moe_w_up.reshape(moe_rows, D_FF_EXPERT))


def _rope_slices(t, cos, sin_signed, first_half):
    outs = []
    for j in range(t.shape[1] // LANES):
        s = t[:, j * LANES:(j + 1) * LANES]
        partner = jnp.where(first_half, pltpu.roll(s, LANES - HEAD_DIM // 2, 1), pltpu.roll(s, HEAD_DIM // 2, 1))
        outs.append(s * cos + partner * sin_signed)
    return jnp.concatenate(outs, axis=1)


def _dot_nt(a, b):
    return lax.dot_general(a, b, (((1,), (1,)), ((), ())), preferred_element_type=F32)


def _ffn_ple_qkv_kernel(x_ref, p_ref, cos_ref, sin_ref, cost_ref, sint_ref, fg_ref, wg_ref, wu_ref, wd_ref,
                        wproj_ref, pg_ref, wgate_ref, mg_ref, wqt_ref, bq_ref, kg_ref, wk_ref, bk_ref,
                        wvt_ref, bv_ref, moe_d_ref, x_out, qt_out, k_out, vt_out, moe_db_ref):
    moe_db_ref[...] = moe_d_ref[...].astype(BF16)

    x = x_ref[...]
    hn = _rms(x, fg_ref[...]).astype(BF16)
    a = (_silu(_dot(hn, wg_ref[...])) * _dot(hn, wu_ref[...])).astype(BF16)
    x = x + _dot(a, wd_ref[...])
    gate = _sigmoid(_dot(_rms(x, pg_ref[...]).astype(BF16), wgate_ref[...]))
    x = x + _dot(p_ref[...].astype(BF16), wproj_ref[...]) * gate
    x_out[...] = x

    qt = _dot_nt(wqt_ref[...], _rms(x, mg_ref[...]).astype(BF16)) + bq_ref[...]
    cos_t = cost_ref[...]
    sin_t = sint_ref[...]
    half = HEAD_DIM // 2
    scale = 1.0 / math.sqrt(HEAD_DIM)
    for h in range(N_HEADS):
        t1 = qt[h * HEAD_DIM:h * HEAD_DIM + half]
        t2 = qt[h * HEAD_DIM + half:(h + 1) * HEAD_DIM]
        qt_out[h * HEAD_DIM:h * HEAD_DIM + half, :] = ((t1 * cos_t - t2 * sin_t) * scale).astype(BF16)
        qt_out[h * HEAD_DIM + half:(h + 1) * HEAD_DIM, :] = ((t2 * cos_t + t1 * sin_t) * scale).astype(BF16)

    hkv = _rms(x, kg_ref[...]).astype(BF16)
    lane = lax.broadcasted_iota(jnp.int32, cos_ref.shape, 1)
    first_half = (lane % HEAD_DIM) < half
    k = _rope_slices(_dot(hkv, wk_ref[...]) + bk_ref[...], cos_ref[...], sin_ref[...], first_half)
    k_out[...] = k.astype(BF16)
    vt_out[...] = (_dot_nt(wvt_ref[...], hkv) + bv_ref[...]).astype(BF16)


def _ffn_ple_qkv(x, p0, rope_tabs, fg, wg, wu, wd, wproj, pg, wgate, mg, wqt, bq_col, kg, wk, bk, wvt, bv_col,
                 moe_w_down):
    t = FFN_TILE
    moe_rows = N_EXPERTS * D_FF_EXPERT
    slab = _cast_slab(moe_rows, D_MODEL, N_TOK // t)
    per_seq = SEQ // t
    half = HEAD_DIM // 2
    tok = lambda w: pl.BlockSpec((t, w), lambda i: (i, 0))
    feat = lambda w: pl.BlockSpec((w, t), lambda i: (0, i))
    rope = pl.BlockSpec((t, LANES), lambda i: (i % per_seq, 0))
    rope_t = pl.BlockSpec((half, t), lambda i: (0, i % per_seq))
    cos_t, sin_t, cos_tt, sin_tt = rope_tabs
    return pl.pallas_call(
        _ffn_ple_qkv_kernel,
        grid=(N_TOK // t,),
        in_specs=[tok(D_MODEL), _ple_input(t, 0), rope, rope, rope_t, rope_t,
                  _resident((1, D_MODEL)), _resident((D_MODEL, D_FF_DENSE)), _resident((D_MODEL, D_FF_DENSE)),
                  _resident((D_FF_DENSE, D_MODEL)),
                  _resident((PLE_DIM, D_MODEL)), _resident((1, D_MODEL)), _resident((D_MODEL, D_MODEL)),
                  _resident((1, D_MODEL)), _resident((D_MODEL, D_MODEL)), _resident((D_MODEL, 1)),
                  _resident((1, D_MODEL)), _resident((D_MODEL, KV_WIDTH)), _resident((1, KV_WIDTH)),
                  _resident((KV_WIDTH, D_MODEL)), _resident((KV_WIDTH, 1)), slab],
        out_specs=[tok(D_MODEL), feat(D_MODEL), tok(KV_WIDTH), feat(KV_WIDTH), slab],
        out_shape=[jax.ShapeDtypeStruct((N_TOK, D_MODEL), F32), jax.ShapeDtypeStruct((D_MODEL, N_TOK), BF16),
                   jax.ShapeDtypeStruct((N_TOK, KV_WIDTH), BF16), jax.ShapeDtypeStruct((KV_WIDTH, N_TOK), BF16),
                   jax.ShapeDtypeStruct((moe_rows, D_MODEL), BF16)],
        compiler_params=pltpu.CompilerParams(dimension_semantics=("arbitrary",), vmem_limit_bytes=VMEM_LIMIT),
        name="ffn_ple_qkv",
    )(x, p0, cos_t, sin_t, cos_tt, sin_tt, fg, wg, wu, wd, wproj, pg, wgate, mg, wqt, bq_col, kg, wk, bk,
      wvt, bv_col, moe_w_down.reshape(moe_rows, D_MODEL))


def _attn_kernel(sink_ref, x_ref, qt_ref, kp_ref, kc_ref, vtp_ref, vtc_ref, wo_ref, bo_ref, o_ref, attnt_ref):
    i = pl.program_id(0)
    seq_start = (i % (SEQ // ATTN_TILE)) == 0
    kk = jnp.concatenate([kp_ref[...], kc_ref[...]], axis=0)
    vvt = jnp.concatenate([vtp_ref[...], vtc_ref[...]], axis=1)

    width = KV_REP * WINDOW
    kj = lax.broadcasted_iota(jnp.int32, (2 * WINDOW, width), 0)
    qi = lax.broadcasted_iota(jnp.int32, (2 * WINDOW, width), 1) % WINDOW
    band = (kj > qi) & (kj <= qi + WINDOW)
    first_band = band & ((kj >= WINDOW) | jnp.logical_not(seq_start))
    rep = lax.broadcasted_iota(jnp.int32, (1, width), 1) // WINDOW

    def sinks_of(kh):
        sink = jnp.zeros((1, width), F32)
        for r in range(KV_REP):
            sink = jnp.where(rep == r, sink_ref[kh * KV_REP + r], sink)
        return sink

    def scores(kh, sb):
        feats = slice(kh * HEAD_DIM, (kh + 1) * HEAD_DIM)
        toks = slice(sb * WINDOW, (sb + 1) * WINDOW)
        keys = slice(sb * WINDOW, sb * WINDOW + 2 * WINDOW)
        qt = jnp.concatenate(
            [qt_ref[(kh * KV_REP + r) * HEAD_DIM:(kh * KV_REP + r + 1) * HEAD_DIM, toks] for r in range(KV_REP)],
            axis=1)
        s = _dot(kk[keys, feats], qt)
        return jnp.where(first_band if sb == 0 else band, s, MASK_VALUE)

    def finish(kh, sb, s, sink):
        feats = slice(kh * HEAD_DIM, (kh + 1) * HEAD_DIM)
        toks = slice(sb * WINDOW, (sb + 1) * WINDOW)
        keys = slice(sb * WINDOW, sb * WINDOW + 2 * WINDOW)
        m = jnp.maximum(jnp.max(s, axis=0, keepdims=True), sink)
        pr = jnp.exp(s - m)
        den = jnp.sum(pr, axis=0, keepdims=True) + jnp.exp(sink - m)
        ot = _dot(vvt[feats, keys], pr.astype(BF16)) * (1.0 / den)
        for r in range(KV_REP):
            h = kh * KV_REP + r
            attnt_ref[h * HEAD_DIM:(h + 1) * HEAD_DIM, toks] = ot[:, r * WINDOW:(r + 1) * WINDOW].astype(BF16)

    units = [(kh, sb) for kh in range(N_KV_HEADS) for sb in range(ATTN_TILE // WINDOW)]
    s_cur = scores(*units[0])
    for n, (kh, sb) in enumerate(units):
        s_next = scores(*units[n + 1]) if n + 1 < len(units) else None
        finish(kh, sb, s_cur, sinks_of(kh))
        s_cur = s_next
    attn_out = lax.dot_general(attnt_ref[...], wo_ref[...], (((0,), (0,)), ((), ())), preferred_element_type=F32)
    o_ref[...] = x_ref[...] + attn_out + bo_ref[...]


def _swa_attention(sinks, x, qt, k, vt, wo, bo):
    t = ATTN_TILE
    blocks_per_tile = t // WINDOW
    tok = lambda w: pl.BlockSpec((t, w), lambda i: (i, 0))
    feat = lambda w: pl.BlockSpec((w, t), lambda i: (0, i))
    prev_block = lambda i: jnp.maximum(i * blocks_per_tile - 1, 0)
    k_prev = pl.BlockSpec((WINDOW, KV_WIDTH), lambda i: (prev_block(i), 0))
    vt_prev = pl.BlockSpec((KV_WIDTH, WINDOW), lambda i: (0, prev_block(i)))
    return pl.pallas_call(
        _attn_kernel,
        grid=(N_TOK // t,),
        in_specs=[pl.BlockSpec(memory_space=pltpu.SMEM), tok(D_MODEL), feat(D_MODEL), k_prev, tok(KV_WIDTH),
                  vt_prev, feat(KV_WIDTH), _resident((D_MODEL, D_MODEL)), _resident((1, D_MODEL))],
        out_specs=tok(D_MODEL),
        out_shape=jax.ShapeDtypeStruct((N_TOK, D_MODEL), F32),
        scratch_shapes=[pltpu.VMEM((D_MODEL, t), BF16)],
        compiler_params=pltpu.CompilerParams(dimension_semantics=("arbitrary",), vmem_limit_bytes=VMEM_LIMIT),
        name="swa_attention",
    )(sinks, x, qt, k, k, vt, vt, wo, bo)


def _router_kernel(x_ref, g_ref, wr_ref, hn_ref, idx_ref, wts_ref, rank_ref, cnt_ref):
    t = ROUTER_TILE

    @pl.when(pl.program_id(0) == 0)
    def _():
        cnt_ref[...] = jnp.zeros_like(cnt_ref)

    hn = _rms(x_ref[...], g_ref[...])
    _store_token_tiles(hn_ref, hn)
    hn_hi = hn.astype(BF16)
    hn_lo = (hn - hn_hi.astype(F32)).astype(BF16)
    wr = wr_ref[...]
    wr_hi = wr.astype(BF16)
    wr_lo = (wr - wr_hi.astype(F32)).astype(BF16)
    logits = _dot(hn_hi, wr_hi) + (_dot(hn_hi, wr_lo) + _dot(hn_lo, wr_hi))
    lane = lax.broadcasted_iota(jnp.int32, (t, LANES), 1)
    logits = jnp.where(lane < N_EXPERTS, logits, -jnp.inf)

    m1 = jnp.max(logits, axis=-1, keepdims=True)
    i1 = jnp.min(jnp.where(logits == m1, lane, LANES), axis=-1, keepdims=True)
    rest = jnp.where(lane == i1, -jnp.inf, logits)
    m2 = jnp.max(rest, axis=-1, keepdims=True)
    i2 = jnp.min(jnp.where(rest == m2, lane, LANES), axis=-1, keepdims=True)
    e2 = jnp.exp(m2 - m1)
    w1 = 1.0 / (1.0 + e2)
    w2 = e2 / (1.0 + e2)

    chosen = (lane == i1) | (lane == i2)
    onehot = jnp.where(chosen, 1.0, 0.0).astype(BF16)
    r = lax.broadcasted_iota(jnp.int32, (t, t), 0)
    c = lax.broadcasted_iota(jnp.int32, (t, t), 1)
    before = jnp.where(c < r, 1.0, 0.0).astype(BF16)
    seen = _dot(before, onehot) + cnt_ref[...]
    rank1 = jnp.sum(jnp.where(lane == i1, seen, 0.0), axis=-1, keepdims=True)
    rank2 = jnp.sum(jnp.where(lane == i2, seen, 0.0), axis=-1, keepdims=True)
    cnt_ref[...] = cnt_ref[...] + jnp.sum(onehot.astype(F32), axis=0, keepdims=True)

    idx_ref[...] = jnp.concatenate([i1, i2], axis=1)
    wts_ref[...] = jnp.concatenate([w1, w2], axis=1)
    rank_ref[...] = jnp.concatenate([rank1, rank2], axis=1).astype(jnp.int32)


def _moe_router(x, g, w_router_padded):
    t = ROUTER_TILE
    tok = lambda w: pl.BlockSpec((t, w), lambda i: (i, 0))
    return pl.pallas_call(
        _router_kernel,
        grid=(N_TOK // t,),
        in_specs=[tok(D_MODEL), _resident((1, D_MODEL)), _resident((D_MODEL, LANES))],
        out_specs=[pl.BlockSpec((t * SUBLANES, LANES), lambda i: (i, 0)), tok(2), tok(2), tok(2),
                   pl.BlockSpec((1, LANES), lambda i: (0, 0))],
        out_shape=[jax.ShapeDtypeStruct((N_TOK * SUBLANES, LANES), F32), jax.ShapeDtypeStruct((N_TOK, 2), jnp.int32),
                   jax.ShapeDtypeStruct((N_TOK, 2), F32), jax.ShapeDtypeStruct((N_TOK, 2), jnp.int32),
                   jax.ShapeDtypeStruct((1, LANES), F32)],
        compiler_params=pltpu.CompilerParams(dimension_semantics=("arbitrary",), vmem_limit_bytes=VMEM_LIMIT),
        name="moe_router",
    )(x, g, w_router_padded)


def _dispatch_kernel(pad_start_ref, pad_len_ref, pad_blocks_ref, nv_ref, pos_ref, hn_ref, xs_ref, zero_ref, sem):
    def row_copy(t, k):
        return pltpu.make_async_copy(_token(hn_ref, t), _token(xs_ref, pos_ref[2 * t + k]), sem)

    def issue(t, carry):
        row_copy(t, 0).start(priority=0)
        row_copy(t, 1).start(priority=1)
        return carry

    def drain(t, carry):
        row_copy(t, 0).wait()
        row_copy(t, 1).wait()
        return carry

    lax.fori_loop(0, DISPATCH_TILE, issue, 0, unroll=ROW_COPY_UNROLL)
    lax.fori_loop(0, DISPATCH_TILE, drain, 0, unroll=ROW_COPY_UNROLL)

    @pl.when(pl.program_id(0) == pl.num_programs(0) - 1)
    def _():
        zero_ref[...] = jnp.zeros_like(zero_ref)
        for e in range(N_EXPERTS):
            def zero_row(r, e=e):
                return pltpu.make_async_copy(_token(zero_ref, 0), _token(xs_ref, pad_start_ref[e] + r), sem)

            def row_issue(r, carry, zero_row=zero_row):
                zero_row(r).start()
                return carry

            def row_drain(r, carry, zero_row=zero_row):
                zero_row(r).wait()
                return carry

            lax.fori_loop(0, pad_len_ref[e], row_issue, 0)
            lax.fori_loop(0, pad_len_ref[e], row_drain, 0)

        def zero_block(first_token):
            return pltpu.make_async_copy(zero_ref, _tokens(xs_ref, first_token, EXPERT_SUB), sem)

        def fill_blocks(first_token, n_blocks):
            def block_issue(b, carry):
                zero_block(first_token + b * EXPERT_SUB).start()
                return carry

            def block_drain(b, carry):
                zero_block(first_token + b * EXPERT_SUB).wait()
                return carry

            lax.fori_loop(0, n_blocks, block_issue, 0)
            lax.fori_loop(0, n_blocks, block_drain, 0)

        for e in range(N_EXPERTS):
            fill_blocks(pad_start_ref[e] + pad_len_ref[e], pad_blocks_ref[e])
        fill_blocks(nv_ref[0] * EXPERT_TILE, (N_ROW_TILES - nv_ref[0]) * (EXPERT_TILE // EXPERT_SUB))


def _moe_dispatch(pad_start, pad_len, pad_blocks, n_valid, pos_flat, hn):
    t = DISPATCH_TILE
    grid_spec = pltpu.PrefetchScalarGridSpec(
        num_scalar_prefetch=4,
        grid=(N_TOK // t,),
        in_specs=[pl.BlockSpec((2 * t,), lambda i, ps, pn, pb, nv: (i,), memory_space=pltpu.SMEM),
                  pl.BlockSpec((t * SUBLANES, LANES), lambda i, ps, pn, pb, nv: (i, 0))],
        out_specs=pl.BlockSpec(memory_space=pl.ANY),
        scratch_shapes=[pltpu.VMEM((EXPERT_SUB * SUBLANES, LANES), F32), pltpu.SemaphoreType.DMA(())],
    )
    return pl.pallas_call(
        _dispatch_kernel,
        grid_spec=grid_spec,
        out_shape=jax.ShapeDtypeStruct((SORTED_ROWS * SUBLANES, LANES), F32),
        compiler_params=pltpu.CompilerParams(dimension_semantics=("arbitrary",), has_side_effects=True,
                                             vmem_limit_bytes=VMEM_LIMIT),
        name="moe_dispatch",
    )(pad_start, pad_len, pad_blocks, n_valid, pos_flat, hn)


def _experts_kernel(te_ref, nr_ref, nv_ref, xs_ref, wg_ref, wu_ref, wd_ref, ys_ref, acc_ref):
    del te_ref
    i = pl.program_id(0)
    j = pl.program_id(1)
    last = pl.num_programs(1) - 1
    sub = EXPERT_SUB
    n_rows = nr_ref[i]

    @pl.when((i == 0) & (j == 0))
    def _():
        acc_ref[...] = jnp.zeros_like(acc_ref)

    def sub_block(first_row, n):
        rows = slice(first_row, first_row + n)
        tiles = lambda s: pl.ds(first_row * SUBLANES + s, n, stride=SUBLANES)
        xb = jnp.concatenate([xs_ref[tiles(s), :].astype(BF16) for s in range(SUBLANES)], axis=1)
        a = (_silu(_dot(xb, wg_ref[...])) * _dot(xb, wu_ref[...])).astype(BF16)
        total = _dot(a, wd_ref[...]) + jnp.where(j == 0, 0.0, acc_ref[rows, :])
        acc_ref[rows, :] = total
        for s in range(SUBLANES):
            ys_ref[tiles(s), :] = total[:, s * LANES:(s + 1) * LANES]

    def zero_rows(first_row, n):
        ys_ref[first_row * SUBLANES:(first_row + n) * SUBLANES, :] = jnp.zeros((n * SUBLANES, LANES), F32)

    @pl.when(i < nv_ref[0])
    def _():
        half = sub // 2
        for sb in range(EXPERT_TILE // sub):
            left = n_rows - sb * sub

            @pl.when(left > half)
            def _(sb=sb):
                sub_block(sb * sub, sub)

            @pl.when((left > 0) & (left <= half))
            def _(sb=sb):
                sub_block(sb * sub, half)

                @pl.when(j == last)
                def _():
                    zero_rows(sb * sub + half, half)

            @pl.when((left <= 0) & (j == last))
            def _(sb=sb):
                zero_rows(sb * sub, sub)

    @pl.when((i >= nv_ref[0]) & (j == last))
    def _():
        ys_ref[...] = jnp.zeros_like(ys_ref)


def _moe_experts(tile_expert, tile_rows, n_valid, xs, wg, wu, wd):
    tm, fb = EXPERT_TILE, EXPERT_FF_BLOCK
    n_fb = D_FF_EXPERT // fb
    assert n_fb >= 2
    rows = lambda i, j, te, nr, nv: jnp.maximum(jnp.minimum(i, nv[0] - 1), 0)
    ff = lambda i, j, te, nr, nv: jnp.where(i < nv[0], j, n_fb - 1)
    grid_spec = pltpu.PrefetchScalarGridSpec(
        num_scalar_prefetch=3,
        grid=(N_ROW_TILES, n_fb),
        in_specs=[pl.BlockSpec((tm * SUBLANES, LANES), lambda i, j, te, nr, nv: (rows(i, j, te, nr, nv), 0)),
                  pl.BlockSpec((None, D_MODEL, fb), lambda i, j, te, nr, nv: (te[i], 0, ff(i, j, te, nr, nv))),
                  pl.BlockSpec((None, D_MODEL, fb), lambda i, j, te, nr, nv: (te[i], 0, ff(i, j, te, nr, nv))),
                  pl.BlockSpec((None, fb, D_MODEL), lambda i, j, te, nr, nv: (te[i], ff(i, j, te, nr, nv), 0))],
        out_specs=pl.BlockSpec((tm * SUBLANES, LANES), lambda i, j, te, nr, nv: (i, 0)),
        scratch_shapes=[pltpu.VMEM((tm, D_MODEL), F32)],
    )
    return pl.pallas_call(
        _experts_kernel,
        grid_spec=grid_spec,
        out_shape=jax.ShapeDtypeStruct((SORTED_ROWS * SUBLANES, LANES), F32),
        compiler_params=pltpu.CompilerParams(dimension_semantics=("arbitrary", "arbitrary"),
                                             vmem_limit_bytes=VMEM_LIMIT),
        name="moe_experts",
    )(tile_expert, tile_rows, n_valid, xs, wg, wu, wd)


def _combine_kernel(pos_ref, pos_next_ref, x_ref, wts_ref, p_ref, ys_ref, wproj_ref, pg_ref, wgate_ref, fg_ref,
                    o_ref, rows_ref, sems):
    t = COMBINE_TILE
    i = pl.program_id(0)
    slot = i % 2

    def row_copy(pref, slot, r, k):
        return pltpu.make_async_copy(_token(ys_ref, pref[2 * r + k]), _token(rows_ref.at[slot, k], r),
                                     sems.at[slot])

    def issue(pref, slot):
        def body(r, carry):
            row_copy(pref, slot, r, 0).start(priority=0)
            row_copy(pref, slot, r, 1).start(priority=1)
            return carry
        lax.fori_loop(0, t, body, 0, unroll=ROW_COPY_UNROLL)

    def drain(pref, slot):
        def body(r, carry):
            row_copy(pref, slot, r, 0).wait()
            row_copy(pref, slot, r, 1).wait()
            return carry
        lax.fori_loop(0, t, body, 0, unroll=ROW_COPY_UNROLL)

    @pl.when(i == 0)
    def _():
        issue(pos_ref, 0)

    @pl.when(i + 1 < pl.num_programs(0))
    def _():
        issue(pos_next_ref, 1 - slot)

    drain(pos_ref, slot)

    half = t // 2
    ple = _dot(p_ref[...].astype(BF16), wproj_ref[...])
    xs, gate_in = [], []
    for hf in range(2):
        rows = slice(hf * half, (hf + 1) * half)
        w = wts_ref[rows, :]
        y0 = _load_token_tiles(_tokens(rows_ref.at[slot, 0], hf * half, half), half)
        y1 = _load_token_tiles(_tokens(rows_ref.at[slot, 1], hf * half, half), half)
        x = x_ref[rows, :] + w[:, 0:1] * y0 + w[:, 1:2] * y1
        xs.append(x)
        gate_in.append(_rms(x, pg_ref[...]).astype(BF16))
    for hf in range(2):
        rows = slice(hf * half, (hf + 1) * half)
        gate = _sigmoid(_dot(gate_in[hf], wgate_ref[...]))
        o_ref[rows, :] = _rms(xs[hf] + ple[rows, :] * gate, fg_ref[...])


def _moe_combine_out(pos_flat, x, wts, p1, ys, wproj, pg, wgate, fg):
    t = COMBINE_TILE
    n = N_TOK // t
    tok = lambda w: pl.BlockSpec((t, w), lambda i: (i, 0))
    return pl.pallas_call(
        _combine_kernel,
        grid=(n,),
        in_specs=[pl.BlockSpec((2 * t,), lambda i: (i,), memory_space=pltpu.SMEM),
                  pl.BlockSpec((2 * t,), lambda i: (jnp.minimum(i + 1, n - 1),), memory_space=pltpu.SMEM),
                  tok(D_MODEL), tok(2), _ple_input(t, 1), pl.BlockSpec(memory_space=pl.ANY),
                  _resident((PLE_DIM, D_MODEL)), _resident((1, D_MODEL)), _resident((D_MODEL, D_MODEL)),
                  _resident((1, D_MODEL))],
        out_specs=tok(D_MODEL),
        out_shape=jax.ShapeDtypeStruct((N_TOK, D_MODEL), F32),
        scratch_shapes=[pltpu.VMEM((2, 2, t * SUBLANES, LANES), F32), pltpu.SemaphoreType.DMA((2,))],
        compiler_params=pltpu.CompilerParams(dimension_semantics=("arbitrary",), vmem_limit_bytes=VMEM_LIMIT),
        name="moe_combine_out",
    )(pos_flat, pos_flat, x, wts, p1, ys, wproj, pg, wgate, fg)


def _rope_tables():
    half = HEAD_DIM // 2
    freqs = ROPE_THETA ** (-jnp.arange(0, HEAD_DIM, 2, dtype=F32) / HEAD_DIM)
    ang = jnp.arange(SEQ, dtype=F32)[:, None] * freqs[None, :]
    cos, sin = jnp.cos(ang), jnp.sin(ang)
    reps = LANES // HEAD_DIM
    cos_t = jnp.tile(jnp.concatenate([cos, cos], axis=1), (1, reps))
    sin_t = jnp.tile(jnp.concatenate([-sin, sin], axis=1), (1, reps))
    del half
    return cos_t, sin_t, cos.T, sin.T


def _routing_plan(idx, rank, counts):
    tm = EXPERT_TILE
    tiles = (counts + tm - 1) // tm
    tile_end = jnp.cumsum(tiles)
    group_start = (tile_end - tiles) * tm
    pos = (group_start[idx] + rank).reshape(-1).astype(jnp.int32)
    n_valid = tile_end[-1]
    tile_ids = jnp.minimum(jnp.arange(N_ROW_TILES, dtype=jnp.int32), n_valid - 1)
    tile_expert = jnp.sum(tile_ids[:, None] >= tile_end[None, :], axis=1).astype(jnp.int32)
    tile_first = (tile_end - tiles)[tile_expert]
    tile_rows = jnp.clip(counts[tile_expert] - (tile_ids - tile_first) * tm, 0, tm).astype(jnp.int32)
    sub = EXPERT_SUB
    pad_start = (group_start + counts).astype(jnp.int32)
    pad_len = ((counts + sub - 1) // sub * sub - counts).astype(jnp.int32)
    pad_blocks = ((tiles * tm - counts - pad_len) // sub).astype(jnp.int32)
    return pos, tile_expert, tile_rows, n_valid.reshape(1).astype(jnp.int32), pad_start, pad_len, pad_blocks


def kernel(x, p, mix_norm_g, ffn_norm_g, gmlp_w_in, gmlp_b_in, gmlp_ln_g, gmlp_ln_b, gmlp_w_s, gmlp_b_s, gmlp_w_out, gmlp_b_out, kv_norm_g, w_kv, b_kv, attn_w_q, attn_b_q, attn_sinks, attn_w_o, attn_b_o, ffn_w_gate, ffn_w_up, ffn_w_down, moe_w_router, moe_w_gate, moe_w_up, moe_w_down, ple_w_proj, ple_norm_g, ple_w_gate, final_norm_g):
    row = lambda a: a.reshape(1, -1)
    bf = lambda a: a.astype(BF16)
    xf = x.reshape(N_TOK, D_MODEL)
    pf = p.reshape(2, N_TOK, PLE_DIM)

    x1, moe_gate_bf, moe_up_bf = _gmlp_mixer(
        xf, row(mix_norm_g[0]), bf(gmlp_w_in[0]), row(gmlp_b_in[0]), row(gmlp_ln_g[0]), row(gmlp_ln_b[0]),
        gmlp_w_s[0], gmlp_b_s[0][:, :, None], bf(gmlp_w_out[0]), row(gmlp_b_out[0]), moe_w_gate[0], moe_w_up[0])

    col = lambda a: a.reshape(-1, 1)
    x3, qt, k, vt, moe_down_bf = _ffn_ple_qkv(x1, pf, _rope_tables(), row(ffn_norm_g[0]), bf(ffn_w_gate[0]),
                                 bf(ffn_w_up[0]), bf(ffn_w_down[0]), bf(ple_w_proj[0]), row(ple_norm_g[0]),
                                 bf(ple_w_gate[0]), row(mix_norm_g[1]), bf(attn_w_q[0].T), col(attn_b_q[0]),
                                 row(kv_norm_g), bf(w_kv[:, :KV_WIDTH]), row(b_kv[:KV_WIDTH]),
                                 bf(w_kv[:, KV_WIDTH:].T), col(b_kv[KV_WIDTH:]), moe_w_down[0])

    x4 = _swa_attention(attn_sinks[0], x3, qt, k, vt, bf(attn_w_o[0]), row(attn_b_o[0]))

    w_router = jnp.pad(moe_w_router[0], ((0, 0), (0, LANES - N_EXPERTS)))
    hn, idx, wts, rank, counts = _moe_router(x4, row(ffn_norm_g[1]), w_router)
    pos, tile_expert, tile_rows, n_valid, pad_start, pad_len, pad_blocks = _routing_plan(
        idx, rank, counts[0, :N_EXPERTS].astype(jnp.int32))

    xs = _moe_dispatch(pad_start, pad_len, pad_blocks, n_valid, pos, hn)
    expert_w = lambda a, k, n: a.reshape(N_EXPERTS, k, n)
    ys = _moe_experts(tile_expert, tile_rows, n_valid, xs, expert_w(moe_gate_bf, D_MODEL, D_FF_EXPERT),
                      expert_w(moe_up_bf, D_MODEL, D_FF_EXPERT), expert_w(moe_down_bf, D_FF_EXPERT, D_MODEL))
    out = _moe_combine_out(pos, x4, wts, pf, ys, bf(ple_w_proj[1]), row(ple_norm_g[1]), bf(ple_w_gate[1]),
                           row(final_norm_g))
    return out.reshape(BATCH, SEQ, D_MODEL)
```

```python
import functools
import math

import jax
import jax.numpy as jnp
from jax import lax
from jax.experimental import pallas as pl
from jax.experimental.pallas import tpu as pltpu

F32 = jnp.float32
BF16 = jnp.bfloat16

D_MODEL = 1024
BATCH = 4
SEQ = 4096
N_TOK = BATCH * SEQ

CHUNK = 128
GMLP_FFN = 6 * D_MODEL
GMLP_HALF = GMLP_FFN // 2
GMLP_GROUPS = 8
GMLP_GROUP_DIM = GMLP_HALF // GMLP_GROUPS

N_HEADS = 16
N_KV_HEADS = 4
HEAD_DIM = 64
KV_REP = N_HEADS // N_KV_HEADS
KV_WIDTH = N_KV_HEADS * HEAD_DIM
WINDOW = 128
ROPE_THETA = 10000.0

D_FF_DENSE = 2816
N_EXPERTS = 8
D_FF_EXPERT = 3584
PLE_DIM = 256

EPS = 1e-6
MASK_VALUE = -1e30

LANES = 128
SUBLANES = 8
assert D_MODEL == SUBLANES * LANES

GMLP_TILE = 256
FFN_TILE = 256
ATTN_TILE = 512
ROUTER_TILE = 512
DISPATCH_TILE = 1024
ROW_COPY_UNROLL = 8
EXPERT_TILE = 1024
EXPERT_SUB = 512
EXPERT_FF_BLOCK = 1792
COMBINE_TILE = 256

SORTED_ROWS = 2 * N_TOK + N_EXPERTS * EXPERT_TILE
N_ROW_TILES = SORTED_ROWS // EXPERT_TILE

VMEM_LIMIT = 56 * 1024 * 1024


def _ple_input(t, layer):
    return pl.BlockSpec((None, t, PLE_DIM), lambda i: (layer, i, 0))


def _resident(shape):
    zeros = (0,) * len(shape)
    return pl.BlockSpec(shape, lambda *_: zeros, pipeline_mode=pl.Buffered(1))


def _rms(x, g):
    return x * lax.rsqrt(jnp.mean(x * x, axis=-1, keepdims=True) + EPS) * g


def _gelu_tanh(x):
    c = math.sqrt(2.0 / math.pi)
    return 0.5 * x * (1.0 + jnp.tanh(c * (x + 0.044715 * (x * x * x))))


def _silu(x):
    return x * (1.0 / (1.0 + jnp.exp(-x)))


def _sigmoid(x):
    return 1.0 / (1.0 + jnp.exp(-x))


def _dot(a, b):
    return jnp.dot(a, b, preferred_element_type=F32)


def _store_token_tiles(ref, x):
    t = x.shape[0]
    for s in range(SUBLANES):
        ref[pl.ds(s, t, stride=SUBLANES), :] = x[:, s * LANES:(s + 1) * LANES]


def _tokens(ref, first, n):
    return ref.at[pl.ds(first * SUBLANES, n * SUBLANES)]


def _token(ref, r):
    return _tokens(ref, r, 1)


def _load_token_tiles(ref, t):
    return jnp.concatenate([ref[pl.ds(s, t, stride=SUBLANES), :] for s in range(SUBLANES)], axis=1)


def _gmlp_kernel(x_ref, g_ref, win_ref, bin_ref, lng_ref, lnb_ref, ws_ref, bs_ref,
                 wout_ref, bout_ref, *refs):
    n_cast = len(refs) // 2
    o_ref = refs[n_cast]
    for src_ref, dst_ref in zip(refs[:n_cast], refs[n_cast + 1:]):
        dst_ref[...] = src_ref[...].astype(BF16)

    x = x_ref[...]
    h = _rms(x, g_ref[...]).astype(BF16)
    pair_w = 2 * GMLP_GROUP_DIM
    n_pairs = GMLP_GROUPS // 2

    def u_product(pair):
        c0 = pair * pair_w
        return _dot(h, win_ref[:, c0:c0 + pair_w]) + bin_ref[:, c0:c0 + pair_w]

    v_pre = _dot(h, win_ref[:, GMLP_HALF:]) + bin_ref[:, GMLP_HALF:]
    u_pre = u_product(0)
    v = _gelu_tanh(v_pre)
    mu = jnp.mean(v, axis=-1, keepdims=True)
    vc = v - mu
    var = jnp.mean(vc * vc, axis=-1, keepdims=True)
    vn = (vc * lax.rsqrt(var + EPS) * lng_ref[...] + lnb_ref[...]).astype(BF16)

    row = lax.broadcasted_iota(jnp.int32, (CHUNK, CHUNK), 0)
    col = lax.broadcasted_iota(jnp.int32, (CHUNK, CHUNK), 1)
    causal = col <= row

    acc = x + bout_ref[...]
    for pair in range(n_pairs):
        c0 = pair * pair_w
        u_next = u_product(pair + 1) if pair + 1 < n_pairs else None
        u = _gelu_tanh(u_pre)
        parts = []
        for gi in range(2 * pair, 2 * pair + 2):
            ws = jnp.where(causal, ws_ref[gi], 0.0).astype(BF16)
            bs = bs_ref[gi]
            vg = vn[:, gi * GMLP_GROUP_DIM:(gi + 1) * GMLP_GROUP_DIM]
            rows = [_dot(ws, vg[c * CHUNK:(c + 1) * CHUNK]) + bs for c in range(GMLP_TILE // CHUNK)]
            parts.append(jnp.concatenate(rows, axis=0))
        mixed = jnp.concatenate(parts, axis=1)
        gated = (u * mixed).astype(BF16)
        acc = acc + _dot(gated, wout_ref[c0:c0 + pair_w, :])
        u_pre = u_next
    o_ref[...] = acc


BF16_ROWS = 16


def _cast_slab(rows, width, steps):
    slab = -(-rows // (steps * BF16_ROWS)) * BF16_ROWS
    while rows % slab:
        slab += BF16_ROWS
    last = rows // slab - 1
    return pl.BlockSpec((slab, width), lambda i: (jnp.minimum(i, last), 0))


def _gmlp_mixer(x, g, w_in, b_in, ln_g, ln_b, w_s, b_s, w_out, b_out, cast_weights):
    t = GMLP_TILE
    steps = N_TOK // t
    tok = pl.BlockSpec((t, D_MODEL), lambda i: (i, 0))
    slabs = [_cast_slab(w.shape[0], w.shape[1], steps) for w in cast_weights]
    return pl.pallas_call(
        _gmlp_kernel,
        grid=(steps,),
        in_specs=[tok, _resident((1, D_MODEL)), _resident((D_MODEL, GMLP_FFN)), _resident((1, GMLP_FFN)),
                  _resident((1, GMLP_HALF)), _resident((1, GMLP_HALF)),
                  _resident((GMLP_GROUPS, CHUNK, CHUNK)), _resident((GMLP_GROUPS, CHUNK, 1)),
                  _resident((GMLP_HALF, D_MODEL)), _resident((1, D_MODEL))] + slabs,
        out_specs=[tok] + slabs,
        out_shape=[jax.ShapeDtypeStruct((N_TOK, D_MODEL), F32)]
                  + [jax.ShapeDtypeStruct(w.shape, BF16) for w in cast_weights],
        compiler_params=pltpu.CompilerParams(dimension_semantics=("arbitrary",), vmem_limit_bytes=VMEM_LIMIT),
        name="gmlp_mixer",
    )(x, g, w_in, b_in, ln_g, ln_b, w_s, b_s, w_out, b_out, *cast_weights)


def _rope_slices(t, cos, sin_signed, first_half):
    outs = []
    for j in range(t.shape[1] // LANES):
        s = t[:, j * LANES:(j + 1) * LANES]
        partner = jnp.where(first_half, pltpu.roll(s, LANES - HEAD_DIM // 2, 1), pltpu.roll(s, HEAD_DIM // 2, 1))
        outs.append(s * cos + partner * sin_signed)
    return jnp.concatenate(outs, axis=1)


def _dot_nt(a, b):
    return lax.dot_general(a, b, (((1,), (1,)), ((), ())), preferred_element_type=F32)


def _ffn_ple_qkv_kernel(x_ref, p_ref, cos_ref, sin_ref, cost_ref, sint_ref, fg_ref, wg_ref, wu_ref, wd_ref,
                        wproj_ref, pg_ref, wgate_ref, mg_ref, wqt_ref, bq_ref, kg_ref, wk_ref, bk_ref,
                        wvt_ref, bv_ref, moe_d_ref, x_out, qt_out, k_out, vt_out, moe_db_ref):
    moe_db_ref[...] = moe_d_ref[...].astype(BF16)

    x = x_ref[...]
    hn = _rms(x, fg_ref[...]).astype(BF16)
    a = (_silu(_dot(hn, wg_ref[...])) * _dot(hn, wu_ref[...])).astype(BF16)
    x = x + _dot(a, wd_ref[...])
    gate = _sigmoid(_dot(_rms(x, pg_ref[...]).astype(BF16), wgate_ref[...]))
    x = x + _dot(p_ref[...].astype(BF16), wproj_ref[...]) * gate
    x_out[...] = x

    qt = _dot_nt(wqt_ref[...], _rms(x, mg_ref[...]).astype(BF16)) + bq_ref[...]
    cos_t = cost_ref[...]
    sin_t = sint_ref[...]
    half = HEAD_DIM // 2
    scale = 1.0 / math.sqrt(HEAD_DIM)
    for h in range(N_HEADS):
        t1 = qt[h * HEAD_DIM:h * HEAD_DIM + half]
        t2 = qt[h * HEAD_DIM + half:(h + 1) * HEAD_DIM]
        qt_out[h * HEAD_DIM:h * HEAD_DIM + half, :] = ((t1 * cos_t - t2 * sin_t) * scale).astype(BF16)
        qt_out[h * HEAD_DIM + half:(h + 1) * HEAD_DIM, :] = ((t2 * cos_t + t1 * sin_t) * scale).astype(BF16)

    hkv = _rms(x, kg_ref[...]).astype(BF16)
    lane = lax.broadcasted_iota(jnp.int32, cos_ref.shape, 1)
    first_half = (lane % HEAD_DIM) < half
    k = _rope_slices(_dot(hkv, wk_ref[...]) + bk_ref[...], cos_ref[...], sin_ref[...], first_half)
    k_out[...] = k.astype(BF16)
    vt_out[...] = (_dot_nt(wvt_ref[...], hkv) + bv_ref[...]).astype(BF16)


def _ffn_ple_qkv(x, p0, rope_tabs, fg, wg, wu, wd, wproj, pg, wgate, mg, wqt, bq_col, kg, wk, bk, wvt, bv_col,
                 moe_w_down):
    t = FFN_TILE
    moe_rows = N_EXPERTS * D_FF_EXPERT
    slab = _cast_slab(moe_rows, D_MODEL, N_TOK // t)
    per_seq = SEQ // t
    half = HEAD_DIM // 2
    tok = lambda w: pl.BlockSpec((t, w), lambda i: (i, 0))
    feat = lambda w: pl.BlockSpec((w, t), lambda i: (0, i))
    rope = pl.BlockSpec((t, LANES), lambda i: (i % per_seq, 0))
    rope_t = pl.BlockSpec((half, t), lambda i: (0, i % per_seq))
    cos_t, sin_t, cos_tt, sin_tt = rope_tabs
    return pl.pallas_call(
        _ffn_ple_qkv_kernel,
        grid=(N_TOK // t,),
        in_specs=[tok(D_MODEL), _ple_input(t, 0), rope, rope, rope_t, rope_t,
                  _resident((1, D_MODEL)), _resident((D_MODEL, D_FF_DENSE)), _resident((D_MODEL, D_FF_DENSE)),
                  _resident((D_FF_DENSE, D_MODEL)),
                  _resident((PLE_DIM, D_MODEL)), _resident((1, D_MODEL)), _resident((D_MODEL, D_MODEL)),
                  _resident((1, D_MODEL)), _resident((D_MODEL, D_MODEL)), _resident((D_MODEL, 1)),
                  _resident((1, D_MODEL)), _resident((D_MODEL, KV_WIDTH)), _resident((1, KV_WIDTH)),
                  _resident((KV_WIDTH, D_MODEL)), _resident((KV_WIDTH, 1)), slab],
        out_specs=[tok(D_MODEL), feat(D_MODEL), tok(KV_WIDTH), feat(KV_WIDTH), slab],
        out_shape=[jax.ShapeDtypeStruct((N_TOK, D_MODEL), F32), jax.ShapeDtypeStruct((D_MODEL, N_TOK), BF16),
                   jax.ShapeDtypeStruct((N_TOK, KV_WIDTH), BF16), jax.ShapeDtypeStruct((KV_WIDTH, N_TOK), BF16),
                   jax.ShapeDtypeStruct((moe_rows, D_MODEL), BF16)],
        compiler_params=pltpu.CompilerParams(dimension_semantics=("arbitrary",), vmem_limit_bytes=VMEM_LIMIT),
        name="ffn_ple_qkv",
    )(x, p0, cos_t, sin_t, cos_tt, sin_tt, fg, wg, wu, wd, wproj, pg, wgate, mg, wqt, bq_col, kg, wk, bk,
      wvt, bv_col, moe_w_down.reshape(moe_rows, D_MODEL))


def _attn_kernel(sink_ref, x_ref, qt_ref, kp_ref, kc_ref, vtp_ref, vtc_ref, wo_ref, bo_ref, o_ref, attnt_ref):
    i = pl.program_id(0)
    seq_start = (i % (SEQ // ATTN_TILE)) == 0
    kk = jnp.concatenate([kp_ref[...], kc_ref[...]], axis=0)
    vvt = jnp.concatenate([vtp_ref[...], vtc_ref[...]], axis=1)

    width = KV_REP * WINDOW
    kj = lax.broadcasted_iota(jnp.int32, (2 * WINDOW, width), 0)
    qi = lax.broadcasted_iota(jnp.int32, (2 * WINDOW, width), 1) % WINDOW
    band = (kj > qi) & (kj <= qi + WINDOW)
    first_band = band & ((kj >= WINDOW) | jnp.logical_not(seq_start))
    rep = lax.broadcasted_iota(jnp.int32, (1, width), 1) // WINDOW

    def sinks_of(kh):
        sink = jnp.zeros((1, width), F32)
        for r in range(KV_REP):
            sink = jnp.where(rep == r, sink_ref[kh * KV_REP + r], sink)
        return sink

    def scores(kh, sb):
        feats = slice(kh * HEAD_DIM, (kh + 1) * HEAD_DIM)
        toks = slice(sb * WINDOW, (sb + 1) * WINDOW)
        keys = slice(sb * WINDOW, sb * WINDOW + 2 * WINDOW)
        qt = jnp.concatenate(
            [qt_ref[(kh * KV_REP + r) * HEAD_DIM:(kh * KV_REP + r + 1) * HEAD_DIM, toks] for r in range(KV_REP)],
            axis=1)
        s = _dot(kk[keys, feats], qt)
        return jnp.where(first_band if sb == 0 else band, s, MASK_VALUE)

    def finish(kh, sb, s, sink):
        feats = slice(kh * HEAD_DIM, (kh + 1) * HEAD_DIM)
        toks = slice(sb * WINDOW, (sb + 1) * WINDOW)
        keys = slice(sb * WINDOW, sb * WINDOW + 2 * WINDOW)
        m = jnp.maximum(jnp.max(s, axis=0, keepdims=True), sink)
        pr = jnp.exp(s - m)
        den = jnp.sum(pr, axis=0, keepdims=True) + jnp.exp(sink - m)
        ot = _dot(vvt[feats, keys], pr.astype(BF16)) * (1.0 / den)
        for r in range(KV_REP):
            h = kh * KV_REP + r
            attnt_ref[h * HEAD_DIM:(h + 1) * HEAD_DIM, toks] = ot[:, r * WINDOW:(r + 1) * WINDOW].astype(BF16)

    units = [(kh, sb) for kh in range(N_KV_HEADS) for sb in range(ATTN_TILE // WINDOW)]
    s_cur = scores(*units[0])
    for n, (kh, sb) in enumerate(units):
        s_next = scores(*units[n + 1]) if n + 1 < len(units) else None
        finish(kh, sb, s_cur, sinks_of(kh))
        s_cur = s_next
    attn_out = lax.dot_general(attnt_ref[...], wo_ref[...], (((0,), (0,)), ((), ())), preferred_element_type=F32)
    o_ref[...] = x_ref[...] + attn_out + bo_ref[...]


def _swa_attention(sinks, x, qt, k, vt, wo, bo):
    t = ATTN_TILE
    blocks_per_tile = t // WINDOW
    tok = lambda w: pl.BlockSpec((t, w), lambda i: (i, 0))
    feat = lambda w: pl.BlockSpec((w, t), lambda i: (0, i))
    prev_block = lambda i: jnp.maximum(i * blocks_per_tile - 1, 0)
    k_prev = pl.BlockSpec((WINDOW, KV_WIDTH), lambda i: (prev_block(i), 0))
    vt_prev = pl.BlockSpec((KV_WIDTH, WINDOW), lambda i: (0, prev_block(i)))
    return pl.pallas_call(
        _attn_kernel,
        grid=(N_TOK // t,),
        in_specs=[pl.BlockSpec(memory_space=pltpu.SMEM), tok(D_MODEL), feat(D_MODEL), k_prev, tok(KV_WIDTH),
                  vt_prev, feat(KV_WIDTH), _resident((D_MODEL, D_MODEL)), _resident((1, D_MODEL))],
        out_specs=tok(D_MODEL),
        out_shape=jax.ShapeDtypeStruct((N_TOK, D_MODEL), F32),
        scratch_shapes=[pltpu.VMEM((D_MODEL, t), BF16)],
        compiler_params=pltpu.CompilerParams(dimension_semantics=("arbitrary",), vmem_limit_bytes=VMEM_LIMIT),
        name="swa_attention",
    )(sinks, x, qt, k, k, vt, vt, wo, bo)


def _router_kernel(x_ref, g_ref, wr_ref, hn_ref, idx_ref, wts_ref, rank_ref, cnt_ref):
    t = ROUTER_TILE

    @pl.when(pl.program_id(0) == 0)
    def _():
        cnt_ref[...] = jnp.zeros_like(cnt_ref)

    hn = _rms(x_ref[...], g_ref[...])
    _store_token_tiles(hn_ref, hn)
    hn_hi = hn.astype(BF16)
    hn_lo = (hn - hn_hi.astype(F32)).astype(BF16)
    wr = wr_ref[...]
    wr_hi = wr.astype(BF16)
    wr_lo = (wr - wr_hi.astype(F32)).astype(BF16)
    logits = _dot(hn_hi, wr_hi) + (_dot(hn_hi, wr_lo) + _dot(hn_lo, wr_hi))
    lane = lax.broadcasted_iota(jnp.int32, (t, LANES), 1)
    logits = jnp.where(lane < N_EXPERTS, logits, -jnp.inf)

    m1 = jnp.max(logits, axis=-1, keepdims=True)
    i1 = jnp.min(jnp.where(logits == m1, lane, LANES), axis=-1, keepdims=True)
    rest = jnp.where(lane == i1, -jnp.inf, logits)
    m2 = jnp.max(rest, axis=-1, keepdims=True)
    i2 = jnp.min(jnp.where(rest == m2, lane, LANES), axis=-1, keepdims=True)
    e2 = jnp.exp(m2 - m1)
    w1 = 1.0 / (1.0 + e2)
    w2 = e2 / (1.0 + e2)

    chosen = (lane == i1) | (lane == i2)
    onehot = jnp.where(chosen, 1.0, 0.0).astype(BF16)
    r = lax.broadcasted_iota(jnp.int32, (t, t), 0)
    c = lax.broadcasted_iota(jnp.int32, (t, t), 1)
    before = jnp.where(c < r, 1.0, 0.0).astype(BF16)
    seen = _dot(before, onehot) + cnt_ref[...]
    rank1 = jnp.sum(jnp.where(lane == i1, seen, 0.0), axis=-1, keepdims=True)
    rank2 = jnp.sum(jnp.where(lane == i2, seen, 0.0), axis=-1, keepdims=True)
    cnt_ref[...] = cnt_ref[...] + jnp.sum(onehot.astype(F32), axis=0, keepdims=True)

    idx_ref[...] = jnp.concatenate([i1, i2], axis=1)
    wts_ref[...] = jnp.concatenate([w1, w2], axis=1)
    rank_ref[...] = jnp.concatenate([rank1, rank2], axis=1).astype(jnp.int32)


def _moe_router(x, g, w_router_padded):
    t = ROUTER_TILE
    tok = lambda w: pl.BlockSpec((t, w), lambda i: (i, 0))
    return pl.pallas_call(
        _router_kernel,
        grid=(N_TOK // t,),
        in_specs=[tok(D_MODEL), _resident((1, D_MODEL)), _resident((D_MODEL, LANES))],
        out_specs=[pl.BlockSpec((t * SUBLANES, LANES), lambda i: (i, 0)), tok(2), tok(2), tok(2),
                   pl.BlockSpec((1, LANES), lambda i: (0, 0))],
        out_shape=[jax.ShapeDtypeStruct((N_TOK * SUBLANES, LANES), F32), jax.ShapeDtypeStruct((N_TOK, 2), jnp.int32),
                   jax.ShapeDtypeStruct((N_TOK, 2), F32), jax.ShapeDtypeStruct((N_TOK, 2), jnp.int32),
                   jax.ShapeDtypeStruct((1, LANES), F32)],
        compiler_params=pltpu.CompilerParams(dimension_semantics=("arbitrary",), vmem_limit_bytes=VMEM_LIMIT),
        name="moe_router",
    )(x, g, w_router_padded)


def _dispatch_kernel(pad_start_ref, pad_len_ref, pad_blocks_ref, nv_ref, pos_ref, hn_ref, xs_ref, zero_ref, sem):
    def row_copy(t, k):
        return pltpu.make_async_copy(_token(hn_ref, t), _token(xs_ref, pos_ref[2 * t + k]), sem)

    def issue(t, carry):
        row_copy(t, 0).start(priority=0)
        row_copy(t, 1).start(priority=1)
        return carry

    def drain(t, carry):
        row_copy(t, 0).wait()
        row_copy(t, 1).wait()
        return carry

    lax.fori_loop(0, DISPATCH_TILE, issue, 0, unroll=ROW_COPY_UNROLL)
    lax.fori_loop(0, DISPATCH_TILE, drain, 0, unroll=ROW_COPY_UNROLL)

    @pl.when(pl.program_id(0) == pl.num_programs(0) - 1)
    def _():
        zero_ref[...] = jnp.zeros_like(zero_ref)
        for e in range(N_EXPERTS):
            def zero_row(r, e=e):
                return pltpu.make_async_copy(_token(zero_ref, 0), _token(xs_ref, pad_start_ref[e] + r), sem)

            def row_issue(r, carry, zero_row=zero_row):
                zero_row(r).start()
                return carry

            def row_drain(r, carry, zero_row=zero_row):
                zero_row(r).wait()
                return carry

            lax.fori_loop(0, pad_len_ref[e], row_issue, 0)
            lax.fori_loop(0, pad_len_ref[e], row_drain, 0)

        def zero_block(first_token):
            return pltpu.make_async_copy(zero_ref, _tokens(xs_ref, first_token, EXPERT_SUB), sem)

        def fill_blocks(first_token, n_blocks):
            def block_issue(b, carry):
                zero_block(first_token + b * EXPERT_SUB).start()
                return carry

            def block_drain(b, carry):
                zero_block(first_token + b * EXPERT_SUB).wait()
                return carry

            lax.fori_loop(0, n_blocks, block_issue, 0)
            lax.fori_loop(0, n_blocks, block_drain, 0)

        for e in range(N_EXPERTS):
            fill_blocks(pad_start_ref[e] + pad_len_ref[e], pad_blocks_ref[e])
        fill_blocks(nv_ref[0] * EXPERT_TILE, (N_ROW_TILES - nv_ref[0]) * (EXPERT_TILE // EXPERT_SUB))


def _moe_dispatch(pad_start, pad_len, pad_blocks, n_valid, pos_flat, hn):
    t = DISPATCH_TILE
    grid_spec = pltpu.PrefetchScalarGridSpec(
        num_scalar_prefetch=4,
        grid=(N_TOK // t,),
        in_specs=[pl.BlockSpec((2 * t,), lambda i, ps, pn, pb, nv: (i,), memory_space=pltpu.SMEM),
                  pl.BlockSpec((t * SUBLANES, LANES), lambda i, ps, pn, pb, nv: (i, 0))],
        out_specs=pl.BlockSpec(memory_space=pl.ANY),
        scratch_shapes=[pltpu.VMEM((EXPERT_SUB * SUBLANES, LANES), F32), pltpu.SemaphoreType.DMA(())],
    )
    return pl.pallas_call(
        _dispatch_kernel,
        grid_spec=grid_spec,
        out_shape=jax.ShapeDtypeStruct((SORTED_ROWS * SUBLANES, LANES), F32),
        compiler_params=pltpu.CompilerParams(dimension_semantics=("arbitrary",), has_side_effects=True,
                                             vmem_limit_bytes=VMEM_LIMIT),
        name="moe_dispatch",
    )(pad_start, pad_len, pad_blocks, n_valid, pos_flat, hn)


def _experts_kernel(te_ref, nr_ref, nv_ref, xs_ref, wg_ref, wu_ref, wd_ref, ys_ref, acc_ref):
    del te_ref
    i = pl.program_id(0)
    j = pl.program_id(1)
    last = pl.num_programs(1) - 1
    sub = EXPERT_SUB
    n_rows = nr_ref[i]

    @pl.when((i == 0) & (j == 0))
    def _():
        acc_ref[...] = jnp.zeros_like(acc_ref)

    @pl.when(i < nv_ref[0])
    def _():
        for sb in range(EXPERT_TILE // sub):
            rows = slice(sb * sub, (sb + 1) * sub)
            tiles = lambda s, sb=sb: pl.ds(sb * sub * SUBLANES + s, sub, stride=SUBLANES)

            @pl.when(sb * sub < n_rows)
            def _(rows=rows, tiles=tiles):
                xb = jnp.concatenate([xs_ref[tiles(s), :].astype(BF16) for s in range(SUBLANES)], axis=1)
                a = (_silu(_dot(xb, wg_ref[...])) * _dot(xb, wu_ref[...])).astype(BF16)
                total = _dot(a, wd_ref[...]) + jnp.where(j == 0, 0.0, acc_ref[rows, :])
                acc_ref[rows, :] = total
                for s in range(SUBLANES):
                    ys_ref[tiles(s), :] = total[:, s * LANES:(s + 1) * LANES]

            @pl.when((sb * sub >= n_rows) & (j == last))
            def _(sb=sb):
                ys_ref[sb * sub * SUBLANES:(sb + 1) * sub * SUBLANES, :] = jnp.zeros((sub * SUBLANES, LANES), F32)

    @pl.when((i >= nv_ref[0]) & (j == last))
    def _():
        ys_ref[...] = jnp.zeros_like(ys_ref)


def _moe_experts(tile_expert, tile_rows, n_valid, xs, wg, wu, wd):
    tm, fb = EXPERT_TILE, EXPERT_FF_BLOCK
    n_fb = D_FF_EXPERT // fb
    assert n_fb >= 2
    rows = lambda i, j, te, nr, nv: jnp.maximum(jnp.minimum(i, nv[0] - 1), 0)
    ff = lambda i, j, te, nr, nv: jnp.where(i < nv[0], j, n_fb - 1)
    grid_spec = pltpu.PrefetchScalarGridSpec(
        num_scalar_prefetch=3,
        grid=(N_ROW_TILES, n_fb),
        in_specs=[pl.BlockSpec((tm * SUBLANES, LANES), lambda i, j, te, nr, nv: (rows(i, j, te, nr, nv), 0)),
                  pl.BlockSpec((None, D_MODEL, fb), lambda i, j, te, nr, nv: (te[i], 0, ff(i, j, te, nr, nv))),
                  pl.BlockSpec((None, D_MODEL, fb), lambda i, j, te, nr, nv: (te[i], 0, ff(i, j, te, nr, nv))),
                  pl.BlockSpec((None, fb, D_MODEL), lambda i, j, te, nr, nv: (te[i], ff(i, j, te, nr, nv), 0))],
        out_specs=pl.BlockSpec((tm * SUBLANES, LANES), lambda i, j, te, nr, nv: (i, 0)),
        scratch_shapes=[pltpu.VMEM((tm, D_MODEL), F32)],
    )
    return pl.pallas_call(
        _experts_kernel,
        grid_spec=grid_spec,
        out_shape=jax.ShapeDtypeStruct((SORTED_ROWS * SUBLANES, LANES), F32),
        compiler_params=pltpu.CompilerParams(dimension_semantics=("arbitrary", "arbitrary"),
                                             vmem_limit_bytes=VMEM_LIMIT),
        name="moe_experts",
    )(tile_expert, tile_rows, n_valid, xs, wg, wu, wd)


def _combine_kernel(pos_ref, pos_next_ref, x_ref, wts_ref, p_ref, ys_ref, wproj_ref, pg_ref, wgate_ref, fg_ref,
                    o_ref, rows_ref, sems):
    t = COMBINE_TILE
    i = pl.program_id(0)
    slot = i % 2

    def row_copy(pref, slot, r, k):
        return pltpu.make_async_copy(_token(ys_ref, pref[2 * r + k]), _token(rows_ref.at[slot, k], r),
                                     sems.at[slot])

    def issue(pref, slot):
        def body(r, carry):
            row_copy(pref, slot, r, 0).start(priority=0)
            row_copy(pref, slot, r, 1).start(priority=1)
            return carry
        lax.fori_loop(0, t, body, 0, unroll=ROW_COPY_UNROLL)

    def drain(pref, slot):
        def body(r, carry):
            row_copy(pref, slot, r, 0).wait()
            row_copy(pref, slot, r, 1).wait()
            return carry
        lax.fori_loop(0, t, body, 0, unroll=ROW_COPY_UNROLL)

    @pl.when(i == 0)
    def _():
        issue(pos_ref, 0)

    @pl.when(i + 1 < pl.num_programs(0))
    def _():
        issue(pos_next_ref, 1 - slot)

    drain(pos_ref, slot)

    half = t // 2
    ple = _dot(p_ref[...].astype(BF16), wproj_ref[...])
    xs, gate_in = [], []
    for hf in range(2):
        rows = slice(hf * half, (hf + 1) * half)
        w = wts_ref[rows, :]
        y0 = _load_token_tiles(_tokens(rows_ref.at[slot, 0], hf * half, half), half)
        y1 = _load_token_tiles(_tokens(rows_ref.at[slot, 1], hf * half, half), half)
        x = x_ref[rows, :] + w[:, 0:1] * y0 + w[:, 1:2] * y1
        xs.append(x)
        gate_in.append(_rms(x, pg_ref[...]).astype(BF16))
    for hf in range(2):
        rows = slice(hf * half, (hf + 1) * half)
        gate = _sigmoid(_dot(gate_in[hf], wgate_ref[...]))
        o_ref[rows, :] = _rms(xs[hf] + ple[rows, :] * gate, fg_ref[...])


def _moe_combine_out(pos_flat, x, wts, p1, ys, wproj, pg, wgate, fg):
    t = COMBINE_TILE
    n = N_TOK // t
    tok = lambda w: pl.BlockSpec((t, w), lambda i: (i, 0))
    return pl.pallas_call(
        _combine_kernel,
        grid=(n,),
        in_specs=[pl.BlockSpec((2 * t,), lambda i: (i,), memory_space=pltpu.SMEM),
                  pl.BlockSpec((2 * t,), lambda i: (jnp.minimum(i + 1, n - 1),), memory_space=pltpu.SMEM),
                  tok(D_MODEL), tok(2), _ple_input(t, 1), pl.BlockSpec(memory_space=pl.ANY),
                  _resident((PLE_DIM, D_MODEL)), _resident((1, D_MODEL)), _resident((D_MODEL, D_MODEL)),
                  _resident((1, D_MODEL))],
        out_specs=tok(D_MODEL),
        out_shape=jax.ShapeDtypeStruct((N_TOK, D_MODEL), F32),
        scratch_shapes=[pltpu.VMEM((2, 2, t * SUBLANES, LANES), F32), pltpu.SemaphoreType.DMA((2,))],
        compiler_params=pltpu.CompilerParams(dimension_semantics=("arbitrary",), vmem_limit_bytes=VMEM_LIMIT),
        name="moe_combine_out",
    )(pos_flat, pos_flat, x, wts, p1, ys, wproj, pg, wgate, fg)


def _rope_tables():
    half = HEAD_DIM // 2
    freqs = ROPE_THETA ** (-jnp.arange(0, HEAD_DIM, 2, dtype=F32) / HEAD_DIM)
    ang = jnp.arange(SEQ, dtype=F32)[:, None] * freqs[None, :]
    cos, sin = jnp.cos(ang), jnp.sin(ang)
    reps = LANES // HEAD_DIM
    cos_t = jnp.tile(jnp.concatenate([cos, cos], axis=1), (1, reps))
    sin_t = jnp.tile(jnp.concatenate([-sin, sin], axis=1), (1, reps))
    del half
    return cos_t, sin_t, cos.T, sin.T


def _routing_plan(idx, rank, counts):
    tm = EXPERT_TILE
    tiles = (counts + tm - 1) // tm
    tile_end = jnp.cumsum(tiles)
    group_start = (tile_end - tiles) * tm
    pos = (group_start[idx] + rank).reshape(-1).astype(jnp.int32)
    n_valid = tile_end[-1]
    tile_ids = jnp.minimum(jnp.arange(N_ROW_TILES, dtype=jnp.int32), n_valid - 1)
    tile_expert = jnp.sum(tile_ids[:, None] >= tile_end[None, :], axis=1).astype(jnp.int32)
    tile_first = (tile_end - tiles)[tile_expert]
    tile_rows = jnp.clip(counts[tile_expert] - (tile_ids - tile_first) * tm, 0, tm).astype(jnp.int32)
    sub = EXPERT_SUB
    pad_start = (group_start + counts).astype(jnp.int32)
    pad_len = ((counts + sub - 1) // sub * sub - counts).astype(jnp.int32)
    pad_blocks = ((tiles * tm - counts - pad_len) // sub).astype(jnp.int32)
    return pos, tile_expert, tile_rows, n_valid.reshape(1).astype(jnp.int32), pad_start, pad_len, pad_blocks


def kernel(x, p, mix_norm_g, ffn_norm_g, gmlp_w_in, gmlp_b_in, gmlp_ln_g, gmlp_ln_b, gmlp_w_s, gmlp_b_s, gmlp_w_out, gmlp_b_out, kv_norm_g, w_kv, b_kv, attn_w_q, attn_b_q, attn_sinks, attn_w_o, attn_b_o, ffn_w_gate, ffn_w_up, ffn_w_down, moe_w_router, moe_w_gate, moe_w_up, moe_w_down, ple_w_proj, ple_norm_g, ple_w_gate, final_norm_g):
    row = lambda a: a.reshape(1, -1)
    bf = lambda a: a.astype(BF16)
    xf = x.reshape(N_TOK, D_MODEL)
    pf = p.reshape(2, N_TOK, PLE_DIM)

    moe_rows = N_EXPERTS * D_MODEL
    x1, moe_gate_bf, moe_up_bf, ffn_gate_bf, ffn_up_bf, ffn_down_bf, ple_gate0_bf, ple_gate1_bf, wo_bf = _gmlp_mixer(
        xf, row(mix_norm_g[0]), bf(gmlp_w_in[0]), row(gmlp_b_in[0]), row(gmlp_ln_g[0]), row(gmlp_ln_b[0]),
        gmlp_w_s[0], gmlp_b_s[0][:, :, None], bf(gmlp_w_out[0]), row(gmlp_b_out[0]),
        [moe_w_gate[0].reshape(moe_rows, D_FF_EXPERT), moe_w_up[0].reshape(moe_rows, D_FF_EXPERT),
         ffn_w_gate[0], ffn_w_up[0], ffn_w_down[0], ple_w_gate[0], ple_w_gate[1], attn_w_o[0]])

    col = lambda a: a.reshape(-1, 1)
    x3, qt, k, vt, moe_down_bf = _ffn_ple_qkv(x1, pf, _rope_tables(), row(ffn_norm_g[0]), ffn_gate_bf,
                                 ffn_up_bf, ffn_down_bf, bf(ple_w_proj[0]), row(ple_norm_g[0]),
                                 ple_gate0_bf, row(mix_norm_g[1]), bf(attn_w_q[0].T), col(attn_b_q[0]),
                                 row(kv_norm_g), bf(w_kv[:, :KV_WIDTH]), row(b_kv[:KV_WIDTH]),
                                 bf(w_kv[:, KV_WIDTH:].T), col(b_kv[KV_WIDTH:]), moe_w_down[0])

    x4 = _swa_attention(attn_sinks[0], x3, qt, k, vt, wo_bf, row(attn_b_o[0]))

    w_router = jnp.pad(moe_w_router[0], ((0, 0), (0, LANES - N_EXPERTS)))
    hn, idx, wts, rank, counts = _moe_router(x4, row(ffn_norm_g[1]), w_router)
    pos, tile_expert, tile_rows, n_valid, pad_start, pad_len, pad_blocks = _routing_plan(
        idx, rank, counts[0, :N_EXPERTS].astype(jnp.int32))

    xs = _moe_dispatch(pad_start, pad_len, pad_blocks, n_valid, pos, hn)
    expert_w = lambda a, k, n: a.reshape(N_EXPERTS, k, n)
    ys = _moe_experts(tile_expert, tile_rows, n_valid, xs, expert_w(moe_gate_bf, D_MODEL, D_FF_EXPERT),
                      expert_w(moe_up_bf, D_MODEL, D_FF_EXPERT), expert_w(moe_down_bf, D_FF_EXPERT, D_MODEL))
    out = _moe_combine_out(pos, x4, wts, pf, ys, bf(ple_w_proj[1]), row(ple_norm_g[1]), ple_gate1_bf,
                           row(final_norm_g))
    return out.reshape(BATCH, SEQ, D_MODEL)
```

```python
import functools
import math

import jax
import jax.numpy as jnp
from jax import lax
from jax.experimental import pallas as pl
from jax.experimental.pallas import tpu as pltpu

F32 = jnp.float32
BF16 = jnp.bfloat16

D_MODEL = 1024
BATCH = 4
SEQ = 4096
N_TOK = BATCH * SEQ

CHUNK = 128
GMLP_FFN = 6 * D_MODEL
GMLP_HALF = GMLP_FFN // 2
GMLP_GROUPS = 8
GMLP_GROUP_DIM = GMLP_HALF // GMLP_GROUPS

N_HEADS = 16
N_KV_HEADS = 4
HEAD_DIM = 64
KV_REP = N_HEADS // N_KV_HEADS
KV_WIDTH = N_KV_HEADS * HEAD_DIM
WINDOW = 128
ROPE_THETA = 10000.0

D_FF_DENSE = 2816
N_EXPERTS = 8
D_FF_EXPERT = 3584
PLE_DIM = 256

EPS = 1e-6
MASK_VALUE = -1e30

LANES = 128
SUBLANES = 8
assert D_MODEL == SUBLANES * LANES

GMLP_TILE = 256
FFN_TILE = 256
ATTN_TILE = 512
ROUTER_TILE = 512
DISPATCH_TILE = 1024
ROW_COPY_UNROLL = 8
EXPERT_TILE = 1024
EXPERT_SUB = 512
EXPERT_FF_BLOCK = 1792
COMBINE_TILE = 256

SORTED_ROWS = 2 * N_TOK + N_EXPERTS * EXPERT_TILE
N_ROW_TILES = SORTED_ROWS // EXPERT_TILE

VMEM_LIMIT = 56 * 1024 * 1024


def _ple_input(t, layer):
    return pl.BlockSpec((None, t, PLE_DIM), lambda i: (layer, i, 0))


def _resident(shape):
    zeros = (0,) * len(shape)
    return pl.BlockSpec(shape, lambda *_: zeros, pipeline_mode=pl.Buffered(1))


def _rms(x, g):
    return x * lax.rsqrt(jnp.mean(x * x, axis=-1, keepdims=True) + EPS) * g


def _gelu_tanh(x):
    c = math.sqrt(2.0 / math.pi)
    return 0.5 * x * (1.0 + jnp.tanh(c * (x + 0.044715 * (x * x * x))))


def _silu(x):
    return x * (1.0 / (1.0 + jnp.exp(-x)))


def _sigmoid(x):
    return 1.0 / (1.0 + jnp.exp(-x))


def _dot(a, b):
    return jnp.dot(a, b, preferred_element_type=F32)


def _store_token_tiles(ref, x):
    t = x.shape[0]
    for s in range(SUBLANES):
        ref[pl.ds(s, t, stride=SUBLANES), :] = x[:, s * LANES:(s + 1) * LANES]


def _tokens(ref, first, n):
    return ref.at[pl.ds(first * SUBLANES, n * SUBLANES)]


def _token(ref, r):
    return _tokens(ref, r, 1)


def _load_token_tiles(ref, t):
    return jnp.concatenate([ref[pl.ds(s, t, stride=SUBLANES), :] for s in range(SUBLANES)], axis=1)


def _gmlp_kernel(x_ref, g_ref, win_ref, bin_ref, lng_ref, lnb_ref, ws_ref, bs_ref,
                 wout_ref, bout_ref, moe_g_ref, moe_u_ref, o_ref, moe_gb_ref, moe_ub_ref):
    moe_gb_ref[...] = moe_g_ref[...].astype(BF16)
    moe_ub_ref[...] = moe_u_ref[...].astype(BF16)

    x = x_ref[...]
    h = _rms(x, g_ref[...]).astype(BF16)
    pair_w = 2 * GMLP_GROUP_DIM
    n_pairs = GMLP_GROUPS // 2

    def u_product(pair):
        c0 = pair * pair_w
        return _dot(h, win_ref[:, c0:c0 + pair_w]) + bin_ref[:, c0:c0 + pair_w]

    v_pre = _dot(h, win_ref[:, GMLP_HALF:]) + bin_ref[:, GMLP_HALF:]
    u_pre = u_product(0)
    v = _gelu_tanh(v_pre)
    mu = jnp.mean(v, axis=-1, keepdims=True)
    vc = v - mu
    var = jnp.mean(vc * vc, axis=-1, keepdims=True)
    vn = (vc * lax.rsqrt(var + EPS) * lng_ref[...] + lnb_ref[...]).astype(BF16)

    row = lax.broadcasted_iota(jnp.int32, (CHUNK, CHUNK), 0)
    col = lax.broadcasted_iota(jnp.int32, (CHUNK, CHUNK), 1)
    causal = col <= row

    acc = x + bout_ref[...]
    for pair in range(n_pairs):
        c0 = pair * pair_w
        u_next = u_product(pair + 1) if pair + 1 < n_pairs else None
        u = _gelu_tanh(u_pre)
        parts = []
        for gi in range(2 * pair, 2 * pair + 2):
            ws = jnp.where(causal, ws_ref[gi], 0.0).astype(BF16)
            bs = bs_ref[gi]
            vg = vn[:, gi * GMLP_GROUP_DIM:(gi + 1) * GMLP_GROUP_DIM]
            rows = [_dot(ws, vg[c * CHUNK:(c + 1) * CHUNK]) + bs for c in range(GMLP_TILE // CHUNK)]
            parts.append(jnp.concatenate(rows, axis=0))
        mixed = jnp.concatenate(parts, axis=1)
        gated = (u * mixed).astype(BF16)
        acc = acc + _dot(gated, wout_ref[c0:c0 + pair_w, :])
        u_pre = u_next
    o_ref[...] = acc


def _cast_slab(rows, width, steps):
    assert rows % (steps * 16) == 0
    return pl.BlockSpec((rows // steps, width), lambda i: (i, 0))


def _gmlp_mixer(x, g, w_in, b_in, ln_g, ln_b, w_s, b_s, w_out, b_out, moe_w_gate, moe_w_up):
    t = GMLP_TILE
    steps = N_TOK // t
    tok = pl.BlockSpec((t, D_MODEL), lambda i: (i, 0))
    moe_rows = N_EXPERTS * D_MODEL
    slab = _cast_slab(moe_rows, D_FF_EXPERT, steps)
    moe_bf16 = jax.ShapeDtypeStruct((moe_rows, D_FF_EXPERT), BF16)
    return pl.pallas_call(
        _gmlp_kernel,
        grid=(steps,),
        in_specs=[tok, _resident((1, D_MODEL)), _resident((D_MODEL, GMLP_FFN)), _resident((1, GMLP_FFN)),
                  _resident((1, GMLP_HALF)), _resident((1, GMLP_HALF)),
                  _resident((GMLP_GROUPS, CHUNK, CHUNK)), _resident((GMLP_GROUPS, CHUNK, 1)),
                  _resident((GMLP_HALF, D_MODEL)), _resident((1, D_MODEL)), slab, slab],
        out_specs=[tok, slab, slab],
        out_shape=[jax.ShapeDtypeStruct((N_TOK, D_MODEL), F32), moe_bf16, moe_bf16],
        compiler_params=pltpu.CompilerParams(dimension_semantics=("arbitrary",), vmem_limit_bytes=VMEM_LIMIT),
        name="gmlp_mixer",
    )(x, g, w_in, b_in, ln_g, ln_b, w_s, b_s, w_out, b_out,
      moe_w_gate.reshape(moe_rows, D_FF_EXPERT), moe_w_up.reshape(moe_rows, D_FF_EXPERT))


def _rope_slices(t, cos, sin_signed, first_half):
    outs = []
    for j in range(t.shape[1] // LANES):
        s = t[:, j * LANES:(j + 1) * LANES]
        partner = jnp.where(first_half, pltpu.roll(s, LANES - HEAD_DIM // 2, 1), pltpu.roll(s, HEAD_DIM // 2, 1))
        outs.append(s * cos + partner * sin_signed)
    return jnp.concatenate(outs, axis=1)


def _dot_nt(a, b):
    return lax.dot_general(a, b, (((1,), (1,)), ((), ())), preferred_element_type=F32)


def _ffn_ple_qkv_kernel(x_ref, p_ref, cos_ref, sin_ref, cost_ref, sint_ref, fg_ref, wg_ref, wu_ref, wd_ref,
                        wproj_ref, pg_ref, wgate_ref, mg_ref, wqt_ref, bq_ref, kg_ref, wk_ref, bk_ref,
                        wvt_ref, bv_ref, moe_d_ref, x_out, qt_out, k_out, vt_out, moe_db_ref):
    moe_db_ref[...] = moe_d_ref[...].astype(BF16)

    x = x_ref[...]
    hn = _rms(x, fg_ref[...]).astype(BF16)
    a = (_silu(_dot(hn, wg_ref[...])) * _dot(hn, wu_ref[...])).astype(BF16)
    x = x + _dot(a, wd_ref[...])
    gate = _sigmoid(_dot(_rms(x, pg_ref[...]).astype(BF16), wgate_ref[...]))
    x = x + _dot(p_ref[...].astype(BF16), wproj_ref[...]) * gate
    x_out[...] = x

    qt = _dot_nt(wqt_ref[...], _rms(x, mg_ref[...]).astype(BF16)) + bq_ref[...]
    cos_t = cost_ref[...]
    sin_t = sint_ref[...]
    half = HEAD_DIM // 2
    scale = 1.0 / math.sqrt(HEAD_DIM)
    for h in range(N_HEADS):
        t1 = qt[h * HEAD_DIM:h * HEAD_DIM + half]
        t2 = qt[h * HEAD_DIM + half:(h + 1) * HEAD_DIM]
        qt_out[h * HEAD_DIM:h * HEAD_DIM + half, :] = ((t1 * cos_t - t2 * sin_t) * scale).astype(BF16)
        qt_out[h * HEAD_DIM + half:(h + 1) * HEAD_DIM, :] = ((t2 * cos_t + t1 * sin_t) * scale).astype(BF16)

    hkv = _rms(x, kg_ref[...]).astype(BF16)
    lane = lax.broadcasted_iota(jnp.int32, cos_ref.shape, 1)
    first_half = (lane % HEAD_DIM) < half
    k = _rope_slices(_dot(hkv, wk_ref[...]) + bk_ref[...], cos_ref[...], sin_ref[...], first_half)
    k_out[...] = k.astype(BF16)
    vt_out[...] = (_dot_nt(wvt_ref[...], hkv) + bv_ref[...]).astype(BF16)


def _ffn_ple_qkv(x, p0, rope_tabs, fg, wg, wu, wd, wproj, pg, wgate, mg, wqt, bq_col, kg, wk, bk, wvt, bv_col,
                 moe_w_down):
    t = FFN_TILE
    moe_rows = N_EXPERTS * D_FF_EXPERT
    slab = _cast_slab(moe_rows, D_MODEL, N_TOK // t)
    per_seq = SEQ // t
    half = HEAD_DIM // 2
    tok = lambda w: pl.BlockSpec((t, w), lambda i: (i, 0))
    feat = lambda w: pl.BlockSpec((w, t), lambda i: (0, i))
    rope = pl.BlockSpec((t, LANES), lambda i: (i % per_seq, 0))
    rope_t = pl.BlockSpec((half, t), lambda i: (0, i % per_seq))
    cos_t, sin_t, cos_tt, sin_tt = rope_tabs
    return pl.pallas_call(
        _ffn_ple_qkv_kernel,
        grid=(N_TOK // t,),
        in_specs=[tok(D_MODEL), _ple_input(t, 0), rope, rope, rope_t, rope_t,
                  _resident((1, D_MODEL)), _resident((D_MODEL, D_FF_DENSE)), _resident((D_MODEL, D_FF_DENSE)),
                  _resident((D_FF_DENSE, D_MODEL)),
                  _resident((PLE_DIM, D_MODEL)), _resident((1, D_MODEL)), _resident((D_MODEL, D_MODEL)),
                  _resident((1, D_MODEL)), _resident((D_MODEL, D_MODEL)), _resident((D_MODEL, 1)),
                  _resident((1, D_MODEL)), _resident((D_MODEL, KV_WIDTH)), _resident((1, KV_WIDTH)),
                  _resident((KV_WIDTH, D_MODEL)), _resident((KV_WIDTH, 1)), slab],
        out_specs=[tok(D_MODEL), feat(D_MODEL), tok(KV_WIDTH), feat(KV_WIDTH), slab],
        out_shape=[jax.ShapeDtypeStruct((N_TOK, D_MODEL), F32), jax.ShapeDtypeStruct((D_MODEL, N_TOK), BF16),
                   jax.ShapeDtypeStruct((N_TOK, KV_WIDTH), BF16), jax.ShapeDtypeStruct((KV_WIDTH, N_TOK), BF16),
                   jax.ShapeDtypeStruct((moe_rows, D_MODEL), BF16)],
        compiler_params=pltpu.CompilerParams(dimension_semantics=("arbitrary",), vmem_limit_bytes=VMEM_LIMIT),
        name="ffn_ple_qkv",
    )(x, p0, cos_t, sin_t, cos_tt, sin_tt, fg, wg, wu, wd, wproj, pg, wgate, mg, wqt, bq_col, kg, wk, bk,
      wvt, bv_col, moe_w_down.reshape(moe_rows, D_MODEL))


def _attn_kernel(sink_ref, x_ref, qt_ref, kp_ref, kc_ref, vtp_ref, vtc_ref, wo_ref, bo_ref, o_ref, attnt_ref):
    i = pl.program_id(0)
    seq_start = (i % (SEQ // ATTN_TILE)) == 0
    kk = jnp.concatenate([kp_ref[...], kc_ref[...]], axis=0)
    vvt = jnp.concatenate([vtp_ref[...], vtc_ref[...]], axis=1)

    width = KV_REP * WINDOW
    kj = lax.broadcasted_iota(jnp.int32, (2 * WINDOW, width), 0)
    qi = lax.broadcasted_iota(jnp.int32, (2 * WINDOW, width), 1) % WINDOW
    band = (kj > qi) & (kj <= qi + WINDOW)
    first_band = band & ((kj >= WINDOW) | jnp.logical_not(seq_start))
    rep = lax.broadcasted_iota(jnp.int32, (1, width), 1) // WINDOW

    def sinks_of(kh):
        sink = jnp.zeros((1, width), F32)
        for r in range(KV_REP):
            sink = jnp.where(rep == r, sink_ref[kh * KV_REP + r], sink)
        return sink

    def scores(kh, sb):
        feats = slice(kh * HEAD_DIM, (kh + 1) * HEAD_DIM)
        toks = slice(sb * WINDOW, (sb + 1) * WINDOW)
        keys = slice(sb * WINDOW, sb * WINDOW + 2 * WINDOW)
        qt = jnp.concatenate(
            [qt_ref[(kh * KV_REP + r) * HEAD_DIM:(kh * KV_REP + r + 1) * HEAD_DIM, toks] for r in range(KV_REP)],
            axis=1)
        s = _dot(kk[keys, feats], qt)
        return jnp.where(first_band if sb == 0 else band, s, MASK_VALUE)

    def finish(kh, sb, s, sink):
        feats = slice(kh * HEAD_DIM, (kh + 1) * HEAD_DIM)
        toks = slice(sb * WINDOW, (sb + 1) * WINDOW)
        keys = slice(sb * WINDOW, sb * WINDOW + 2 * WINDOW)
        m = jnp.maximum(jnp.max(s, axis=0, keepdims=True), sink)
        pr = jnp.exp(s - m)
        den = jnp.sum(pr, axis=0, keepdims=True) + jnp.exp(sink - m)
        ot = _dot(vvt[feats, keys], pr.astype(BF16)) * (1.0 / den)
        for r in range(KV_REP):
            h = kh * KV_REP + r
            attnt_ref[h * HEAD_DIM:(h + 1) * HEAD_DIM, toks] = ot[:, r * WINDOW:(r + 1) * WINDOW].astype(BF16)

    units = [(kh, sb) for kh in range(N_KV_HEADS) for sb in range(ATTN_TILE // WINDOW)]
    s_cur = scores(*units[0])
    for n, (kh, sb) in enumerate(units):
        s_next = scores(*units[n + 1]) if n + 1 < len(units) else None
        finish(kh, sb, s_cur, sinks_of(kh))
        s_cur = s_next
    attn_out = lax.dot_general(attnt_ref[...], wo_ref[...], (((0,), (0,)), ((), ())), preferred_element_type=F32)
    o_ref[...] = x_ref[...] + attn_out + bo_ref[...]


def _swa_attention(sinks, x, qt, k, vt, wo, bo):
    t = ATTN_TILE
    blocks_per_tile = t // WINDOW
    tok = lambda w: pl.BlockSpec((t, w), lambda i: (i, 0))
    feat = lambda w: pl.BlockSpec((w, t), lambda i: (0, i))
    prev_block = lambda i: jnp.maximum(i * blocks_per_tile - 1, 0)
    k_prev = pl.BlockSpec((WINDOW, KV_WIDTH), lambda i: (prev_block(i), 0))
    vt_prev = pl.BlockSpec((KV_WIDTH, WINDOW), lambda i: (0, prev_block(i)))
    return pl.pallas_call(
        _attn_kernel,
        grid=(N_TOK // t,),
        in_specs=[pl.BlockSpec(memory_space=pltpu.SMEM), tok(D_MODEL), feat(D_MODEL), k_prev, tok(KV_WIDTH),
                  vt_prev, feat(KV_WIDTH), _resident((D_MODEL, D_MODEL)), _resident((1, D_MODEL))],
        out_specs=tok(D_MODEL),
        out_shape=jax.ShapeDtypeStruct((N_TOK, D_MODEL), F32),
        scratch_shapes=[pltpu.VMEM((D_MODEL, t), BF16)],
        compiler_params=pltpu.CompilerParams(dimension_semantics=("arbitrary",), vmem_limit_bytes=VMEM_LIMIT),
        name="swa_attention",
    )(sinks, x, qt, k, k, vt, vt, wo, bo)


def _router_kernel(x_ref, g_ref, wr_ref, hn_ref, idx_ref, wts_ref, rank_ref, cnt_ref):
    t = ROUTER_TILE

    @pl.when(pl.program_id(0) == 0)
    def _():
        cnt_ref[...] = jnp.zeros_like(cnt_ref)

    hn = _rms(x_ref[...], g_ref[...])
    _store_token_tiles(hn_ref, hn)
    hn_hi = hn.astype(BF16)
    hn_lo = (hn - hn_hi.astype(F32)).astype(BF16)
    wr = wr_ref[...]
    wr_hi = wr.astype(BF16)
    wr_lo = (wr - wr_hi.astype(F32)).astype(BF16)
    logits = _dot(hn_hi, wr_hi) + (_dot(hn_hi, wr_lo) + _dot(hn_lo, wr_hi))
    lane = lax.broadcasted_iota(jnp.int32, (t, LANES), 1)
    logits = jnp.where(lane < N_EXPERTS, logits, -jnp.inf)

    m1 = jnp.max(logits, axis=-1, keepdims=True)
    i1 = jnp.min(jnp.where(logits == m1, lane, LANES), axis=-1, keepdims=True)
    rest = jnp.where(lane == i1, -jnp.inf, logits)
    m2 = jnp.max(rest, axis=-1, keepdims=True)
    i2 = jnp.min(jnp.where(rest == m2, lane, LANES), axis=-1, keepdims=True)
    e2 = jnp.exp(m2 - m1)
    w1 = 1.0 / (1.0 + e2)
    w2 = e2 / (1.0 + e2)

    chosen = (lane == i1) | (lane == i2)
    onehot = jnp.where(chosen, 1.0, 0.0).astype(BF16)
    r = lax.broadcasted_iota(jnp.int32, (t, t), 0)
    c = lax.broadcasted_iota(jnp.int32, (t, t), 1)
    before = jnp.where(c < r, 1.0, 0.0).astype(BF16)
    seen = _dot(before, onehot) + cnt_ref[...]
    rank1 = jnp.sum(jnp.where(lane == i1, seen, 0.0), axis=-1, keepdims=True)
    rank2 = jnp.sum(jnp.where(lane == i2, seen, 0.0), axis=-1, keepdims=True)
    cnt_ref[...] = cnt_ref[...] + jnp.sum(onehot.astype(F32), axis=0, keepdims=True)

    idx_ref[...] = jnp.concatenate([i1, i2], axis=1)
    wts_ref[...] = jnp.concatenate([w1, w2], axis=1)
    rank_ref[...] = jnp.concatenate([rank1, rank2], axis=1).astype(jnp.int32)


def _moe_router(x, g, w_router_padded):
    t = ROUTER_TILE
    tok = lambda w: pl.BlockSpec((t, w), lambda i: (i, 0))
    return pl.pallas_call(
        _router_kernel,
        grid=(N_TOK // t,),
        in_specs=[tok(D_MODEL), _resident((1, D_MODEL)), _resident((D_MODEL, LANES))],
        out_specs=[pl.BlockSpec((t * SUBLANES, LANES), lambda i: (i, 0)), tok(2), tok(2), tok(2),
                   pl.BlockSpec((1, LANES), lambda i: (0, 0))],
        out_shape=[jax.ShapeDtypeStruct((N_TOK * SUBLANES, LANES), F32), jax.ShapeDtypeStruct((N_TOK, 2), jnp.int32),
                   jax.ShapeDtypeStruct((N_TOK, 2), F32), jax.ShapeDtypeStruct((N_TOK, 2), jnp.int32),
                   jax.ShapeDtypeStruct((1, LANES), F32)],
        compiler_params=pltpu.CompilerParams(dimension_semantics=("arbitrary",), vmem_limit_bytes=VMEM_LIMIT),
        name="moe_router",
    )(x, g, w_router_padded)


def _dispatch_kernel(pad_start_ref, pad_len_ref, pad_blocks_ref, nv_ref, pos_ref, hn_ref, xs_ref, zero_ref, sem):
    def row_copy(t, k):
        return pltpu.make_async_copy(_token(hn_ref, t), _token(xs_ref, pos_ref[2 * t + k]), sem)

    def issue(t, carry):
        row_copy(t, 0).start(priority=0)
        row_copy(t, 1).start(priority=1)
        return carry

    def drain(t, carry):
        row_copy(t, 0).wait()
        row_copy(t, 1).wait()
        return carry

    lax.fori_loop(0, DISPATCH_TILE, issue, 0, unroll=ROW_COPY_UNROLL)
    lax.fori_loop(0, DISPATCH_TILE, drain, 0, unroll=ROW_COPY_UNROLL)

    @pl.when(pl.program_id(0) == pl.num_programs(0) - 1)
    def _():
        zero_ref[...] = jnp.zeros_like(zero_ref)
        for e in range(N_EXPERTS):
            def zero_row(r, e=e):
                return pltpu.make_async_copy(_token(zero_ref, 0), _token(xs_ref, pad_start_ref[e] + r), sem)

            def row_issue(r, carry, zero_row=zero_row):
                zero_row(r).start()
                return carry

            def row_drain(r, carry, zero_row=zero_row):
                zero_row(r).wait()
                return carry

            lax.fori_loop(0, pad_len_ref[e], row_issue, 0)
            lax.fori_loop(0, pad_len_ref[e], row_drain, 0)

        def zero_block(first_token):
            return pltpu.make_async_copy(zero_ref, _tokens(xs_ref, first_token, EXPERT_SUB), sem)

        def fill_blocks(first_token, n_blocks):
            def block_issue(b, carry):
                zero_block(first_token + b * EXPERT_SUB).start()
                return carry

            def block_drain(b, carry):
                zero_block(first_token + b * EXPERT_SUB).wait()
                return carry

            lax.fori_loop(0, n_blocks, block_issue, 0)
            lax.fori_loop(0, n_blocks, block_drain, 0)

        for e in range(N_EXPERTS):
            fill_blocks(pad_start_ref[e] + pad_len_ref[e], pad_blocks_ref[e])
        fill_blocks(nv_ref[0] * EXPERT_TILE, (N_ROW_TILES - nv_ref[0]) * (EXPERT_TILE // EXPERT_SUB))


def _moe_dispatch(pad_start, pad_len, pad_blocks, n_valid, pos_flat, hn):
    t = DISPATCH_TILE
    grid_spec = pltpu.PrefetchScalarGridSpec(
        num_scalar_prefetch=4,
        grid=(N_TOK // t,),
        in_specs=[pl.BlockSpec((2 * t,), lambda i, ps, pn, pb, nv: (i,), memory_space=pltpu.SMEM),
                  pl.BlockSpec((t * SUBLANES, LANES), lambda i, ps, pn, pb, nv: (i, 0))],
        out_specs=pl.BlockSpec(memory_space=pl.ANY),
        scratch_shapes=[pltpu.VMEM((EXPERT_SUB * SUBLANES, LANES), F32), pltpu.SemaphoreType.DMA(())],
    )
    return pl.pallas_call(
        _dispatch_kernel,
        grid_spec=grid_spec,
        out_shape=jax.ShapeDtypeStruct((SORTED_ROWS * SUBLANES, LANES), F32),
        compiler_params=pltpu.CompilerParams(dimension_semantics=("arbitrary",), has_side_effects=True,
                                             vmem_limit_bytes=VMEM_LIMIT),
        name="moe_dispatch",
    )(pad_start, pad_len, pad_blocks, n_valid, pos_flat, hn)


def _experts_kernel(te_ref, nr_ref, nv_ref, xs_ref, wg_ref, wu_ref, wd_ref, ys_ref, acc_ref):
    del te_ref
    i = pl.program_id(0)
    j = pl.program_id(1)
    last = pl.num_programs(1) - 1
    sub = EXPERT_SUB
    n_rows = nr_ref[i]

    @pl.when((i == 0) & (j == 0))
    def _():
        acc_ref[...] = jnp.zeros_like(acc_ref)

    def sub_block(first_row, n):
        rows = slice(first_row, first_row + n)
        tiles = lambda s: pl.ds(first_row * SUBLANES + s, n, stride=SUBLANES)
        xb = jnp.concatenate([xs_ref[tiles(s), :].astype(BF16) for s in range(SUBLANES)], axis=1)
        a = (_silu(_dot(xb, wg_ref[...])) * _dot(xb, wu_ref[...])).astype(BF16)
        total = _dot(a, wd_ref[...]) + jnp.where(j == 0, 0.0, acc_ref[rows, :])
        acc_ref[rows, :] = total
        for s in range(SUBLANES):
            ys_ref[tiles(s), :] = total[:, s * LANES:(s + 1) * LANES]

    def zero_rows(first_row, n):
        ys_ref[first_row * SUBLANES:(first_row + n) * SUBLANES, :] = jnp.zeros((n * SUBLANES, LANES), F32)

    @pl.when(i < nv_ref[0])
    def _():
        half = sub // 2
        for sb in range(EXPERT_TILE // sub):
            left = n_rows - sb * sub

            @pl.when(left > half)
            def _(sb=sb):
                sub_block(sb * sub, sub)

            @pl.when((left > 0) & (left <= half))
            def _(sb=sb):
                sub_block(sb * sub, half)

                @pl.when(j == last)
                def _():
                    zero_rows(sb * sub + half, half)

            @pl.when((left <= 0) & (j == last))
            def _(sb=sb):
                zero_rows(sb * sub, sub)

    @pl.when((i >= nv_ref[0]) & (j == last))
    def _():
        ys_ref[...] = jnp.zeros_like(ys_ref)


def _moe_experts(tile_expert, tile_rows, n_valid, xs, wg, wu, wd):
    tm, fb = EXPERT_TILE, EXPERT_FF_BLOCK
    n_fb = D_FF_EXPERT // fb
    assert n_fb >= 2
    rows = lambda i, j, te, nr, nv: jnp.maximum(jnp.minimum(i, nv[0] - 1), 0)
    ff = lambda i, j, te, nr, nv: jnp.where(i < nv[0], j, n_fb - 1)
    grid_spec = pltpu.PrefetchScalarGridSpec(
        num_scalar_prefetch=3,
        grid=(N_ROW_TILES, n_fb),
        in_specs=[pl.BlockSpec((tm * SUBLANES, LANES), lambda i, j, te, nr, nv: (rows(i, j, te, nr, nv), 0)),
                  pl.BlockSpec((None, D_MODEL, fb), lambda i, j, te, nr, nv: (te[i], 0, ff(i, j, te, nr, nv))),
                  pl.BlockSpec((None, D_MODEL, fb), lambda i, j, te, nr, nv: (te[i], 0, ff(i, j, te, nr, nv))),
                  pl.BlockSpec((None, fb, D_MODEL), lambda i, j, te, nr, nv: (te[i], ff(i, j, te, nr, nv), 0))],
        out_specs=pl.BlockSpec((tm * SUBLANES, LANES), lambda i, j, te, nr, nv: (i, 0)),
        scratch_shapes=[pltpu.VMEM((tm, D_MODEL), F32)],
    )
    return pl.pallas_call(
        _experts_kernel,
        grid_spec=grid_spec,
        out_shape=jax.ShapeDtypeStruct((SORTED_ROWS * SUBLANES, LANES), F32),
        compiler_params=pltpu.CompilerParams(dimension_semantics=("arbitrary", "arbitrary"),
                                             vmem_limit_bytes=VMEM_LIMIT),
        name="moe_experts",
    )(tile_expert, tile_rows, n_valid, xs, wg, wu, wd)


def _combine_kernel(pos_ref, pos_next_ref, x_ref, wts_ref, p_ref, ys_ref, wproj_ref, pg_ref, wgate_ref, fg_ref,
                    o_ref, rows_ref, sems):
    t = COMBINE_TILE
    i = pl.program_id(0)
    slot = i % 2

    def row_copy(pref, slot, r, k):
        return pltpu.make_async_copy(_token(ys_ref, pref[2 * r + k]), _token(rows_ref.at[slot, k], r),
                                     sems.at[slot])

    def issue(pref, slot):
        def body(r, carry):
            row_copy(pref, slot, r, 0).start(priority=0)
            row_copy(pref, slot, r, 1).start(priority=1)
            return carry
        lax.fori_loop(0, t, body, 0, unroll=ROW_COPY_UNROLL)

    def drain(pref, slot):
        def body(r, carry):
            row_copy(pref, slot, r, 0).wait()
            row_copy(pref, slot, r, 1).wait()
            return carry
        lax.fori_loop(0, t, body, 0, unroll=ROW_COPY_UNROLL)

    @pl.when(i == 0)
    def _():
        issue(pos_ref, 0)

    @pl.when(i + 1 < pl.num_programs(0))
    def _():
        issue(pos_next_ref, 1 - slot)

    drain(pos_ref, slot)

    half = t // 2
    ple = _dot(p_ref[...].astype(BF16), wproj_ref[...])
    xs, gate_in = [], []
    for hf in range(2):
        rows = slice(hf * half, (hf + 1) * half)
        w = wts_ref[rows, :]
        y0 = _load_token_tiles(_tokens(rows_ref.at[slot, 0], hf * half, half), half)
        y1 = _load_token_tiles(_tokens(rows_ref.at[slot, 1], hf * half, half), half)
        x = x_ref[rows, :] + w[:, 0:1] * y0 + w[:, 1:2] * y1
        xs.append(x)
        gate_in.append(_rms(x, pg_ref[...]).astype(BF16))
    for hf in range(2):
        rows = slice(hf * half, (hf + 1) * half)
        gate = _sigmoid(_dot(gate_in[hf], wgate_ref[...]))
        o_ref[rows, :] = _rms(xs[hf] + ple[rows, :] * gate, fg_ref[...])


def _moe_combine_out(pos_flat, x, wts, p1, ys, wproj, pg, wgate, fg):
    t = COMBINE_TILE
    n = N_TOK // t
    tok = lambda w: pl.BlockSpec((t, w), lambda i: (i, 0))
    return pl.pallas_call(
        _combine_kernel,
        grid=(n,),
        in_specs=[pl.BlockSpec((2 * t,), lambda i: (i,), memory_space=pltpu.SMEM),
                  pl.BlockSpec((2 * t,), lambda i: (jnp.minimum(i + 1, n - 1),), memory_space=pltpu.SMEM),
                  tok(D_MODEL), tok(2), _ple_input(t, 1), pl.BlockSpec(memory_space=pl.ANY),
                  _resident((PLE_DIM, D_MODEL)), _resident((1, D_MODEL)), _resident((D_MODEL, D_MODEL)),
                  _resident((1, D_MODEL))],
        out_specs=tok(D_MODEL),
        out_shape=jax.ShapeDtypeStruct((N_TOK, D_MODEL), F32),
        scratch_shapes=[pltpu.VMEM((2, 2, t * SUBLANES, LANES), F32), pltpu.SemaphoreType.DMA((2,))],
        compiler_params=pltpu.CompilerParams(dimension_semantics=("arbitrary",), vmem_limit_bytes=VMEM_LIMIT),
        name="moe_combine_out",
    )(pos_flat, pos_flat, x, wts, p1, ys, wproj, pg, wgate, fg)


def _rope_tables():
    half = HEAD_DIM // 2
    freqs = ROPE_THETA ** (-jnp.arange(0, HEAD_DIM, 2, dtype=F32) / HEAD_DIM)
    ang = jnp.arange(SEQ, dtype=F32)[:, None] * freqs[None, :]
    cos, sin = jnp.cos(ang), jnp.sin(ang)
    reps = LANES // HEAD_DIM
    cos_t = jnp.tile(jnp.concatenate([cos, cos], axis=1), (1, reps))
    sin_t = jnp.tile(jnp.concatenate([-sin, sin], axis=1), (1, reps))
    del half
    return cos_t, sin_t, cos.T, sin.T


def _routing_plan(idx, rank, counts):
    tm = EXPERT_TILE
    tiles = (counts + tm - 1) // tm
    tile_end = jnp.cumsum(tiles)
    group_start = (tile_end - tiles) * tm
    pos = (group_start[idx] + rank).reshape(-1).astype(jnp.int32)
    n_valid = tile_end[-1]
    tile_ids = jnp.minimum(jnp.arange(N_ROW_TILES, dtype=jnp.int32), n_valid - 1)
    tile_expert = jnp.sum(tile_ids[:, None] >= tile_end[None, :], axis=1).astype(jnp.int32)
    tile_first = (tile_end - tiles)[tile_expert]
    tile_rows = jnp.clip(counts[tile_expert] - (tile_ids - tile_first) * tm, 0, tm).astype(jnp.int32)
    sub = EXPERT_SUB
    pad_start = (group_start + counts).astype(jnp.int32)
    pad_len = ((counts + sub - 1) // sub * sub - counts).astype(jnp.int32)
    pad_blocks = ((tiles * tm - counts - pad_len) // sub).astype(jnp.int32)
    return pos, tile_expert, tile_rows, n_valid.reshape(1).astype(jnp.int32), pad_start, pad_len, pad_blocks


def kernel(x, p, mix_norm_g, ffn_norm_g, gmlp_w_in, gmlp_b_in, gmlp_ln_g, gmlp_ln_b, gmlp_w_s, gmlp_b_s, gmlp_w_out, gmlp_b_out, kv_norm_g, w_kv, b_kv, attn_w_q, attn_b_q, attn_sinks, attn_w_o, attn_b_o, ffn_w_gate, ffn_w_up, ffn_w_down, moe_w_router, moe_w_gate, moe_w_up, moe_w_down, ple_w_proj, ple_norm_g, ple_w_gate, final_norm_g):
    row = lambda a: a.reshape(1, -1)
    bf = lambda a: a.astype(BF16)
    xf = x.reshape(N_TOK, D_MODEL)
    pf = p.reshape(2, N_TOK, PLE_DIM)

    x1, moe_gate_bf, moe_up_bf = _gmlp_mixer(
        xf, row(mix_norm_g[0]), bf(gmlp_w_in[0]), row(gmlp_b_in[0]), row(gmlp_ln_g[0]), row(gmlp_ln_b[0]),
        gmlp_w_s[0], gmlp_b_s[0][:, :, None], bf(gmlp_w_out[0]), row(gmlp_b_out[0]), moe_w_gate[0], moe_w_up[0])

    col = lambda a: a.reshape(-1, 1)
    x3, qt, k, vt, moe_down_bf = _ffn_ple_qkv(x1, pf, _rope_tables(), row(ffn_norm_g[0]), bf(ffn_w_gate[0]),
                                 bf(ffn_w_up[0]), bf(ffn_w_down[0]), bf(ple_w_proj[0]), row(ple_norm_g[0]),
                                 bf(ple_w_gate[0]), row(mix_norm_g[1]), bf(attn_w_q[0].T), col(attn_b_q[0]),
                                 row(kv_norm_g), bf(w_kv[:, :KV_WIDTH]), row(b_kv[:KV_WIDTH]),
                                 bf(w_kv[:, KV_WIDTH:].T), col(b_kv[KV_WIDTH:]), moe_w_down[0])

    x4 = _swa_attention(attn_sinks[0], x3, qt, k, vt, bf(attn_w_o[0]), row(attn_b_o[0]))

    w_router = jnp.pad(moe_w_router[0], ((0, 0), (0, LANES - N_EXPERTS)))
    hn, idx, wts, rank, counts = _moe_router(x4, row(ffn_norm_g[1]), w_router)
    pos, tile_expert, tile_rows, n_valid, pad_start, pad_len, pad_blocks = _routing_plan(
        idx, rank, counts[0, :N_EXPERTS].astype(jnp.int32))

    xs = _moe_dispatch(pad_start, pad_len, pad_blocks, n_valid, pos, hn)
    expert_w = lambda a, k, n: a.reshape(N_EXPERTS, k, n)
    ys = _moe_experts(tile_expert, tile_rows, n_valid, xs, expert_w(moe_gate_bf, D_MODEL, D_FF_EXPERT),
                      expert_w(moe_up_bf, D_MODEL, D_FF_EXPERT), expert_w(moe_down_bf, D_FF_EXPERT, D_MODEL))
    out = _moe_combine_out(pos, x4, wts, pf, ys, bf(ple_w_proj[1]), row(ple_norm_g[1]), bf(ple_w_gate[1]),
                           row(final_norm_g))
    return out.reshape(BATCH, SEQ, D_MODEL)
```
